```python
import math
import jax
import jax.numpy as jnp
from jax import lax
import numpy as np

D_MODEL = 2048
BATCH = 1
SEQ = 8192
DEPTH = 2
DEC_BATCH = 32
DEC_SEQ = 1
PAST_LEN = 8192
PAGE_SIZE = 128

N_EVEN = (DEPTH + 1) // 2
N_ODD = DEPTH // 2
ALPHA = (2 * DEPTH) ** 0.25
BETA = (8 * DEPTH) ** -0.25
LN_EPS = 1e-5
RMS_EPS = 1e-6
D_FF = 5632

NSA_HEADS = 8
NSA_KV = 2
NSA_GROUP = NSA_HEADS // NSA_KV
HD = 128
L_CMP = 32
D_CMP = 16
CMP_HIDDEN = 128
L_SLC = 64
N_SEL = 16
WINDOW = 512
Q_BLOCK = 128
NEG_INF = -1e30
FORCE_SCORE = 1e6

DN_HEADS = 8
DK = 128
DV = 128
CONV_W = 4
DN_CHUNK = 64
DN_QKV = DN_HEADS * (2 * DK + DV)

RWKV_HS = 64
RWKV_HEADS = D_MODEL // RWKV_HS
LORA_W = 96
LORA_A = 96
LORA_G = 256
GN_EPS = 64e-5

A_Q = NSA_HEADS * HD
A_KVW = 2 * NSA_KV * HD
A_GATES = 3 * NSA_HEADS
EVEN_SPLIT = (A_Q, A_KVW, A_KVW, A_KVW, A_GATES, DN_QKV, DN_HEADS * DV, DN_HEADS, DN_HEADS)
E_IN = A_Q + 3 * A_KVW + A_GATES + DN_QKV + DN_HEADS * DV + 2 * DN_HEADS
E_MIX = A_Q + DN_HEADS * DV

kernel_name = 'hybrid_nsa_gdn_rwkv7_macaron_deepnorm_step'


def _layernorm(x, g, b):
    xf = x.astype(jnp.float32)
    mu = jnp.mean(xf, -1, keepdims=True)
    var = jnp.mean(jnp.square(xf - mu), -1, keepdims=True)
    return ((xf - mu) * lax.rsqrt(var + LN_EPS) * g + b).astype(x.dtype)


def _post_ln(x, sub, g, b):
    return _layernorm(ALPHA * x + sub, g, b)


def _swiglu(x, wi, wo):
    gate, up = jnp.split(x @ wi, 2, axis=-1)
    return (jax.nn.silu(gate) * up) @ wo


def _l2norm(x):
    return x * lax.rsqrt(jnp.sum(x * x, -1, keepdims=True) + 1e-6)


def _alibi_slopes(n):
    return jnp.asarray(2.0 ** (-8.0 * np.arange(1, n + 1) / n), dtype=jnp.float32)


def _masked_softmax(logits, mask, axis):
    p = jax.nn.softmax(jnp.where(mask, logits, NEG_INF), axis=axis)
    return jnp.where(mask, p, 0.0)


def _split_even(h):
    offs = np.cumsum(EVEN_SPLIT)[:-1].tolist()
    return jnp.split(h, offs, axis=-1)


def _nsa_compress(kv, pe, w1, b1, w2):
    B, T = kv.shape[:2]
    nc = (T - L_CMP) // D_CMP + 1
    nch = -(-T // D_CMP)
    kv = jnp.pad(kv, ((0, 0), (0, nch * D_CMP - T), (0, 0), (0, 0), (0, 0)))
    ch = kv.reshape(B, nch, D_CMP, 2, NSA_KV, HD)
    w1h = w1.reshape(2, 2, D_CMP, HD, CMP_HIDDEN)
    peh = pe.reshape(2, 2, D_CMP, HD)
    h = jnp.einsum('bcpsgd,shpde->bcsghe', ch, w1h)
    h = h + jnp.einsum('shpd,shpde->she', peh, w1h)[None, None, :, None]
    hid = h[:, :nc, :, :, 0] + h[:, 1:nc + 1, :, :, 1] + b1[None, None, :, None]
    return jnp.einsum('bnsge,sed->bnsgd', jax.nn.gelu(hid), w2)


def _to_blocks(kv):
    B, T = kv.shape[:2]
    ns = -(-T // L_SLC)
    kv = jnp.pad(kv, ((0, 0), (0, ns * L_SLC - T), (0, 0), (0, 0), (0, 0)))
    return kv.reshape(B, ns, L_SLC, 2, NSA_KV, HD).transpose(0, 4, 1, 2, 3, 5)


def _nsa_core(q, qpos, gates, cmp, slc_blocks, kv_win, kwpos):
    f32 = jnp.float32
    B, Tq = q.shape[:2]
    qg = q.reshape(B, Tq, NSA_KV, NSA_GROUP, HD)
    m = _alibi_slopes(NSA_HEADS).reshape(1, NSA_KV, NSA_GROUP, 1, 1)
    qp = qpos[:, None]
    nc = cmp.shape[1]
    cstart = jnp.arange(nc) * D_CMP
    cmask = cstart[None, :] + (L_CMP - 1) <= qp
    cdist = (qp - cstart[None, :]).astype(f32) - 0.5 * (L_CMP - 1)
    lc = jnp.einsum('bqgjd,bngd->bgjqn', qg, cmp[:, :, 0]).astype(f32) - m * cdist
    pc = _masked_softmax(lc, cmask, -1)
    o_cmp = jnp.einsum('bgjqn,bngd->bqgjd', pc.astype(q.dtype), cmp[:, :, 1])
    ns = slc_blocks.shape[2]
    n_pick = min(N_SEL, ns)
    sstart = jnp.arange(ns) * L_SLC
    overlap = ((cstart[:, None] < sstart[None, :] + L_SLC) & (cstart[:, None] + L_CMP > sstart[None, :])).astype(f32)
    imp = jnp.einsum('bgjqn,ns->bgqs', pc, overlap)
    sid = jnp.arange(ns)[None, :]
    cur = qp // L_SLC
    svalid = sstart[None, :] <= qp
    forced = (sid == 0) | (sid == cur) | (sid == cur - 1)
    score = jnp.where(svalid, jnp.where(forced, FORCE_SCORE, imp), NEG_INF)
    top_s, top_i = lax.top_k(score, n_pick)
    bi = jnp.arange(B)[:, None, None, None]
    gi = jnp.arange(NSA_KV)[None, :, None, None]
    sel = slc_blocks[bi, gi, top_i]
    spos = top_i[..., None] * L_SLC + jnp.arange(L_SLC)
    smask = (top_s > 0.5 * NEG_INF)[..., None] & (spos <= qpos[:, None, None])
    sdist = (qpos[:, None, None] - spos).astype(f32)
    ls = jnp.einsum('bqgjd,bgqnld->bgjqnl', qg, sel[..., 0, :]).astype(f32) - m[..., None] * sdist[:, :, None]
    ps = _masked_softmax(ls, smask[:, :, None], (-2, -1))
    o_slc = jnp.einsum('bgjqnl,bgqnld->bqgjd', ps.astype(q.dtype), sel[..., 1, :])
    wd = qp - kwpos[None, :]
    wmask = (wd >= 0) & (wd <= WINDOW) & (kwpos[None, :] >= 0)
    lw = jnp.einsum('bqgjd,bkgd->bgjqk', qg, kv_win[:, :, 0]).astype(f32) - m * wd.astype(f32)
    pw = _masked_softmax(lw, wmask, -1)
    o_win = jnp.einsum('bgjqk,bkgd->bqgjd', pw.astype(q.dtype), kv_win[:, :, 1])
    gt = gates.reshape(B, Tq, NSA_KV, NSA_GROUP, 3)
    o = gt[..., 0:1] * o_cmp + gt[..., 1:2] * o_slc + gt[..., 2:3] * o_win
    return o.reshape(B, Tq, NSA_HEADS * HD)


def _nsa_attend(q, gates, kvc, kvs, kvw, cmp_p, past):
    B, T = q.shape[:2]
    if past is None:
        cmp = _nsa_compress(kvc, *cmp_p)
        slc_blocks = _to_blocks(kvs)
        kvw_pad = jnp.pad(kvw, ((0, 0), (WINDOW, 0), (0, 0), (0, 0), (0, 0)))

        def block(i):
            s0 = i * Q_BLOCK
            qpos = s0 + jnp.arange(Q_BLOCK)
            kwpos = s0 - WINDOW + jnp.arange(WINDOW + Q_BLOCK)
            return _nsa_core(lax.dynamic_slice_in_dim(q, s0, Q_BLOCK, 1), qpos,
                             lax.dynamic_slice_in_dim(gates, s0, Q_BLOCK, 1), cmp, slc_blocks,
                             lax.dynamic_slice_in_dim(kvw_pad, s0, WINDOW + Q_BLOCK, 1), kwpos)

        o = lax.map(block, jnp.arange(T // Q_BLOCK))
        o = jnp.moveaxis(o, 0, 1).reshape(B, T, NSA_HEADS * HD)
        win_new = kvw[:, T - min(WINDOW, T):]
    else:
        cache_c, cache_s, win_buf, page_table = past
        past_len = page_table.shape[1] * cache_c.shape[1]

        def gather(pool):
            return pool[page_table].reshape(B, past_len, 2, NSA_KV, HD)

        cmp = _nsa_compress(jnp.concatenate([gather(cache_c), kvc], 1), *cmp_p)
        slc_blocks = _to_blocks(jnp.concatenate([gather(cache_s), kvs], 1))
        kvw_all = jnp.concatenate([win_buf, kvw], 1)
        nb = win_buf.shape[1]
        kwpos = past_len - nb + jnp.arange(nb + T)
        qpos = past_len + jnp.arange(T)
        o = _nsa_core(q, qpos, gates, cmp, slc_blocks, kvw_all, kwpos)
        win_new = kvw_all[:, T:]
    return o, win_new


def _short_conv(x, buf, w):
    xp = jnp.concatenate([buf.astype(x.dtype), x], axis=1)
    y = lax.conv_general_dilated(xp, w.T[:, None, :].astype(x.dtype), window_strides=(1,), padding='VALID',
                                 dimension_numbers=('NWC', 'WIO', 'NWC'), feature_group_count=x.shape[-1])
    return jax.nn.silu(y), xp[:, xp.shape[1] - (CONV_W - 1):]


def _gated_delta_chunked(q, k, v, g, beta, S0):
    B, T, H, _ = q.shape
    nc = -(-T // DN_CHUNK)
    pad = nc * DN_CHUNK - T

    def chunks(a):
        a = jnp.pad(a, [(0, 0), (0, pad)] + [(0, 0)] * (a.ndim - 2))
        a = a.reshape((B, nc, DN_CHUNK) + a.shape[2:])
        return jnp.moveaxis(jnp.moveaxis(a, 1, 0), 3, 2)

    qc, kc, vc, gc, bc = chunks(q), chunks(k), chunks(v), chunks(g), chunks(beta)
    gcum = jnp.cumsum(gc, axis=-1)
    lower = jnp.tril(jnp.ones((DN_CHUNK, DN_CHUNK), dtype=bool))
    strict = jnp.tril(jnp.ones((DN_CHUNK, DN_CHUNK), dtype=bool), -1)
    diff = gcum[..., :, None] - gcum[..., None, :]
    decay = jnp.where(lower, jnp.exp(jnp.where(lower, diff, 0.0)), 0.0)
    kb = kc * bc[..., None]
    a_mat = jnp.where(strict, jnp.einsum('nbhik,nbhjk->nbhij', kb, kc) * decay, 0.0)
    rhs = jnp.concatenate([vc * bc[..., None], kb * jnp.exp(gcum)[..., None]], axis=-1)
    sol = lax.linalg.triangular_solve(a_mat + jnp.eye(DN_CHUNK, dtype=q.dtype), rhs,
                                      left_side=True, lower=True, unit_diagonal=True)
    val, kcd = sol[..., :DV], sol[..., DV:]

    def step(S, inp):
        qi, ki, gi, vi, kdi, di = inp
        v_new = vi - jnp.einsum('bhik,bhkv->bhiv', kdi, S)
        attn = jnp.einsum('bhik,bhjk->bhij', qi, ki) * di
        o = jnp.einsum('bhik,bhkv->bhiv', qi * jnp.exp(gi)[..., None], S) + jnp.einsum('bhij,bhjv->bhiv', attn, v_new)
        gl = gi[..., -1:]
        S = S * jnp.exp(gl)[..., None] + jnp.einsum('bhik,bhiv->bhkv', ki * jnp.exp(gl - gi)[..., None], v_new)
        return S, o

    S, o = lax.scan(step, S0, (qc, kc, gcum, val, kcd, decay))
    o = jnp.moveaxis(jnp.moveaxis(o, 2, 3), 0, 1).reshape(B, nc * DN_CHUNK, H, DV)[:, :T]
    return o, S


def _deltanet(qkv_raw, z, b_raw, a_raw, conv_buf, S0, conv_w, a_log, dt_bias, norm_w):
    f32 = jnp.float32
    B, T, _ = qkv_raw.shape
    qkv, conv_new = _short_conv(qkv_raw, conv_buf, conv_w)
    q, k, v = jnp.split(qkv.astype(f32), [DN_HEADS * DK, 2 * DN_HEADS * DK], axis=-1)
    q = _l2norm(q.reshape(B, T, DN_HEADS, DK)) * DK ** -0.5
    k = _l2norm(k.reshape(B, T, DN_HEADS, DK))
    v = v.reshape(B, T, DN_HEADS, DV)
    beta = jax.nn.sigmoid(b_raw.astype(f32))
    g = -jnp.exp(a_log.astype(f32)) * jax.nn.softplus(a_raw.astype(f32) + dt_bias.astype(f32))
    o, S = _gated_delta_chunked(q, k, v, g, beta, S0.astype(f32))
    o = o * lax.rsqrt(jnp.mean(o * o, -1, keepdims=True) + RMS_EPS) * norm_w.astype(f32)
    o = o * jax.nn.silu(z.astype(f32).reshape(B, T, DN_HEADS, DV))
    return o.reshape(B, T, DN_HEADS * DV).astype(qkv_raw.dtype), conv_new, S


def _even_mixer(x, w_in, w_out, cmp_p, dn_p, conv_buf, S0, past):
    B, T, _ = x.shape
    qa, kvc, kvs, kvw, ga, qkv, z, b, a = _split_even(x @ w_in)
    kv_shape = (B, T, 2, NSA_KV, HD)
    qa = qa.reshape(B, T, NSA_HEADS, HD) * HD ** -0.5
    ga = jax.nn.sigmoid(ga).reshape(B, T, NSA_HEADS, 3)
    kvc, kvs, kvw = kvc.reshape(kv_shape), kvs.reshape(kv_shape), kvw.reshape(kv_shape)
    o_a, win_new = _nsa_attend(qa, ga, kvc, kvs, kvw, cmp_p, past)
    o_b, conv_new, S_new = _deltanet(qkv, z, b, a, conv_buf, S0, *dn_p)
    y = jnp.concatenate([o_a, o_b], axis=-1) @ w_out
    return y, kvc, kvs, win_new, conv_new, S_new


def _rwkv7(x, shift_prev, S0, mix, wr, wk, wv, wo, w0, w1, w2, a0, a1, a2, g1, g2, k_k, k_a, r_k, ln_w, ln_b):
    f32 = jnp.float32
    B, T, D = x.shape
    xprev = jnp.concatenate([shift_prev[:, None].astype(x.dtype), x[:, :-1]], axis=1)
    xx = xprev - x
    xr, xw, xk, xv, xa, xg = (x + xx * mix[i] for i in range(6))
    r = xr @ wr
    k = xk @ wk
    v = xv @ wv
    w_log = -jax.nn.softplus(-(w0 + jnp.tanh(xw @ w1) @ w2).astype(f32)) - 0.5
    decay = jnp.exp(-jnp.exp(w_log))
    a = jax.nn.sigmoid((a0 + (xa @ a1) @ a2).astype(f32))
    gate = (jax.nn.sigmoid(xg @ g1) @ g2).astype(f32)

    def hs(t):
        return t.astype(f32).reshape(B, T, RWKV_HEADS, RWKV_HS)

    kk = hs(k * k_k)
    kk = kk / jnp.maximum(jnp.sqrt(jnp.sum(kk * kk, -1, keepdims=True)), 1e-12)
    k_h = hs(k.astype(f32) * (1.0 + (a - 1.0) * k_a.astype(f32)))
    r_h, v_h, w_h, a_h = hs(r), hs(v), hs(decay), hs(a)

    def step(S, inp):
        r_t, w_t, k_t, v_t, aa_t, bb_t = inp
        sa = jnp.einsum('bhij,bhj->bhi', S, aa_t)
        S = S * w_t[:, :, None, :] + sa[..., None] * bb_t[:, :, None, :] + v_t[..., None] * k_t[:, :, None, :]
        return S, jnp.einsum('bhij,bhj->bhi', S, r_t)

    seq = lambda t: jnp.moveaxis(t, 1, 0)
    S, out = lax.scan(step, S0.astype(f32), (seq(r_h), seq(w_h), seq(k_h), seq(v_h), seq(-kk), seq(kk * a_h)))
    out = jnp.moveaxis(out, 0, 1)
    mu = jnp.mean(out, -1, keepdims=True)
    var = jnp.mean(jnp.square(out - mu), -1, keepdims=True)
    out = ((out - mu) * lax.rsqrt(var + GN_EPS)).reshape(B, T, D) * ln_w + ln_b
    bonus = jnp.sum(r_h * k_h * r_k, -1, keepdims=True) * v_h
    out = out + bonus.reshape(B, T, D)
    y = (out * gate).astype(x.dtype) @ wo
    return y, x[:, -1], S


def setup_inputs(seed: int = 0) -> dict:
    key = jax.random.key(seed)
    kit = iter(list(jax.random.split(key, 64)))
    f32 = jnp.float32

    def nrm(shape, scale):
        return jax.random.normal(next(kit), shape, f32) * scale

    def unif(shape, lo, hi):
        return jax.random.uniform(next(kit), shape, f32, lo, hi)

    n_pages = PAST_LEN // PAGE_SIZE
    n_pool = (5 * DEC_BATCH * n_pages) // 4
    win_buf = min(WINDOW, PAST_LEN)
    page_table = jax.random.permutation(next(kit), n_pool)[: DEC_BATCH * n_pages].reshape(DEC_BATCH, n_pages).astype(jnp.int32)
    dt = jnp.exp(unif((N_EVEN, DN_HEADS), math.log(1e-3), math.log(1e-1)))
    return {
        'x_prompt': nrm((BATCH, SEQ, D_MODEL), 1.0),
        'x_sample': nrm((DEC_BATCH, DEC_SEQ, D_MODEL), 1.0),
        'cache_nsa_cmp': nrm((N_EVEN, n_pool, PAGE_SIZE, 2, NSA_KV, HD), 1.0),
        'cache_nsa_slc': nrm((N_EVEN, n_pool, PAGE_SIZE, 2, NSA_KV, HD), 1.0),
        'cache_nsa_win': nrm((N_EVEN, DEC_BATCH, win_buf, 2, NSA_KV, HD), 1.0),
        'state_dn_conv': nrm((N_EVEN, DEC_BATCH, CONV_W - 1, DN_QKV), 1.0),
        'state_dn_S': nrm((N_EVEN, DEC_BATCH, DN_HEADS, DK, DV), 0.1),
        'state_rwkv_shift': nrm((N_ODD, DEC_BATCH, D_MODEL), 1.0),
        'state_rwkv_S': nrm((N_ODD, DEC_BATCH, RWKV_HEADS, RWKV_HS, RWKV_HS), 0.1),
        'page_table': page_table,
        'ln_g': 1.0 + nrm((DEPTH, 3, D_MODEL), 0.02),
        'ln_b': nrm((DEPTH, 3, D_MODEL), 0.02),
        'ffn_wi': nrm((DEPTH, 2, D_MODEL, 2 * D_FF), D_MODEL ** -0.5),
        'ffn_wo': nrm((DEPTH, 2, D_FF, D_MODEL), BETA * D_FF ** -0.5),
        'mix_w_in': nrm((N_EVEN, D_MODEL, E_IN), D_MODEL ** -0.5),
        'mix_w_out': nrm((N_EVEN, E_MIX, D_MODEL), BETA * E_MIX ** -0.5),
        'nsa_cmp_pe': nrm((N_EVEN, 2, L_CMP, HD), 0.1),
        'nsa_cmp_w1': nrm((N_EVEN, 2, L_CMP, HD, CMP_HIDDEN), (L_CMP * HD) ** -0.5),
        'nsa_cmp_b1': nrm((N_EVEN, 2, CMP_HIDDEN), 0.02),
        'nsa_cmp_w2': nrm((N_EVEN, 2, CMP_HIDDEN, HD), CMP_HIDDEN ** -0.5),
        'dn_conv_w': nrm((N_EVEN, DN_QKV, CONV_W), CONV_W ** -0.5),
        'dn_a_log': jnp.log(unif((N_EVEN, DN_HEADS), 1.0, 16.0)),
        'dn_dt_bias': dt + jnp.log(-jnp.expm1(-dt)),
        'dn_norm_w': 1.0 + nrm((N_EVEN, DV), 0.02),
        'rwkv_mix': unif((N_ODD, 6, D_MODEL), 0.0, 1.0),
        'rwkv_wr': nrm((N_ODD, D_MODEL, D_MODEL), D_MODEL ** -0.5),
        'rwkv_wk': nrm((N_ODD, D_MODEL, D_MODEL), D_MODEL ** -0.5),
        'rwkv_wv': nrm((N_ODD, D_MODEL, D_MODEL), D_MODEL ** -0.5),
        'rwkv_wo': nrm((N_ODD, D_MODEL, D_MODEL), BETA * D_MODEL ** -0.5),
        'rwkv_w0': unif((N_ODD, D_MODEL), -6.0, 0.0),
        'rwkv_w1': nrm((N_ODD, D_MODEL, LORA_W), D_MODEL ** -0.5),
        'rwkv_w2': nrm((N_ODD, LORA_W, D_MODEL), 0.1 * LORA_W ** -0.5),
        'rwkv_a0': nrm((N_ODD, D_MODEL), 0.1),
        'rwkv_a1': nrm((N_ODD, D_MODEL, LORA_A), D_MODEL ** -0.5),
        'rwkv_a2': nrm((N_ODD, LORA_A, D_MODEL), 0.1 * LORA_A ** -0.5),
        'rwkv_g1': nrm((N_ODD, D_MODEL, LORA_G), D_MODEL ** -0.5),
        'rwkv_g2': nrm((N_ODD, LORA_G, D_MODEL), LORA_G ** -0.5),
        'rwkv_k_k': 0.85 + nrm((N_ODD, D_MODEL), 0.02),
        'rwkv_k_a': 1.0 + nrm((N_ODD, D_MODEL), 0.02),
        'rwkv_r_k': -0.04 + nrm((N_ODD, RWKV_HEADS, RWKV_HS), 0.02),
        'rwkv_ln_w': 1.0 + nrm((N_ODD, D_MODEL), 0.02),
        'rwkv_ln_b': nrm((N_ODD, D_MODEL), 0.02),
    }


def reference(x_prompt, x_sample, cache_nsa_cmp, cache_nsa_slc, cache_nsa_win, state_dn_conv, state_dn_S,
              state_rwkv_shift, state_rwkv_S, page_table, ln_g, ln_b, ffn_wi, ffn_wo, mix_w_in, mix_w_out,
              nsa_cmp_pe, nsa_cmp_w1, nsa_cmp_b1, nsa_cmp_w2, dn_conv_w, dn_a_log, dn_dt_bias, dn_norm_w,
              rwkv_mix, rwkv_wr, rwkv_wk, rwkv_wv, rwkv_wo, rwkv_w0, rwkv_w1, rwkv_w2, rwkv_a0, rwkv_a1,
              rwkv_a2, rwkv_g1, rwkv_g2, rwkv_k_k, rwkv_k_a, rwkv_r_k, rwkv_ln_w, rwkv_ln_b):
    f32 = jnp.float32
    bp = x_prompt.shape[0]
    xp, xs = x_prompt, x_sample
    cmp_p_l, cmp_s_l, slc_p_l, slc_s_l, win_p_l, win_s_l = [], [], [], [], [], []
    conv_p_l, conv_s_l, dS_p_l, dS_s_l = [], [], [], []
    sh_p_l, sh_s_l, rS_p_l, rS_s_l = [], [], [], []
    for l in range(DEPTH):
        xp = _post_ln(xp, 0.5 * _swiglu(xp, ffn_wi[l, 0], ffn_wo[l, 0]), ln_g[l, 0], ln_b[l, 0])
        xs = _post_ln(xs, 0.5 * _swiglu(xs, ffn_wi[l, 0], ffn_wo[l, 0]), ln_g[l, 0], ln_b[l, 0])
        if l % 2 == 0:
            e = l // 2
            cmp_p = (nsa_cmp_pe[e], nsa_cmp_w1[e], nsa_cmp_b1[e], nsa_cmp_w2[e])
            dn_p = (dn_conv_w[e], dn_a_log[e], dn_dt_bias[e], dn_norm_w[e])
            conv0 = jnp.zeros((bp, CONV_W - 1, DN_QKV), xp.dtype)
            S0 = jnp.zeros((bp, DN_HEADS, DK, DV), f32)
            mp, kvc_p, kvs_p, win_p, conv_p, dS_p = _even_mixer(xp, mix_w_in[e], mix_w_out[e], cmp_p, dn_p,
                                                                conv0, S0, None)
            ms, kvc_s, kvs_s, win_s, conv_s, dS_s = _even_mixer(
                xs, mix_w_in[e], mix_w_out[e], cmp_p, dn_p, state_dn_conv[e], state_dn_S[e],
                (cache_nsa_cmp[e], cache_nsa_slc[e], cache_nsa_win[e], page_table))
            cmp_p_l.append(kvc_p)
            cmp_s_l.append(kvc_s)
            slc_p_l.append(kvs_p)
            slc_s_l.append(kvs_s)
            win_p_l.append(win_p)
            win_s_l.append(win_s)
            conv_p_l.append(conv_p)
            conv_s_l.append(conv_s.astype(state_dn_conv.dtype))
            dS_p_l.append(dS_p)
            dS_s_l.append(dS_s.astype(state_dn_S.dtype))
        else:
            c = l // 2
            rw = (rwkv_mix[c], rwkv_wr[c], rwkv_wk[c], rwkv_wv[c], rwkv_wo[c], rwkv_w0[c], rwkv_w1[c], rwkv_w2[c],
                  rwkv_a0[c], rwkv_a1[c], rwkv_a2[c], rwkv_g1[c], rwkv_g2[c], rwkv_k_k[c], rwkv_k_a[c], rwkv_r_k[c],
                  rwkv_ln_w[c], rwkv_ln_b[c])
            mp, sh_p, rS_p = _rwkv7(xp, jnp.zeros((bp, D_MODEL), xp.dtype),
                                    jnp.zeros((bp, RWKV_HEADS, RWKV_HS, RWKV_HS), f32), *rw)
            ms, sh_s, rS_s = _rwkv7(xs, state_rwkv_shift[c], state_rwkv_S[c], *rw)
            sh_p_l.append(sh_p)
            sh_s_l.append(sh_s.astype(state_rwkv_shift.dtype))
            rS_p_l.append(rS_p)
            rS_s_l.append(rS_s.astype(state_rwkv_S.dtype))
        xp = _post_ln(xp, mp, ln_g[l, 1], ln_b[l, 1])
        xs = _post_ln(xs, ms, ln_g[l, 1], ln_b[l, 1])
        xp = _post_ln(xp, 0.5 * _swiglu(xp, ffn_wi[l, 1], ffn_wo[l, 1]), ln_g[l, 2], ln_b[l, 2])
        xs = _post_ln(xs, 0.5 * _swiglu(xs, ffn_wi[l, 1], ffn_wo[l, 1]), ln_g[l, 2], ln_b[l, 2])
    new_cmp_prompt, new_cmp_sample = jnp.stack(cmp_p_l), jnp.stack(cmp_s_l)
    new_slc_prompt, new_slc_sample = jnp.stack(slc_p_l), jnp.stack(slc_s_l)
    new_win_prompt, new_win_sample = jnp.stack(win_p_l), jnp.stack(win_s_l)
    new_conv_prompt, new_conv_sample = jnp.stack(conv_p_l), jnp.stack(conv_s_l)
    new_dnS_prompt, new_dnS_sample = jnp.stack(dS_p_l), jnp.stack(dS_s_l)
    new_shift_prompt, new_shift_sample = jnp.stack(sh_p_l), jnp.stack(sh_s_l)
    new_rwkvS_prompt, new_rwkvS_sample = jnp.stack(rS_p_l), jnp.stack(rS_s_l)
    return (xp, xs, new_cmp_prompt, new_cmp_sample, new_slc_prompt, new_slc_sample, new_win_prompt, new_win_sample,
            new_conv_prompt, new_conv_sample, new_dnS_prompt, new_dnS_sample, new_shift_prompt, new_shift_sample,
            new_rwkvS_prompt, new_rwkvS_sample)
```

```python
import functools

import numpy as np
import jax
import jax.numpy as jnp
from jax import lax
from jax.experimental import pallas as pl
from jax.experimental.pallas import tpu as pltpu

F32 = jnp.float32
BF16 = jnp.bfloat16
HI = lax.Precision.HIGHEST

DEPTH = 2
ALPHA = (2 * DEPTH) ** 0.25
LN_EPS = 1e-5
RMS_EPS = 1e-6
NSA_HEADS = 8
NSA_KV = 2
NSA_GROUP = NSA_HEADS // NSA_KV
HD = 128
L_CMP = 32
D_CMP = 16
L_SLC = 64
N_SEL = 16
WINDOW = 512
NEG_INF = -1e30
FORCE_SCORE = 1e6
DN_HEADS = 8
DK = 128
DV = 128
CONV_W = 4
RWKV_HS = 64
GN_EPS = 64e-5

A_Q = NSA_HEADS * HD
A_KVW = 2 * NSA_KV * HD
A_GATES = 3 * NSA_HEADS
DN_QKV = DN_HEADS * (2 * DK + DV)
EVEN_SPLIT = (A_Q, A_KVW, A_KVW, A_KVW, A_GATES, DN_QKV, DN_HEADS * DV, DN_HEADS, DN_HEADS)
COL_QA = 0
COL_DQKV = 1024
COL_Z = 4096
COL_KVC = 5120
COL_KVS = 5632
COL_KVW = 6144
COL_SM = 6656
SM_BETA = A_GATES
SM_A = A_GATES + DN_HEADS
E_IN_PAD = 7168
LANE = 128
PAGE_ROWS = 128

VMEM_LIMIT = 56 * 1024 * 1024


def _cp(sem):
    return pltpu.CompilerParams(dimension_semantics=sem, vmem_limit_bytes=VMEM_LIMIT)


def _tile(n, prefs):
    for t in prefs:
        if n % t == 0:
            return t
    return n


def _dot(a, b, precision=None):
    return jnp.dot(a, b, preferred_element_type=F32, precision=precision)


def _dot_nt(a, b, precision=None):
    return lax.dot_general(a, b, (((1,), (1,)), ((), ())), preferred_element_type=F32, precision=precision)


def _dot_tn(a, b, precision=None):
    return lax.dot_general(a, b, (((0,), (0,)), ((), ())), preferred_element_type=F32, precision=precision)


def _layernorm_rows(y, g, b):
    mu = jnp.mean(y, axis=-1, keepdims=True)
    d = y - mu
    var = jnp.mean(d * d, axis=-1, keepdims=True)
    return d * lax.rsqrt(var + LN_EPS) * g + b


def _softplus(x):
    return jnp.maximum(x, 0.0) + jnp.log1p(jnp.exp(-jnp.abs(x)))


def _silu(x):
    return x * jax.nn.sigmoid(x)


def _ffn_kernel(x_ref, wg_ref, wu_ref, wo_ref, g_ref, b_ref, o_ref, xb_ref, acc_ref, *, nf):
    f = pl.program_id(1)

    @pl.when(f == 0)
    def _():
        xb_ref[...] = x_ref[...].astype(BF16)
        acc_ref[...] = jnp.zeros_like(acc_ref)

    xb = xb_ref[...]
    gate = _dot(xb, wg_ref[...])
    up = _dot(xb, wu_ref[...])
    act = (_silu(gate) * up).astype(BF16)
    acc_ref[...] += _dot(act, wo_ref[...])

    @pl.when(f == nf - 1)
    def _():
        y = ALPHA * x_ref[...] + 0.5 * acc_ref[...]
        o_ref[...] = _layernorm_rows(y, g_ref[...], b_ref[...])


def _ffn_ln(x, wi, wo, g, b):
    m, d = x.shape
    f = wo.shape[0]
    tm = _tile(m, (512, 256, 128, 64, 32, 16, 8))
    tf = _tile(f, (512, 256, 128))
    nf = f // tf
    return pl.pallas_call(
        functools.partial(_ffn_kernel, nf=nf),
        grid=(m // tm, nf),
        in_specs=[
            pl.BlockSpec((tm, d), lambda i, j: (i, 0)),
            pl.BlockSpec((d, tf), lambda i, j: (0, j)),
            pl.BlockSpec((d, tf), lambda i, j: (0, j + nf)),
            pl.BlockSpec((tf, d), lambda i, j: (j, 0)),
            pl.BlockSpec((1, d), lambda i, j: (0, 0)),
            pl.BlockSpec((1, d), lambda i, j: (0, 0)),
        ],
        out_specs=pl.BlockSpec((tm, d), lambda i, j: (i, 0)),
        out_shape=jax.ShapeDtypeStruct((m, d), F32),
        scratch_shapes=[pltpu.VMEM((tm, d), BF16), pltpu.VMEM((tm, d), F32)],
        compiler_params=_cp(("parallel", "arbitrary")),
        name="ffn_ln",
    )(x, wi, wi, wo, g, b)


def _mm_kernel(*refs, mix, act, pair_out):
    if mix:
        x_ref, xp_ref, m_ref, w_ref, o_ref = refs
        x = x_ref[...]
        x = x + (xp_ref[...] - x) * m_ref[...]
    else:
        x_ref, w_ref, o_ref = refs
        x = x_ref[...]
    y = _dot(x.astype(BF16), w_ref[...])
    if act == "tanh":
        y = jnp.tanh(y)
    elif act == "sigmoid":
        y = jax.nn.sigmoid(y)
    if pair_out:
        for p in range(o_ref.shape[0]):
            o_ref[p] = y[:, p * LANE:(p + 1) * LANE]
    else:
        o_ref[...] = y


def _mm(x, w, *, xprev=None, mixrow=None, act=None, pair_out=False, tn_prefs=(512, 256, 128)):
    m, k = x.shape
    n = w.shape[1]
    tm = _tile(m, (512, 256, 128, 64, 32, 16, 8))
    tn = _tile(n, tn_prefs)
    mix = xprev is not None
    in_specs = [pl.BlockSpec((tm, k), lambda i, j: (i, 0))]
    args = [x]
    if mix:
        in_specs += [pl.BlockSpec((tm, k), lambda i, j: (i, 0)), pl.BlockSpec((1, k), lambda i, j: (0, 0))]
        args += [xprev, mixrow]
    in_specs.append(pl.BlockSpec((k, tn), lambda i, j: (0, j)))
    args.append(w)
    if pair_out:
        npb = tn // LANE
        out_spec = pl.BlockSpec((npb, tm, LANE), lambda i, j: (j, i, 0))
        out_shape = jax.ShapeDtypeStruct((n // LANE, m, LANE), F32)
    else:
        out_spec = pl.BlockSpec((tm, tn), lambda i, j: (i, j))
        out_shape = jax.ShapeDtypeStruct((m, n), F32)
    return pl.pallas_call(
        functools.partial(_mm_kernel, mix=mix, act=act, pair_out=pair_out),
        grid=(m // tm, n // tn),
        in_specs=in_specs,
        out_specs=out_spec,
        out_shape=out_shape,
        compiler_params=_cp(("parallel", "parallel")),
        name="matmul",
    )(*args)


def _mmln_kernel(a_ref, w_ref, x_ref, g_ref, b_ref, o_ref, *, pair_in):
    if pair_in:
        a = jnp.concatenate([a_ref[p].astype(BF16) for p in range(a_ref.shape[0])], axis=1)
    else:
        a = a_ref[...].astype(BF16)
    y = _dot(a, w_ref[...])
    o_ref[...] = _layernorm_rows(ALPHA * x_ref[...] + y, g_ref[...], b_ref[...])


def _mm_ln(a, w, x, g, b, *, pair_in=False):
    m, d = x.shape
    k = w.shape[0]
    tm = _tile(m, (512, 256, 128, 64, 32, 16, 8))
    if pair_in:
        a_spec = pl.BlockSpec((k // LANE, tm, LANE), lambda i: (0, i, 0))
    else:
        a_spec = pl.BlockSpec((tm, k), lambda i: (i, 0))
    return pl.pallas_call(
        functools.partial(_mmln_kernel, pair_in=pair_in),
        grid=(m // tm,),
        in_specs=[
            a_spec,
            pl.BlockSpec((k, d), lambda i: (0, 0)),
            pl.BlockSpec((tm, d), lambda i: (i, 0)),
            pl.BlockSpec((1, d), lambda i: (0, 0)),
            pl.BlockSpec((1, d), lambda i: (0, 0)),
        ],
        out_specs=pl.BlockSpec((tm, d), lambda i: (i, 0)),
        out_shape=jax.ShapeDtypeStruct((m, d), F32),
        compiler_params=_cp(("parallel",)),
        name="matmul_ln",
    )(a, w, x, g, b)


CH_PER_PAGE = PAGE_ROWS // D_CMP
CH_FEAT = D_CMP * A_KVW


def _cmp1_kernel(pt_ref, *refs, G):
    del pt_ref
    page_refs = refs[:G]
    w_ref, o_ref, xs_ref = refs[G:]
    for j in range(G):
        xs_ref[j * CH_PER_PAGE:(j + 1) * CH_PER_PAGE, :] = page_refs[j][0]
    rows = G * CH_PER_PAGE
    for sg in range(2 * NSA_KV):
        s = sg // NSA_KV
        acc = jnp.zeros((rows, 2 * HD), F32)
        for p in range(D_CMP):
            c0 = p * A_KVW + sg * HD
            acc = acc + _dot(xs_ref[:, c0:c0 + HD].astype(BF16), w_ref[s, p])
        o_ref[0, :, sg * 2 * HD:(sg + 1) * 2 * HD] = acc


def _cmp_stage1(pool, page_table, w4):
    b, n_pages = page_table.shape
    G = _tile(n_pages, (8, 4, 2, 1))

    def page_map(j):
        return lambda bi, p, pt: (pt[bi, p * G + j], 0, 0)

    in_specs = [pl.BlockSpec((1, CH_PER_PAGE, CH_FEAT), page_map(j)) for j in range(G)]
    in_specs.append(pl.BlockSpec(w4.shape, lambda bi, p, pt: (0, 0, 0, 0)))
    grid_spec = pltpu.PrefetchScalarGridSpec(
        num_scalar_prefetch=1,
        grid=(b, n_pages // G),
        in_specs=in_specs,
        out_specs=pl.BlockSpec((1, G * CH_PER_PAGE, 8 * HD), lambda bi, p, pt: (bi, p, 0)),
        scratch_shapes=[pltpu.VMEM((G * CH_PER_PAGE, CH_FEAT), F32)],
    )
    return pl.pallas_call(
        functools.partial(_cmp1_kernel, G=G),
        grid_spec=grid_spec,
        out_shape=jax.ShapeDtypeStruct((b, n_pages * CH_PER_PAGE, 8 * HD), F32),
        compiler_params=_cp(("parallel", "arbitrary")),
        name="nsa_cmp_stage1",
    )(page_table, *([pool] * G), w4)


def _gelu_tanh(x):
    return 0.5 * x * (1.0 + jnp.tanh(np.sqrt(2.0 / np.pi).astype(np.float32) * (x + 0.044715 * (x * x * x))))


def _cmp2_kernel(h_ref, pe_ref, w4_ref, b1_ref, w2_ref, o_ref):
    nch = h_ref.shape[1]
    for s in range(2):
        pacc = jnp.zeros((8, 2 * HD), F32)
        for p in range(D_CMP):
            pacc = pacc + _dot(pe_ref[s, p].astype(BF16), w4_ref[s, p])
        const = pacc[0:1, 0:HD] + pacc[1:2, HD:2 * HD] + b1_ref[s:s + 1, :]
        for gi in range(NSA_KV):
            sg = s * NSA_KV + gi
            h0 = h_ref[0, :, sg * 2 * HD:sg * 2 * HD + HD]
            h1 = h_ref[0, :, sg * 2 * HD + HD:(sg + 1) * 2 * HD]
            hid = h0 + pltpu.roll(h1, nch - 1, 0) + const
            o_ref[0, :, sg * HD:(sg + 1) * HD] = _dot(_gelu_tanh(hid).astype(BF16), w2_ref[s])


def _cmp_stage2(h, pe8, w4, b1, w2):
    b, nch, _ = h.shape
    return pl.pallas_call(
        _cmp2_kernel,
        grid=(b,),
        in_specs=[
            pl.BlockSpec((1, nch, 8 * HD), lambda i: (i, 0, 0)),
            pl.BlockSpec(pe8.shape, lambda i: (0, 0, 0, 0)),
            pl.BlockSpec(w4.shape, lambda i: (0, 0, 0, 0)),
            pl.BlockSpec(b1.shape, lambda i: (0, 0)),
            pl.BlockSpec(w2.shape, lambda i: (0, 0, 0)),
        ],
        out_specs=pl.BlockSpec((1, nch, A_KVW), lambda i: (i, 0, 0)),
        out_shape=jax.ShapeDtypeStruct((b, nch, A_KVW), F32),
        compiler_params=_cp(("parallel",)),
        name="nsa_cmp_stage2",
    )(h, pe8, w4, b1, w2)


def _nsa_sel_kernel(slopes_ref, q_ref, ck_ref, cv_ref, ov_ref, ocmp_ref, sel_ref, *, TQ, NC, NS, q_off, n_pick):
    i = pl.program_id(1)
    g = pl.program_id(2)
    nch = ck_ref.shape[1]
    nsp = ov_ref.shape[1]
    qpos = q_off + i * TQ + lax.broadcasted_iota(jnp.int32, (TQ, 1), 0)
    cidx = lax.broadcasted_iota(jnp.int32, (1, nch), 1)
    cstart = cidx * D_CMP
    cmask = jnp.logical_and(cstart + (L_CMP - 1) <= qpos, cidx < NC)
    cdist = (qpos - cstart).astype(F32) - 0.5 * (L_CMP - 1)
    kb = ck_ref[0].astype(BF16)
    vb = cv_ref[0].astype(BF16)
    q = q_ref[0]
    pcs = jnp.zeros((TQ, nch), F32)
    for j in range(NSA_GROUP):
        qj = (q[:, j * HD:(j + 1) * HD] * HD ** -0.5).astype(BF16)
        slope = slopes_ref[NSA_GROUP * g + j]
        lg = jnp.where(cmask, _dot_nt(qj, kb) - slope * cdist, NEG_INF)
        e = jnp.exp(lg - jnp.max(lg, axis=-1, keepdims=True))
        p = jnp.where(cmask, e / jnp.sum(e, axis=-1, keepdims=True), 0.0)
        pcs = pcs + p
        ocmp_ref[0, :, j * HD:(j + 1) * HD] = _dot(p.astype(BF16), vb)
    imp = _dot(pcs, ov_ref[...], precision=HI)
    sid = lax.broadcasted_iota(jnp.int32, (1, nsp), 1)
    cur = lax.shift_right_arithmetic(qpos, int(np.log2(L_SLC)))
    svalid = jnp.logical_and(sid * L_SLC <= qpos, sid < NS)
    forced = jnp.logical_or(sid == 0, jnp.logical_or(sid == cur, sid == cur - 1))
    score = jnp.where(svalid, jnp.where(forced, FORCE_SCORE, imp), NEG_INF)
    sidf = jnp.broadcast_to(sid, (TQ, nsp))
    sel = jnp.zeros((TQ, nsp), F32)
    for _ in range(n_pick):
        mx = jnp.max(score, axis=-1, keepdims=True)
        first = jnp.min(jnp.where(score == mx, sidf, nsp), axis=-1, keepdims=True)
        pick = sidf == first
        sel = jnp.where(jnp.logical_and(pick, mx > 0.5 * NEG_INF), 1.0, sel)
        score = jnp.where(pick, -jnp.inf, score)
    sel_ref[0, 0] = sel


def _nsa_select(q, cmp, overlap, slopes, *, TQ, NC, NS, q_off):
    b, t, _ = q.shape
    nch = cmp.shape[1]
    nsp = overlap.shape[1]
    kern = functools.partial(_nsa_sel_kernel, TQ=TQ, NC=NC, NS=NS, q_off=q_off, n_pick=min(N_SEL, NS))
    return pl.pallas_call(
        kern,
        grid=(b, t // TQ, NSA_KV),
        in_specs=[
            pl.BlockSpec(memory_space=pltpu.SMEM),
            pl.BlockSpec((1, TQ, NSA_GROUP * HD), lambda bi, i, g: (bi, i, g)),
            pl.BlockSpec((1, nch, HD), lambda bi, i, g: (bi, 0, g)),
            pl.BlockSpec((1, nch, HD), lambda bi, i, g: (bi, 0, NSA_KV + g)),
            pl.BlockSpec((nch, nsp), lambda bi, i, g: (0, 0)),
        ],
        out_specs=[
            pl.BlockSpec((1, TQ, NSA_GROUP * HD), lambda bi, i, g: (bi, i, g)),
            pl.BlockSpec((1, 1, TQ, nsp), lambda bi, i, g: (bi, g, i, 0)),
        ],
        out_shape=[jax.ShapeDtypeStruct((b, t, A_Q), F32), jax.ShapeDtypeStruct((b, NSA_KV, t, nsp), F32)],
        compiler_params=_cp(("parallel", "parallel", "parallel")),
        name="nsa_cmp_select",
    )(slopes, q, cmp, cmp, overlap)


def _stack_heads(q, tq):
    del tq
    return jnp.concatenate([q[:, j * HD:(j + 1) * HD] for j in range(NSA_GROUP)], axis=0)


def _group_slopes(slopes_ref, g, tq):
    return jnp.concatenate(
        [jnp.full((tq, 1), slopes_ref[NSA_GROUP * g + j], F32) for j in range(NSA_GROUP)], axis=0)


def _online_step(carry, lg, mask, v):
    m, l, acc = carry
    m_new = jnp.maximum(m, jnp.max(lg, axis=-1, keepdims=True))
    p = jnp.where(mask, jnp.exp(lg - m_new), 0.0)
    a = jnp.exp(m - m_new)
    l = a * l + jnp.sum(p, axis=-1, keepdims=True)
    acc = a * acc + _dot(p.astype(BF16), v)
    return m_new, l, acc


def _nsa_slc_kernel(slopes_ref, q_ref, sel_ref, k_ref, v_ref, o_ref, *, TQ, TK):
    g = pl.program_id(0)
    i = pl.program_id(1)
    nsp = sel_ref.shape[-1]
    rows = NSA_GROUP * TQ
    q4 = (_stack_heads(q_ref[...], TQ) * HD ** -0.5).astype(BF16)
    qpos = i * TQ + lax.broadcasted_iota(jnp.int32, (TQ, 1), 0)
    qpos4 = jnp.concatenate([qpos] * NSA_GROUP, axis=0)
    slope4 = _group_slopes(slopes_ref, g, TQ)
    selb = sel_ref[0].astype(BF16)
    blk_per_tile = TK // L_SLC
    srow = lax.broadcasted_iota(jnp.int32, (nsp, TK), 0)
    scol = lax.broadcasted_iota(jnp.int32, (nsp, TK), 1) // L_SLC
    kcol = lax.broadcasted_iota(jnp.int32, (1, TK), 1)

    def body(kt, carry):
        k0 = pl.multiple_of(kt * TK, TK)
        k = k_ref[pl.ds(k0, TK), :]
        v = v_ref[pl.ds(k0, TK), :]
        s = _dot_nt(q4, k)
        dist = qpos4 - (k0 + kcol)
        expand = jnp.where(srow == kt * blk_per_tile + scol, 1.0, 0.0).astype(BF16)
        se = _dot(selb, expand)
        se4 = jnp.concatenate([se] * NSA_GROUP, axis=0)
        mask = jnp.logical_and(se4 > 0.5, dist >= 0)
        lg = jnp.where(mask, s - slope4 * dist.astype(F32), NEG_INF)
        return _online_step(carry, lg, mask, v)

    ntile = (i * TQ + TQ + TK - 1) // TK
    init = (jnp.full((rows, 1), NEG_INF, F32), jnp.zeros((rows, 1), F32), jnp.zeros((rows, HD), F32))
    _, l, acc = lax.fori_loop(0, ntile, body, init)
    o = acc / l
    for j in range(NSA_GROUP):
        o_ref[:, j * HD:(j + 1) * HD] = o[j * TQ:(j + 1) * TQ]


def _nsa_slc_prompt(q, sel, kvb, slopes, *, TQ, TK):
    t = q.shape[0]
    nsp = sel.shape[-1]
    kb0 = A_KVW // HD
    return pl.pallas_call(
        functools.partial(_nsa_slc_kernel, TQ=TQ, TK=TK),
        grid=(NSA_KV, t // TQ),
        in_specs=[
            pl.BlockSpec(memory_space=pltpu.SMEM),
            pl.BlockSpec((TQ, NSA_GROUP * HD), lambda g, i: (i, g)),
            pl.BlockSpec((1, TQ, nsp), lambda g, i: (g, i, 0)),
            pl.BlockSpec((t, HD), lambda g, i: (0, kb0 + g)),
            pl.BlockSpec((t, HD), lambda g, i: (0, kb0 + NSA_KV + g)),
        ],
        out_specs=pl.BlockSpec((TQ, NSA_GROUP * HD), lambda g, i: (i, g)),
        out_shape=jax.ShapeDtypeStruct((t, A_Q), F32),
        compiler_params=_cp(("parallel", "parallel")),
        name="nsa_slc_prompt",
    )(slopes, q, sel, kvb, kvb)


def _nsa_win_kernel(slopes_ref, q_ref, k_ref, v_ref, ga_ref, ocmp_ref, oslc_ref, o_ref, *, TQ):
    g = pl.program_id(0)
    i = pl.program_id(1)
    rows = NSA_GROUP * TQ
    q4 = (_stack_heads(q_ref[...], TQ) * HD ** -0.5).astype(BF16)
    qpos = i * TQ + lax.broadcasted_iota(jnp.int32, (TQ, 1), 0)
    qpos4 = jnp.concatenate([qpos] * NSA_GROUP, axis=0)
    slope4 = _group_slopes(slopes_ref, g, TQ)
    kcol = lax.broadcasted_iota(jnp.int32, (1, TQ), 1)

    def body(kt, carry):
        k0 = pl.multiple_of(kt * TQ, TQ)
        k = k_ref[pl.ds(k0, TQ), :]
        v = v_ref[pl.ds(k0, TQ), :]
        s = _dot_nt(q4, k)
        dist = qpos4 - (k0 + kcol)
        mask = jnp.logical_and(dist >= 0, dist <= WINDOW)
        lg = jnp.where(mask, s - slope4 * dist.astype(F32), NEG_INF)
        return _online_step(carry, lg, mask, v)

    lo = jnp.maximum(i - WINDOW // TQ, 0)
    init = (jnp.full((rows, 1), NEG_INF, F32), jnp.zeros((rows, 1), F32), jnp.zeros((rows, HD), F32))
    _, l, acc = lax.fori_loop(lo, i + 1, body, init)
    o_win = acc / l
    gates = jax.nn.sigmoid(ga_ref[...])
    for j in range(NSA_GROUP):
        ca = 3 * j
        cb = 3 * (NSA_GROUP + j)

        def gate(c):
            return jnp.where(g == 0, gates[:, ca + c:ca + c + 1], gates[:, cb + c:cb + c + 1])

        sl = slice(j * HD, (j + 1) * HD)
        o_ref[:, sl] = (gate(0) * ocmp_ref[:, sl] + gate(1) * oslc_ref[:, sl]
                        + gate(2) * o_win[j * TQ:(j + 1) * TQ])


def _nsa_win_prompt(q, kvb, h_in, o_cmp, o_slc, slopes, *, TQ):
    t = q.shape[0]
    kb0 = 2 * A_KVW // HD
    hspec = pl.BlockSpec((TQ, NSA_GROUP * HD), lambda g, i: (i, g))
    return pl.pallas_call(
        functools.partial(_nsa_win_kernel, TQ=TQ),
        grid=(NSA_KV, t // TQ),
        in_specs=[
            pl.BlockSpec(memory_space=pltpu.SMEM),
            hspec,
            pl.BlockSpec((t, HD), lambda g, i: (0, kb0 + g)),
            pl.BlockSpec((t, HD), lambda g, i: (0, kb0 + NSA_KV + g)),
            pl.BlockSpec((TQ, LANE), lambda g, i: (i, COL_SM // LANE)),
            hspec,
            hspec,
        ],
        out_specs=hspec,
        out_shape=jax.ShapeDtypeStruct((t, A_Q), F32),
        compiler_params=_cp(("parallel", "parallel")),
        name="nsa_win_prompt",
    )(slopes, q, kvb, kvb, h_in, o_cmp, o_slc)


def _rows8(row, width):
    return jnp.concatenate([row[:, h * width:(h + 1) * width] for h in range(NSA_HEADS)], axis=0)


def _kv_rows8(kn, off):
    return jnp.concatenate(
        [kn[:, off + (h // NSA_GROUP) * HD: off + (h // NSA_GROUP + 1) * HD] for h in range(NSA_HEADS)], axis=0)


def _slope8(slopes_ref):
    hrow = lax.broadcasted_iota(jnp.int32, (NSA_HEADS, 1), 0)
    out = jnp.zeros((NSA_HEADS, 1), F32)
    for h in range(NSA_HEADS):
        out = jnp.where(hrow == h, slopes_ref[h], out)
    return out


def _bf(x):
    return x.astype(BF16).astype(F32)


def _nsa_slc_dec_kernel(pt_ref, slopes_ref, *refs, G, past):
    del pt_ref
    pages = refs[:G]
    q_ref, sel_ref, knew_ref, o_ref, kv_s, m_s, l_s, acc_s = refs[G:]
    pg = pl.program_id(1)
    nsp = sel_ref.shape[-1]
    tk = G * PAGE_ROWS
    q8 = _rows8(q_ref[0], HD) * HD ** -0.5
    q8b = q8.astype(BF16)
    hrow = lax.broadcasted_iota(jnp.int32, (NSA_HEADS, 1), 0)
    first_group = hrow < NSA_GROUP

    @pl.when(pg == 0)
    def _():
        kn = knew_ref[0]
        m_s[...] = jnp.sum(_bf(q8) * _bf(_kv_rows8(kn, 0)), axis=-1, keepdims=True)
        l_s[...] = jnp.ones_like(l_s)
        acc_s[...] = _bf(_kv_rows8(kn, NSA_KV * HD))

    for j in range(G):
        kv_s[j * PAGE_ROWS:(j + 1) * PAGE_ROWS, :] = pages[j][0]
    kpos = pg * tk + lax.broadcasted_iota(jnp.int32, (1, tk), 1)
    dist = (past - kpos).astype(F32)
    srow = lax.broadcasted_iota(jnp.int32, (nsp, tk), 0)
    scol = (pg * tk + lax.broadcasted_iota(jnp.int32, (nsp, tk), 1)) // L_SLC
    expand = jnp.where(srow == scol, 1.0, 0.0).astype(BF16)
    sel2 = sel_ref[0]
    sel8 = jnp.where(first_group, sel2[0:1, :], sel2[1:2, :]).astype(BF16)
    mask = _dot(sel8, expand) > 0.5
    s8 = jnp.where(first_group, _dot_nt(q8b, kv_s[:, 0:HD].astype(BF16)),
                   _dot_nt(q8b, kv_s[:, HD:2 * HD].astype(BF16)))
    lg = jnp.where(mask, s8 - _slope8(slopes_ref) * dist, NEG_INF)
    m = m_s[...]
    m_new = jnp.maximum(m, jnp.max(lg, axis=-1, keepdims=True))
    p = jnp.where(mask, jnp.exp(lg - m_new), 0.0)
    a = jnp.exp(m - m_new)
    pb = p.astype(BF16)
    pv = jnp.where(first_group, _dot(pb, kv_s[:, 2 * HD:3 * HD].astype(BF16)),
                   _dot(pb, kv_s[:, 3 * HD:4 * HD].astype(BF16)))
    m_s[...] = m_new
    l_s[...] = a * l_s[...] + jnp.sum(p, axis=-1, keepdims=True)
    acc_s[...] = a * acc_s[...] + pv

    @pl.when(pg == pl.num_programs(1) - 1)
    def _():
        o = acc_s[...] / l_s[...]
        for h in range(NSA_HEADS):
            o_ref[0, :, h * HD:(h + 1) * HD] = o[h:h + 1, :]


def _nsa_slc_sample(q, sel, knew, pool, page_table, slopes):
    b, n_pages = page_table.shape
    G = _tile(n_pages, (8, 4, 2, 1))
    nsp = sel.shape[-1]
    past = n_pages * PAGE_ROWS

    def page_map(j):
        return lambda bi, p, pt: (pt[bi, p * G + j], 0, 0)

    in_specs = [pl.BlockSpec(memory_space=pltpu.SMEM)]
    in_specs += [pl.BlockSpec((1, PAGE_ROWS, A_KVW), page_map(j)) for j in range(G)]
    in_specs += [
        pl.BlockSpec((1, 1, A_Q), lambda bi, p, pt: (bi, 0, 0)),
        pl.BlockSpec((1, NSA_KV, nsp), lambda bi, p, pt: (bi, 0, 0)),
        pl.BlockSpec((1, 1, A_KVW), lambda bi, p, pt: (bi, 0, 0)),
    ]
    grid_spec = pltpu.PrefetchScalarGridSpec(
        num_scalar_prefetch=1,
        grid=(b, n_pages // G),
        in_specs=in_specs,
        out_specs=pl.BlockSpec((1, 1, A_Q), lambda bi, p, pt: (bi, 0, 0)),
        scratch_shapes=[pltpu.VMEM((G * PAGE_ROWS, A_KVW), F32), pltpu.VMEM((NSA_HEADS, 1), F32),
                        pltpu.VMEM((NSA_HEADS, 1), F32), pltpu.VMEM((NSA_HEADS, HD), F32)],
    )
    return pl.pallas_call(
        functools.partial(_nsa_slc_dec_kernel, G=G, past=past),
        grid_spec=grid_spec,
        out_shape=jax.ShapeDtypeStruct((b, 1, A_Q), F32),
        compiler_params=_cp(("parallel", "arbitrary")),
        name="nsa_slc_sample",
    )(page_table, slopes, *([pool] * G), q, sel, knew)


def _nsa_win_dec_kernel(slopes_ref, q_ref, wb_ref, knew_ref, ga_ref, ocmp_ref, oslc_ref, o_ref, *, past):
    nb = wb_ref.shape[1]
    q8 = _rows8(q_ref[0], HD) * HD ** -0.5
    q8b = q8.astype(BF16)
    hrow = lax.broadcasted_iota(jnp.int32, (NSA_HEADS, 1), 0)
    first_group = hrow < NSA_GROUP
    kn = knew_ref[0]
    wb = wb_ref[0]
    kwpos = past - nb + lax.broadcasted_iota(jnp.int32, (1, nb), 1)
    wd = past - kwpos
    mask = jnp.logical_and(jnp.logical_and(wd >= 0, wd <= WINDOW), kwpos >= 0)
    s8 = jnp.where(first_group, _dot_nt(q8b, wb[:, 0:HD].astype(BF16)),
                   _dot_nt(q8b, wb[:, HD:2 * HD].astype(BF16)))
    lg = jnp.where(mask, s8 - _slope8(slopes_ref) * wd.astype(F32), NEG_INF)
    s_self = jnp.sum(_bf(q8) * _bf(_kv_rows8(kn, 0)), axis=-1, keepdims=True)
    m = jnp.maximum(jnp.max(lg, axis=-1, keepdims=True), s_self)
    p = jnp.where(mask, jnp.exp(lg - m), 0.0)
    p_self = jnp.exp(s_self - m)
    pb = p.astype(BF16)
    pv = jnp.where(first_group, _dot(pb, wb[:, 2 * HD:3 * HD].astype(BF16)),
                   _dot(pb, wb[:, 3 * HD:4 * HD].astype(BF16)))
    pv = pv + _bf(p_self) * _bf(_kv_rows8(kn, NSA_KV * HD))
    o_win = pv / (jnp.sum(p, axis=-1, keepdims=True) + p_self)
    gates = jax.nn.sigmoid(ga_ref[0])
    ocmp = ocmp_ref[0]
    oslc = oslc_ref[0]
    for h in range(NSA_HEADS):
        sl = slice(h * HD, (h + 1) * HD)
        o_ref[0, :, sl] = (gates[:, 3 * h:3 * h + 1] * ocmp[:, sl] + gates[:, 3 * h + 1:3 * h + 2] * oslc[:, sl]
                           + gates[:, 3 * h + 2:3 * h + 3] * o_win[h:h + 1, :])


def _nsa_win_sample(q, win_buf, knew, ga, o_cmp, o_slc, slopes, *, past):
    b = q.shape[0]
    nb = win_buf.shape[1]
    row = lambda w: pl.BlockSpec((1, 1, w), lambda bi: (bi, 0, 0))
    return pl.pallas_call(
        functools.partial(_nsa_win_dec_kernel, past=past),
        grid=(b,),
        in_specs=[pl.BlockSpec(memory_space=pltpu.SMEM), row(A_Q),
                  pl.BlockSpec((1, nb, A_KVW), lambda bi: (bi, 0, 0)), row(A_KVW), row(LANE), row(A_Q), row(A_Q)],
        out_specs=row(A_Q),
        out_shape=jax.ShapeDtypeStruct((b, 1, A_Q), F32),
        compiler_params=_cp(("parallel",)),
        name="nsa_win_sample",
    )(slopes, q, win_buf, knew, ga, o_cmp, o_slc)


def _dn_conv_kernel(x_ref, prev_ref, buf_ref, w_ref, o_ref, hist_ref):
    c = pl.program_id(0)
    t = pl.program_id(1)
    tt = x_ref.shape[0]
    hist_ref[0:8, :] = jnp.where(t == 0, buf_ref[...], prev_ref[...])
    hist_ref[8:8 + tt, :] = x_ref[...]
    w = w_ref[...]
    y = jnp.zeros((tt, LANE), F32)
    for i in range(CONV_W):
        y = y + w[i:i + 1, :] * hist_ref[8 - (CONV_W - 1) + i: 8 - (CONV_W - 1) + i + tt, :]
    y = _silu(y)
    nrm = y * lax.rsqrt(jnp.sum(y * y, axis=-1, keepdims=True) + 1e-6)
    o_ref[...] = jnp.where(c < DN_HEADS, nrm * DK ** -0.5, jnp.where(c < 2 * DN_HEADS, nrm, y))


def _dn_conv_prompt(h_in, buf8, conv_wt):
    t = h_in.shape[0]
    tt = _tile(t, (1024, 512, 256, 128, 64, 32, 16, 8))
    c0 = COL_DQKV // LANE
    return pl.pallas_call(
        _dn_conv_kernel,
        grid=(DN_QKV // LANE, t // tt),
        in_specs=[
            pl.BlockSpec((tt, LANE), lambda c, i: (i, c0 + c)),
            pl.BlockSpec((8, LANE), lambda c, i: (jnp.maximum(i * (tt // 8) - 1, 0), c0 + c)),
            pl.BlockSpec((8, LANE), lambda c, i: (0, c)),
            pl.BlockSpec((CONV_W, LANE), lambda c, i: (0, c)),
        ],
        out_specs=pl.BlockSpec((tt, LANE), lambda c, i: (i, c)),
        out_shape=jax.ShapeDtypeStruct((t, DN_QKV), F32),
        scratch_shapes=[pltpu.VMEM((tt + 8, LANE), F32)],
        compiler_params=_cp(("parallel", "parallel")),
        name="dn_conv",
    )(h_in, h_in, buf8, conv_wt)


def _unit_lower_inverse(a, n):
    ri = lax.broadcasted_iota(jnp.int32, (n, n), 0)
    ci = lax.broadcasted_iota(jnp.int32, (n, n), 1)
    eye = jnp.where(ri == ci, 1.0, 0.0).astype(F32)
    base = min(16, n)

    def same_block(b):
        return (ri // b) == (ci // b)

    p = jnp.where(same_block(base), -a, 0.0)
    r = eye + p
    steps = int(np.log2(base)) - 1
    for _ in range(steps):
        p = _dot(p, p, HI)
        r = r + _dot(r, p, HI)
    b = base
    while b < n:
        off = jnp.where(jnp.logical_and(same_block(2 * b), jnp.logical_not(same_block(b))), a, 0.0)
        r = r - _dot(r, _dot(off, r, HI), HI)
        b *= 2
    return r


def _dn_chunk_kernel(q_ref, k_ref, v_ref, sm_ref, smt_ref, z_ref, prow_ref, pcol_ref, nw_ref,
                     o_ref, s_out_ref, s_ref):
    c = pl.program_id(0)
    C = q_ref.shape[0]

    @pl.when(c == 0)
    def _():
        s_ref[...] = jnp.zeros_like(s_ref)

    ri = lax.broadcasted_iota(jnp.int32, (C, C), 0)
    ci = lax.broadcasted_iota(jnp.int32, (C, C), 1)
    lower = ri >= ci
    strict = ri > ci
    tril = jnp.where(lower, 1.0, 0.0).astype(F32)
    triu = jnp.where(ri <= ci, 1.0, 0.0).astype(F32)
    sm = sm_ref[...]
    smt = smt_ref[...]
    g_cols = -jnp.exp(prow_ref[0:1, :]) * _softplus(sm + prow_ref[1:2, :])
    g_rows = -jnp.exp(pcol_ref[:, 0:1]) * _softplus(smt + pcol_ref[:, 1:2])
    gcum_cols = _dot(tril, g_cols, HI)
    gcum_rows = _dot(g_rows, triu, HI)
    beta_cols = jax.nn.sigmoid(sm)
    nw = nw_ref[...]
    for h in range(DN_HEADS):
        sl = slice(h * DK, (h + 1) * DK)
        q = q_ref[:, sl]
        k = k_ref[:, sl]
        v = v_ref[:, sl]
        gc = gcum_cols[:, SM_A + h:SM_A + h + 1]
        gr = gcum_rows[SM_A + h:SM_A + h + 1, :]
        beta = beta_cols[:, SM_BETA + h:SM_BETA + h + 1]
        decay = jnp.where(lower, jnp.exp(jnp.where(lower, gc - gr, 0.0)), 0.0)
        kb = k * beta
        a_mat = jnp.where(strict, _dot_nt(kb, k, HI) * decay, 0.0)
        tinv = _unit_lower_inverse(a_mat, C)
        val = _dot(tinv, v * beta, HI)
        kcd = _dot(tinv, kb * jnp.exp(gc), HI)
        s = s_ref[h]
        v_new = val - _dot(kcd, s, HI)
        attn = _dot_nt(q, k, HI) * decay
        o = _dot(q * jnp.exp(gc), s, HI) + _dot(attn, v_new, HI)
        gl = gc[C - 1:C, :]
        s_new = s * jnp.exp(gl) + _dot_tn(k * jnp.exp(gl - gc), v_new, HI)
        s_ref[h] = s_new
        o = o * lax.rsqrt(jnp.mean(o * o, axis=-1, keepdims=True) + RMS_EPS) * nw
        o_ref[:, sl] = o * _silu(z_ref[:, sl])

    @pl.when(c == pl.num_programs(0) - 1)
    def _():
        s_out_ref[...] = s_ref[...]


def _dn_chunk_prompt(qkvn, h_in, smt, prow, pcol, norm_w, *, C):
    t = qkvn.shape[0]
    hw = DN_HEADS * DK
    return pl.pallas_call(
        _dn_chunk_kernel,
        grid=(t // C,),
        in_specs=[
            pl.BlockSpec((C, hw), lambda c: (c, 0)),
            pl.BlockSpec((C, hw), lambda c: (c, 1)),
            pl.BlockSpec((C, hw), lambda c: (c, 2)),
            pl.BlockSpec((C, LANE), lambda c: (c, COL_SM // LANE)),
            pl.BlockSpec((LANE, C), lambda c: (0, c)),
            pl.BlockSpec((C, hw), lambda c: (c, COL_Z // hw)),
            pl.BlockSpec((2, LANE), lambda c: (0, 0)),
            pl.BlockSpec((LANE, 2), lambda c: (0, 0)),
            pl.BlockSpec((1, DV), lambda c: (0, 0)),
        ],
        out_specs=[pl.BlockSpec((C, hw), lambda c: (c, 0)),
                   pl.BlockSpec((DN_HEADS, DK, DV), lambda c: (0, 0, 0))],
        out_shape=[jax.ShapeDtypeStruct((t, hw), F32), jax.ShapeDtypeStruct((DN_HEADS, DK, DV), F32)],
        scratch_shapes=[pltpu.VMEM((DN_HEADS, DK, DV), F32)],
        compiler_params=_cp(("arbitrary",)),
        name="dn_chunk",
    )(qkvn, qkvn, qkvn, h_in, smt, h_in, prow, pcol, norm_w)


def _row_to_col(row, n):
    ri = lax.broadcasted_iota(jnp.int32, (n, n), 0)
    ci = lax.broadcasted_iota(jnp.int32, (n, n), 1)
    return jnp.sum(jnp.where(ri == ci, jnp.broadcast_to(row, (n, n)), 0.0), axis=1, keepdims=True)


def _col_to_row(col, n):
    ri = lax.broadcasted_iota(jnp.int32, (n, n), 0)
    ci = lax.broadcasted_iota(jnp.int32, (n, n), 1)
    return jnp.sum(jnp.where(ri == ci, jnp.broadcast_to(col, (n, n)), 0.0), axis=0, keepdims=True)


def _dn_dec_kernel(buf_ref, xq_ref, xk_ref, xv_ref, w_ref, sm_ref, z_ref, prow_ref, nw_ref, s0_ref,
                   o_ref, s_out_ref):
    hw = DN_HEADS * DK
    buf = buf_ref[0]
    w = w_ref[...]
    sm = sm_ref[0]
    g_row = -jnp.exp(prow_ref[0:1, :]) * _softplus(sm + prow_ref[1:2, :])
    beta_row = jax.nn.sigmoid(sm)
    nw = nw_ref[...]
    z = z_ref[0]
    parts = []
    for part, x_ref in enumerate((xq_ref, xk_ref, xv_ref)):
        sl = slice(part * hw, (part + 1) * hw)
        y = w[CONV_W - 1:CONV_W, sl] * x_ref[0]
        for i in range(CONV_W - 1):
            y = y + w[i:i + 1, sl] * buf[i:i + 1, sl]
        parts.append(_silu(y))
    for h in range(DN_HEADS):
        sl = slice(h * DK, (h + 1) * DK)
        q = parts[0][:, sl]
        k = parts[1][:, sl]
        v = parts[2][:, sl]
        q = q * lax.rsqrt(jnp.sum(q * q, axis=-1, keepdims=True) + 1e-6) * DK ** -0.5
        k = k * lax.rsqrt(jnp.sum(k * k, axis=-1, keepdims=True) + 1e-6)
        a = jnp.exp(g_row[:, SM_A + h:SM_A + h + 1])
        beta = beta_row[:, SM_BETA + h:SM_BETA + h + 1]
        k_col = _row_to_col(k, DK)
        q_col = _row_to_col(q, DK)
        s0 = s0_ref[0, h]
        u = beta * (v - a * jnp.sum(s0 * k_col, axis=0, keepdims=True))
        s_new = a * s0 + k_col * u
        s_out_ref[0, h] = s_new
        o = jnp.sum(s_new * q_col, axis=0, keepdims=True)
        o = o * lax.rsqrt(jnp.mean(o * o, axis=-1, keepdims=True) + RMS_EPS) * nw
        o_ref[0, :, sl] = o * _silu(z[:, sl])


def _dn_sample(conv_buf, h3, conv_wt, prow, norm_w, s0):
    b = h3.shape[0]
    hw = DN_HEADS * DK
    c0 = COL_DQKV // hw
    row = lambda w, j: pl.BlockSpec((1, 1, w), lambda bi: (bi, 0, j))
    return pl.pallas_call(
        _dn_dec_kernel,
        grid=(b,),
        in_specs=[
            pl.BlockSpec((1, CONV_W - 1, DN_QKV), lambda bi: (bi, 0, 0)),
            row(hw, c0), row(hw, c0 + 1), row(hw, c0 + 2),
            pl.BlockSpec((CONV_W, DN_QKV), lambda bi: (0, 0)),
            row(LANE, COL_SM // LANE),
            row(hw, COL_Z // hw),
            pl.BlockSpec((2, LANE), lambda bi: (0, 0)),
            pl.BlockSpec((1, DV), lambda bi: (0, 0)),
            pl.BlockSpec((1, DN_HEADS, DK, DV), lambda bi: (bi, 0, 0, 0)),
        ],
        out_specs=[row(hw, 0), pl.BlockSpec((1, DN_HEADS, DK, DV), lambda bi: (bi, 0, 0, 0))],
        out_shape=[jax.ShapeDtypeStruct((b, 1, hw), F32), jax.ShapeDtypeStruct(s0.shape, F32)],
        compiler_params=_cp(("parallel",)),
        name="dn_sample",
    )(conv_buf, h3, h3, h3, conv_wt, h3, h3, prow, norm_w, s0)


def _rwkv_prep(r, k, wl, al, w0, a0, k_k, k_a):
    w_log = -_softplus(-(w0 + wl)) - 0.5
    log_decay = -jnp.exp(w_log)
    a = jax.nn.sigmoid(a0 + al)
    kk_raw = k * k_k
    k_h = k * (1.0 + (a - 1.0) * k_a)
    del r
    return log_decay, a, kk_raw, k_h


def _rwkv_chunk_kernel(r_ref, k_ref, v_ref, wl_ref, al_ref, gate_ref, prm_ref, o_ref, s_out_ref, s_ref):
    c = pl.program_id(1)
    C = r_ref.shape[1]
    N = RWKV_HS

    @pl.when(c == 0)
    def _():
        s_ref[...] = jnp.zeros_like(s_ref)

    prm = prm_ref[0]
    r2 = r_ref[0]
    k2 = k_ref[0]
    v2 = v_ref[0]
    log_decay2, a2, kk_raw2, kh2 = _rwkv_prep(r2, k2, wl_ref[0], al_ref[0], prm[0:1], prm[1:2], prm[2:3], prm[3:4])
    ri = lax.broadcasted_iota(jnp.int32, (C, C), 0)
    ci = lax.broadcasted_iota(jnp.int32, (C, C), 1)
    lower = ri >= ci
    strict = ri > ci
    tril = jnp.where(lower, 1.0, 0.0).astype(F32)
    gcum2 = _dot(tril, log_decay2, HI)
    outs = []
    for hh in range(LANE // N):
        sl = slice(hh * N, (hh + 1) * N)
        r = r2[:, sl]
        v = v2[:, sl]
        k_h = kh2[:, sl]
        a = a2[:, sl]
        kk = kk_raw2[:, sl]
        kk = kk / jnp.maximum(jnp.sqrt(jnp.sum(kk * kk, axis=-1, keepdims=True)), 1e-12)
        gc = gcum2[:, sl]
        p_incl = jnp.exp(gc)
        p_excl = jnp.exp(gc - log_decay2[:, sl])
        p_inv = jnp.exp(-gc)
        at = -kk * p_excl
        bt = kk * a * p_inv
        kt = k_h * p_inv
        rt = r * p_incl
        s0 = s_ref[hh]
        l_ab = jnp.where(strict, _dot_nt(at, bt, HI), 0.0)
        l_ak = jnp.where(strict, _dot_nt(at, kt, HI), 0.0)
        rhs = _dot_nt(at, s0, HI) + _dot(l_ak, v, HI)
        u = _dot(_unit_lower_inverse(-l_ab, C), rhs, HI)
        m_rb = jnp.where(lower, _dot_nt(rt, bt, HI), 0.0)
        m_rk = jnp.where(lower, _dot_nt(rt, kt, HI), 0.0)
        out = _dot_nt(rt, s0, HI) + _dot(m_rb, u, HI) + _dot(m_rk, v, HI)
        p_last = p_incl[C - 1:C, :]
        s_ref[hh] = (s0 + _dot_tn(u, bt, HI) + _dot_tn(v, kt, HI)) * p_last
        mu = jnp.mean(out, axis=-1, keepdims=True)
        d = out - mu
        var = jnp.mean(d * d, axis=-1, keepdims=True)
        gn = d * lax.rsqrt(var + GN_EPS) * prm[5:6, sl] + prm[6:7, sl]
        bonus = jnp.sum(r * k_h * prm[4:5, sl], axis=-1, keepdims=True) * v
        outs.append(gn + bonus)
    o_ref[0] = jnp.concatenate(outs, axis=1) * gate_ref[0]

    @pl.when(c == pl.num_programs(1) - 1)
    def _():
        s_out_ref[...] = s_ref[...]


def _rwkv_chunk_prompt(r, k, v, wl, al, gate, prm, *, C):
    npair, t, _ = r.shape
    hpp = LANE // RWKV_HS
    seq = pl.BlockSpec((1, C, LANE), lambda p, c: (p, c, 0))
    return pl.pallas_call(
        _rwkv_chunk_kernel,
        grid=(npair, t // C),
        in_specs=[seq] * 6 + [pl.BlockSpec((1, 8, LANE), lambda p, c: (p, 0, 0))],
        out_specs=[seq, pl.BlockSpec((hpp, RWKV_HS, RWKV_HS), lambda p, c: (p, 0, 0))],
        out_shape=[jax.ShapeDtypeStruct((npair, t, LANE), F32),
                   jax.ShapeDtypeStruct((npair * hpp, RWKV_HS, RWKV_HS), F32)],
        scratch_shapes=[pltpu.VMEM((hpp, RWKV_HS, RWKV_HS), F32)],
        compiler_params=_cp(("parallel", "arbitrary")),
        name="rwkv_chunk",
    )(r, k, v, wl, al, gate, prm)


def _rwkv_dec_kernel(r_ref, k_ref, v_ref, wl_ref, al_ref, gate_ref, prm_ref, s0_ref, o_ref, s_out_ref):
    N = RWKV_HS
    prm = prm_ref[...]
    r2 = r_ref[0]
    k2 = k_ref[0]
    v2 = v_ref[0]
    log_decay2, a2, kk_raw2, kh2 = _rwkv_prep(r2, k2, wl_ref[0], al_ref[0], prm[0:1], prm[1:2], prm[2:3], prm[3:4])
    w2 = jnp.exp(log_decay2)
    gate = gate_ref[0]
    nh = r2.shape[1] // N
    for h in range(nh):
        sl = slice(h * N, (h + 1) * N)
        r = r2[:, sl]
        v = v2[:, sl]
        k_h = kh2[:, sl]
        kk = kk_raw2[:, sl]
        kk = kk / jnp.maximum(jnp.sqrt(jnp.sum(kk * kk, axis=-1, keepdims=True)), 1e-12)
        s0 = s0_ref[0, h]
        sa = jnp.sum(s0 * (-kk), axis=1, keepdims=True)
        s_new = s0 * w2[:, sl] + sa * (kk * a2[:, sl]) + _row_to_col(v, N) * k_h
        s_out_ref[0, h] = s_new
        out = _col_to_row(jnp.sum(s_new * r, axis=1, keepdims=True), N)
        mu = jnp.mean(out, axis=-1, keepdims=True)
        d = out - mu
        var = jnp.mean(d * d, axis=-1, keepdims=True)
        gn = d * lax.rsqrt(var + GN_EPS) * prm[5:6, sl] + prm[6:7, sl]
        bonus = jnp.sum(r * k_h * prm[4:5, sl], axis=-1, keepdims=True) * v
        o_ref[0, :, sl] = (gn + bonus) * gate[:, sl]


def _rwkv_sample(r, k, v, wl, al, gate, prm, s0):
    b, _, d = r.shape
    row = pl.BlockSpec((1, 1, d), lambda bi: (bi, 0, 0))
    st = pl.BlockSpec((1,) + s0.shape[1:], lambda bi: (bi, 0, 0, 0))
    return pl.pallas_call(
        _rwkv_dec_kernel,
        grid=(b,),
        in_specs=[row] * 6 + [pl.BlockSpec((8, d), lambda bi: (0, 0)), st],
        out_specs=[row, st],
        out_shape=[jax.ShapeDtypeStruct((b, 1, d), F32), jax.ShapeDtypeStruct(s0.shape, F32)],
        compiler_params=_cp(("parallel",)),
        name="rwkv_sample",
    )(r, k, v, wl, al, gate, prm, s0)


def _alibi_slopes():
    return jnp.asarray(2.0 ** (-8.0 * np.arange(1, NSA_HEADS + 1) / NSA_HEADS), dtype=F32)


def _overlap_matrix(nch, nsp):
    cstart = np.arange(nch)[:, None] * D_CMP
    sstart = np.arange(nsp)[None, :] * L_SLC
    return jnp.asarray(((cstart < sstart + L_SLC) & (cstart + L_CMP > sstart)).astype(np.float32))


def _pack_w_in(w):
    offs = np.concatenate([[0], np.cumsum(EVEN_SPLIT)])
    qa, kvc, kvs, kvw, ga, qkv, z, b, a = [w[:, offs[i]:offs[i + 1]] for i in range(len(EVEN_SPLIT))]
    used = COL_SM + A_GATES + 2 * DN_HEADS
    pad = jnp.zeros((w.shape[0], E_IN_PAD - used), w.dtype)
    return jnp.concatenate([qa, qkv, z, kvc, kvs, kvw, ga, b, a, pad], axis=1).astype(BF16)


def _pack_cmp_w1(w1):
    hid = w1.shape[-1]
    w = w1.reshape(2, 2, D_CMP, HD, hid).transpose(0, 2, 3, 1, 4)
    return w.reshape(2, D_CMP, HD, 2 * hid).astype(BF16)


def _pack_cmp_pe(pe):
    p = pe.reshape(2, 2, D_CMP, HD).transpose(0, 2, 1, 3)
    return jnp.concatenate([p, jnp.zeros((2, D_CMP, 6, HD), pe.dtype)], axis=2)


def _dn_gate_params(a_log, dt_bias):
    row = jnp.zeros((2, LANE), F32).at[0, SM_A:SM_A + DN_HEADS].set(a_log).at[1, SM_A:SM_A + DN_HEADS].set(dt_bias)
    return row, row.T


def _even_layer(xp, xs, w_in, w_out, pe, w1, b1, w2, conv_w, a_log, dt_bias, norm_w,
                cache_cmp, cache_slc, win_buf, conv_buf, dn_s0, page_table, g, b):
    t = xp.shape[0]
    bs = xs.shape[0]
    n_pages = page_table.shape[1]
    past = n_pages * PAGE_ROWS
    slopes = _alibi_slopes()
    w_in_p = _pack_w_in(w_in)
    w4 = _pack_cmp_w1(w1)
    pe8 = _pack_cmp_pe(pe)
    w2b = w2.astype(BF16)
    conv_wt = conv_w.T
    prow, pcol = _dn_gate_params(a_log, dt_bias)
    nw = norm_w.reshape(1, DV)

    hp = _mm(xp, w_in_p)
    hs = _mm(xs, w_in_p)
    kvc_p = hp[:, COL_KVC:COL_KVC + A_KVW]
    kvs_p = hp[:, COL_KVS:COL_KVS + A_KVW]
    kvw_p = hp[:, COL_KVW:COL_KVW + A_KVW]
    kvc_s = hs[:, COL_KVC:COL_KVC + A_KVW]
    kvs_s = hs[:, COL_KVS:COL_KVS + A_KVW]
    kvw_s = hs[:, COL_KVW:COL_KVW + A_KVW]

    TQ = _tile(t, (128, 64, 32, 16, 8))
    nch = t // D_CMP
    ns = t // L_SLC
    nsp = -(-ns // LANE) * LANE
    arange_pt = jnp.arange(t // PAGE_ROWS, dtype=jnp.int32)[None]
    h1 = _cmp_stage1(kvc_p.reshape(t // PAGE_ROWS, CH_PER_PAGE, CH_FEAT), arange_pt, w4)
    cmp_p = _cmp_stage2(h1, pe8, w4, b1, w2b)
    q_p = hp[:, COL_QA:COL_QA + A_Q]
    o_cmp, sel = _nsa_select(q_p[None], cmp_p, _overlap_matrix(nch, nsp), slopes,
                             TQ=TQ, NC=(t - L_CMP) // D_CMP + 1, NS=ns, q_off=0)
    kvb = hp[:, COL_KVC:COL_KVC + 3 * A_KVW].astype(BF16)
    o_slc = _nsa_slc_prompt(q_p, sel[0], kvb, slopes, TQ=TQ, TK=_tile(t, (512, 256, 128, 64)))
    o_a_p = _nsa_win_prompt(q_p, kvb, hp, o_cmp[0], o_slc, slopes, TQ=TQ)

    nch_s = n_pages * CH_PER_PAGE
    nc_s = (past + 1 - L_CMP) // D_CMP + 1
    ns_s = -(-(past + 1) // L_SLC)
    nsp_s = -(-ns_s // LANE) * LANE
    h1s = _cmp_stage1(cache_cmp.reshape(cache_cmp.shape[0], CH_PER_PAGE, CH_FEAT), page_table, w4)
    cmp_s = _cmp_stage2(h1s, pe8, w4, b1, w2b)
    hs3 = hs[:, None, :]
    q_s = hs3[:, :, COL_QA:COL_QA + A_Q]
    o_cmp_s, sel_s = _nsa_select(q_s, cmp_s, _overlap_matrix(nch_s, nsp_s), slopes,
                                 TQ=1, NC=nc_s, NS=ns_s, q_off=past)
    o_slc_s = _nsa_slc_sample(q_s, sel_s[:, :, 0, :], kvs_s[:, None, :],
                              cache_slc.reshape(cache_slc.shape[0], PAGE_ROWS, A_KVW), page_table, slopes)
    wb = win_buf.reshape(bs, win_buf.shape[1], A_KVW)
    o_a_s = _nsa_win_sample(q_s, wb, kvw_s[:, None, :], hs3[:, :, COL_SM:COL_SM + LANE], o_cmp_s, o_slc_s,
                            slopes, past=past)

    qkvn = _dn_conv_prompt(hp, jnp.zeros((8, DN_QKV), F32), conv_wt)
    smt = hp[:, COL_SM:COL_SM + LANE].T
    o_b_p, dn_s_p = _dn_chunk_prompt(qkvn, hp, smt, prow, pcol, nw, C=_tile(t, (128,)))
    o_b_s, dn_s_s = _dn_sample(conv_buf, hs3, conv_wt, prow, nw, dn_s0)

    w_out_b = w_out.astype(BF16)
    yp = _mm_ln(jnp.concatenate([o_a_p, o_b_p], axis=1), w_out_b, xp, g, b)
    ys = _mm_ln(jnp.concatenate([o_a_s[:, 0], o_b_s[:, 0]], axis=1), w_out_b, xs, g, b)

    kv6 = lambda a: a.reshape(a.shape[:-1] + (2, NSA_KV, HD))
    raw_p = hp[:, COL_DQKV:COL_DQKV + DN_QKV]
    raw_s = hs[:, COL_DQKV:COL_DQKV + DN_QKV]
    wlen = min(WINDOW, t)
    outs = dict(
        cmp_p=kv6(kvc_p)[None], cmp_s=kv6(kvc_s)[:, None],
        slc_p=kv6(kvs_p)[None], slc_s=kv6(kvs_s)[:, None],
        win_p=kv6(kvw_p[t - wlen:])[None],
        win_s=jnp.concatenate([win_buf, kv6(kvw_s)[:, None]], axis=1)[:, 1:],
        conv_p=jnp.concatenate([jnp.zeros((CONV_W - 1, DN_QKV), F32), raw_p], axis=0)[t:][None],
        conv_s=jnp.concatenate([conv_buf, raw_s[:, None]], axis=1)[:, 1:],
        dns_p=dn_s_p[None], dns_s=dn_s_s,
    )
    return yp, ys, outs


def _odd_layer(xp, xs, shift_s, s0_s, mix, wr, wk, wv, wo, w0, w1, w2, a0, a1, a2, g1, g2, k_k, k_a, r_k,
               ln_w, ln_b, g, b):
    t, d = xp.shape
    bs = xs.shape[0]
    npair = d // LANE
    xprev_p = jnp.concatenate([jnp.zeros((1, d), F32), xp[:-1]], axis=0)
    xprev_s = shift_s

    def padk(wa, wb_):
        r = wa.shape[1]
        rp = -(-r // LANE) * LANE
        return (jnp.pad(wa, ((0, 0), (0, rp - r))).astype(BF16), jnp.pad(wb_, ((0, rp - r), (0, 0))).astype(BF16))

    wrb, wkb, wvb, wob = (w.astype(BF16) for w in (wr, wk, wv, wo))
    w1b, w2b = padk(w1, w2)
    a1b, a2b = padk(a1, a2)
    g1b, g2b = padk(g1, g2)
    prm = jnp.stack([w0, a0, k_k, k_a, r_k.reshape(d), ln_w, ln_b, jnp.zeros((d,), F32)])

    def proj(x, xprev, pair_out):
        mr = lambda i: mix[i:i + 1]
        r = _mm(x, wrb, xprev=xprev, mixrow=mr(0), pair_out=pair_out)
        wl = _mm(_mm(x, w1b, xprev=xprev, mixrow=mr(1), act="tanh"), w2b, pair_out=pair_out)
        k = _mm(x, wkb, xprev=xprev, mixrow=mr(2), pair_out=pair_out)
        v = _mm(x, wvb, xprev=xprev, mixrow=mr(3), pair_out=pair_out)
        al = _mm(_mm(x, a1b, xprev=xprev, mixrow=mr(4)), a2b, pair_out=pair_out)
        gate = _mm(_mm(x, g1b, xprev=xprev, mixrow=mr(5), act="sigmoid"), g2b, pair_out=pair_out)
        return r, k, v, wl, al, gate

    pp = proj(xp, xprev_p, True)
    prm_pair = prm.reshape(8, npair, LANE).transpose(1, 0, 2)
    y_p, s_p = _rwkv_chunk_prompt(*pp, prm_pair, C=_tile(t, (64, 32, 16, 8)))
    yp = _mm_ln(y_p, wob, xp, g, b, pair_in=True)

    ps = [a[:, None, :] for a in proj(xs, xprev_s, False)]
    y_s, s_s = _rwkv_sample(*ps, prm, s0_s)
    ys = _mm_ln(y_s[:, 0], wob, xs, g, b)
    return yp, ys, dict(shift_p=xp[t - 1:t], shift_s=xs, rs_p=s_p[None], rs_s=s_s)


def kernel(x_prompt, x_sample, cache_nsa_cmp, cache_nsa_slc, cache_nsa_win, state_dn_conv, state_dn_S, state_rwkv_shift, state_rwkv_S, page_table, ln_g, ln_b, ffn_wi, ffn_wo, mix_w_in, mix_w_out, nsa_cmp_pe, nsa_cmp_w1, nsa_cmp_b1, nsa_cmp_w2, dn_conv_w, dn_a_log, dn_dt_bias, dn_norm_w, rwkv_mix, rwkv_wr, rwkv_wk, rwkv_wv, rwkv_wo, rwkv_w0, rwkv_w1, rwkv_w2, rwkv_a0, rwkv_a1, rwkv_a2, rwkv_g1, rwkv_g2, rwkv_k_k, rwkv_k_a, rwkv_r_k, rwkv_ln_w, rwkv_ln_b):
    bp, t, d = x_prompt.shape
    assert bp == 1 and x_sample.shape[1] == 1
    depth = ffn_wi.shape[0]
    xp = x_prompt[0]
    xs = x_sample[:, 0]
    even, odd = [], []
    for l in range(depth):
        gl = lambda i: (ln_g[l, i][None], ln_b[l, i][None])
        wi = ffn_wi[l].astype(BF16)
        wo = ffn_wo[l].astype(BF16)
        xp = _ffn_ln(xp, wi[0], wo[0], *gl(0))
        xs = _ffn_ln(xs, wi[0], wo[0], *gl(0))
        if l % 2 == 0:
            e = l // 2
            xp, xs, o = _even_layer(
                xp, xs, mix_w_in[e], mix_w_out[e], nsa_cmp_pe[e], nsa_cmp_w1[e], nsa_cmp_b1[e], nsa_cmp_w2[e],
                dn_conv_w[e], dn_a_log[e], dn_dt_bias[e], dn_norm_w[e], cache_nsa_cmp[e], cache_nsa_slc[e],
                cache_nsa_win[e], state_dn_conv[e], state_dn_S[e], page_table, *gl(1))
            even.append(o)
        else:
            c = l // 2
            xp, xs, o = _odd_layer(
                xp, xs, state_rwkv_shift[c], state_rwkv_S[c], rwkv_mix[c], rwkv_wr[c], rwkv_wk[c], rwkv_wv[c],
                rwkv_wo[c], rwkv_w0[c], rwkv_w1[c], rwkv_w2[c], rwkv_a0[c], rwkv_a1[c], rwkv_a2[c], rwkv_g1[c],
                rwkv_g2[c], rwkv_k_k[c], rwkv_k_a[c], rwkv_r_k[c], rwkv_ln_w[c], rwkv_ln_b[c], *gl(1))
            odd.append(o)
        xp = _ffn_ln(xp, wi[1], wo[1], *gl(2))
        xs = _ffn_ln(xs, wi[1], wo[1], *gl(2))
    st = lambda lst, key: jnp.stack([o[key] for o in lst])
    return (xp[None], xs[:, None],
            st(even, "cmp_p"), st(even, "cmp_s"), st(even, "slc_p"), st(even, "slc_s"),
            st(even, "win_p"), st(even, "win_s"), st(even, "conv_p"), st(even, "conv_s"),
            st(even, "dns_p"), st(even, "dns_s"),
            st(odd, "shift_p"), st(odd, "shift_s"), st(odd, "rs_p"), st(odd, "rs_s"))
```

```python
import functools

import numpy as np
import jax
import jax.numpy as jnp
from jax import lax
from jax.experimental import pallas as pl
from jax.experimental.pallas import tpu as pltpu

F32 = jnp.float32
BF16 = jnp.bfloat16
HI = lax.Precision.HIGHEST

DEPTH = 2
ALPHA = (2 * DEPTH) ** 0.25
LN_EPS = 1e-5
RMS_EPS = 1e-6
NSA_HEADS = 8
NSA_KV = 2
NSA_GROUP = NSA_HEADS // NSA_KV
HD = 128
L_CMP = 32
D_CMP = 16
L_SLC = 64
N_SEL = 16
WINDOW = 512
NEG_INF = -1e30
FORCE_SCORE = 1e6
DN_HEADS = 8
DK = 128
DV = 128
CONV_W = 4
RWKV_HS = 64
GN_EPS = 64e-5

A_Q = NSA_HEADS * HD
A_KVW = 2 * NSA_KV * HD
A_GATES = 3 * NSA_HEADS
DN_QKV = DN_HEADS * (2 * DK + DV)
EVEN_SPLIT = (A_Q, A_KVW, A_KVW, A_KVW, A_GATES, DN_QKV, DN_HEADS * DV, DN_HEADS, DN_HEADS)
COL_QA = 0
COL_DQKV = 1024
COL_Z = 4096
COL_KVC = 5120
COL_KVS = 5632
COL_KVW = 6144
COL_SM = 6656
SM_BETA = A_GATES
SM_A = A_GATES + DN_HEADS
E_IN_PAD = 7168
LANE = 128
PAGE_ROWS = 128

VMEM_LIMIT = 56 * 1024 * 1024


def _cp(sem):
    return pltpu.CompilerParams(dimension_semantics=sem, vmem_limit_bytes=VMEM_LIMIT)


def _tile(n, prefs):
    for t in prefs:
        if n % t == 0:
            return t
    return n


def _dot(a, b, precision=None):
    return jnp.dot(a, b, preferred_element_type=F32, precision=precision)


def _dot_nt(a, b, precision=None):
    return lax.dot_general(a, b, (((1,), (1,)), ((), ())), preferred_element_type=F32, precision=precision)


def _dot_tn(a, b, precision=None):
    return lax.dot_general(a, b, (((0,), (0,)), ((), ())), preferred_element_type=F32, precision=precision)


def _layernorm_rows(y, g, b):
    mu = jnp.mean(y, axis=-1, keepdims=True)
    d = y - mu
    var = jnp.mean(d * d, axis=-1, keepdims=True)
    return d * lax.rsqrt(var + LN_EPS) * g + b


def _softplus(x):
    return jnp.maximum(x, 0.0) + jnp.log1p(jnp.exp(-jnp.abs(x)))


def _silu(x):
    return x * jax.nn.sigmoid(x)


def _ffn_kernel(x_ref, wg_ref, wu_ref, wo_ref, g_ref, b_ref, o_ref, xb_ref, acc_ref, *, nf):
    f = pl.program_id(1)

    @pl.when(f == 0)
    def _():
        xb_ref[...] = x_ref[...].astype(BF16)
        acc_ref[...] = jnp.zeros_like(acc_ref)

    xb = xb_ref[...]
    gate = _dot(xb, wg_ref[...])
    up = _dot(xb, wu_ref[...])
    act = (_silu(gate) * up).astype(BF16)
    acc_ref[...] += _dot(act, wo_ref[...])

    @pl.when(f == nf - 1)
    def _():
        y = ALPHA * x_ref[...] + 0.5 * acc_ref[...]
        o_ref[...] = _layernorm_rows(y, g_ref[...], b_ref[...])


def _ffn_ln(x, wi, wo, g, b):
    m, d = x.shape
    f = wo.shape[0]
    tm = _tile(m, (512, 256, 128, 64, 32, 16, 8))
    tf = _tile(f, (512, 256, 128))
    nf = f // tf
    return pl.pallas_call(
        functools.partial(_ffn_kernel, nf=nf),
        grid=(m // tm, nf),
        in_specs=[
            pl.BlockSpec((tm, d), lambda i, j: (i, 0)),
            pl.BlockSpec((d, tf), lambda i, j: (0, j)),
            pl.BlockSpec((d, tf), lambda i, j: (0, j + nf)),
            pl.BlockSpec((tf, d), lambda i, j: (j, 0)),
            pl.BlockSpec((1, d), lambda i, j: (0, 0)),
            pl.BlockSpec((1, d), lambda i, j: (0, 0)),
        ],
        out_specs=pl.BlockSpec((tm, d), lambda i, j: (i, 0)),
        out_shape=jax.ShapeDtypeStruct((m, d), F32),
        scratch_shapes=[pltpu.VMEM((tm, d), BF16), pltpu.VMEM((tm, d), F32)],
        compiler_params=_cp(("parallel", "arbitrary")),
        name="ffn_ln",
    )(x, wi, wi, wo, g, b)


def _mm_kernel(*refs, mix, act, pair_out):
    if mix:
        x_ref, xp_ref, m_ref, w_ref, o_ref = refs
        x = x_ref[...]
        x = x + (xp_ref[...] - x) * m_ref[...]
    else:
        x_ref, w_ref, o_ref = refs
        x = x_ref[...]
    y = _dot(x.astype(BF16), w_ref[...])
    if act == "tanh":
        y = jnp.tanh(y)
    elif act == "sigmoid":
        y = jax.nn.sigmoid(y)
    if pair_out:
        for p in range(o_ref.shape[0]):
            o_ref[p] = y[:, p * LANE:(p + 1) * LANE]
    else:
        o_ref[...] = y


def _mm(x, w, *, xprev=None, mixrow=None, act=None, pair_out=False, tn_prefs=(512, 256, 128)):
    m, k = x.shape
    n = w.shape[1]
    tm = _tile(m, (512, 256, 128, 64, 32, 16, 8))
    tn = _tile(n, tn_prefs)
    mix = xprev is not None
    in_specs = [pl.BlockSpec((tm, k), lambda i, j: (i, 0))]
    args = [x]
    if mix:
        in_specs += [pl.BlockSpec((tm, k), lambda i, j: (i, 0)), pl.BlockSpec((1, k), lambda i, j: (0, 0))]
        args += [xprev, mixrow]
    in_specs.append(pl.BlockSpec((k, tn), lambda i, j: (0, j)))
    args.append(w)
    if pair_out:
        npb = tn // LANE
        out_spec = pl.BlockSpec((npb, tm, LANE), lambda i, j: (j, i, 0))
        out_shape = jax.ShapeDtypeStruct((n // LANE, m, LANE), F32)
    else:
        out_spec = pl.BlockSpec((tm, tn), lambda i, j: (i, j))
        out_shape = jax.ShapeDtypeStruct((m, n), F32)
    return pl.pallas_call(
        functools.partial(_mm_kernel, mix=mix, act=act, pair_out=pair_out),
        grid=(m // tm, n // tn),
        in_specs=in_specs,
        out_specs=out_spec,
        out_shape=out_shape,
        compiler_params=_cp(("parallel", "parallel")),
        name="matmul",
    )(*args)


def _mmln_kernel(a_ref, w_ref, x_ref, g_ref, b_ref, o_ref, *, pair_in):
    if pair_in:
        a = jnp.concatenate([a_ref[p].astype(BF16) for p in range(a_ref.shape[0])], axis=1)
    else:
        a = a_ref[...].astype(BF16)
    y = _dot(a, w_ref[...])
    o_ref[...] = _layernorm_rows(ALPHA * x_ref[...] + y, g_ref[...], b_ref[...])


def _mm_ln(a, w, x, g, b, *, pair_in=False):
    m, d = x.shape
    k = w.shape[0]
    tm = _tile(m, (512, 256, 128, 64, 32, 16, 8))
    if pair_in:
        a_spec = pl.BlockSpec((k // LANE, tm, LANE), lambda i: (0, i, 0))
    else:
        a_spec = pl.BlockSpec((tm, k), lambda i: (i, 0))
    return pl.pallas_call(
        functools.partial(_mmln_kernel, pair_in=pair_in),
        grid=(m // tm,),
        in_specs=[
            a_spec,
            pl.BlockSpec((k, d), lambda i: (0, 0)),
            pl.BlockSpec((tm, d), lambda i: (i, 0)),
            pl.BlockSpec((1, d), lambda i: (0, 0)),
            pl.BlockSpec((1, d), lambda i: (0, 0)),
        ],
        out_specs=pl.BlockSpec((tm, d), lambda i: (i, 0)),
        out_shape=jax.ShapeDtypeStruct((m, d), F32),
        compiler_params=_cp(("parallel",)),
        name="matmul_ln",
    )(a, w, x, g, b)


CH_PER_PAGE = PAGE_ROWS // D_CMP
CH_FEAT = D_CMP * A_KVW


def _cmp1_kernel(pt_ref, *refs, G):
    del pt_ref
    page_refs = refs[:G]
    w_ref, o_ref, xs_ref = refs[G:]
    for j in range(G):
        xs_ref[j * CH_PER_PAGE:(j + 1) * CH_PER_PAGE, :] = page_refs[j][0]
    rows = G * CH_PER_PAGE
    for sg in range(2 * NSA_KV):
        s = sg // NSA_KV
        acc = jnp.zeros((rows, 2 * HD), F32)
        for p in range(D_CMP):
            c0 = p * A_KVW + sg * HD
            acc = acc + _dot(xs_ref[:, c0:c0 + HD].astype(BF16), w_ref[s, p])
        o_ref[0, :, sg * 2 * HD:(sg + 1) * 2 * HD] = acc


def _cmp_stage1(pool, page_table, w4):
    b, n_pages = page_table.shape
    G = _tile(n_pages, (8, 4, 2, 1))

    def page_map(j):
        return lambda bi, p, pt: (pt[bi, p * G + j], 0, 0)

    in_specs = [pl.BlockSpec((1, CH_PER_PAGE, CH_FEAT), page_map(j)) for j in range(G)]
    in_specs.append(pl.BlockSpec(w4.shape, lambda bi, p, pt: (0, 0, 0, 0)))
    grid_spec = pltpu.PrefetchScalarGridSpec(
        num_scalar_prefetch=1,
        grid=(b, n_pages // G),
        in_specs=in_specs,
        out_specs=pl.BlockSpec((1, G * CH_PER_PAGE, 8 * HD), lambda bi, p, pt: (bi, p, 0)),
        scratch_shapes=[pltpu.VMEM((G * CH_PER_PAGE, CH_FEAT), F32)],
    )
    return pl.pallas_call(
        functools.partial(_cmp1_kernel, G=G),
        grid_spec=grid_spec,
        out_shape=jax.ShapeDtypeStruct((b, n_pages * CH_PER_PAGE, 8 * HD), F32),
        compiler_params=_cp(("parallel", "arbitrary")),
        name="nsa_cmp_stage1",
    )(page_table, *([pool] * G), w4)


def _gelu_tanh(x):
    return 0.5 * x * (1.0 + jnp.tanh(np.sqrt(2.0 / np.pi).astype(np.float32) * (x + 0.044715 * (x * x * x))))


def _cmp2_kernel(h_ref, pe_ref, w4_ref, b1_ref, w2_ref, o_ref):
    nch = h_ref.shape[1]
    for s in range(2):
        pacc = jnp.zeros((8, 2 * HD), F32)
        for p in range(D_CMP):
            pacc = pacc + _dot(pe_ref[s, p].astype(BF16), w4_ref[s, p])
        const = pacc[0:1, 0:HD] + pacc[1:2, HD:2 * HD] + b1_ref[s:s + 1, :]
        for gi in range(NSA_KV):
            sg = s * NSA_KV + gi
            h0 = h_ref[0, :, sg * 2 * HD:sg * 2 * HD + HD]
            h1 = h_ref[0, :, sg * 2 * HD + HD:(sg + 1) * 2 * HD]
            hid = h0 + pltpu.roll(h1, nch - 1, 0) + const
            o_ref[0, :, sg * HD:(sg + 1) * HD] = _dot(_gelu_tanh(hid).astype(BF16), w2_ref[s])


def _cmp_stage2(h, pe8, w4, b1, w2):
    b, nch, _ = h.shape
    return pl.pallas_call(
        _cmp2_kernel,
        grid=(b,),
        in_specs=[
            pl.BlockSpec((1, nch, 8 * HD), lambda i: (i, 0, 0)),
            pl.BlockSpec(pe8.shape, lambda i: (0, 0, 0, 0)),
            pl.BlockSpec(w4.shape, lambda i: (0, 0, 0, 0)),
            pl.BlockSpec(b1.shape, lambda i: (0, 0)),
            pl.BlockSpec(w2.shape, lambda i: (0, 0, 0)),
        ],
        out_specs=pl.BlockSpec((1, nch, A_KVW), lambda i: (i, 0, 0)),
        out_shape=jax.ShapeDtypeStruct((b, nch, A_KVW), F32),
        compiler_params=_cp(("parallel",)),
        name="nsa_cmp_stage2",
    )(h, pe8, w4, b1, w2)


def _nsa_sel_kernel(slopes_ref, q_ref, ck_ref, cv_ref, ov_ref, ocmp_ref, sel_ref, *, TQ, NC, NS, q_off, n_pick):
    i = pl.program_id(1)
    g = pl.program_id(2)
    nch = ck_ref.shape[1]
    nsp = ov_ref.shape[1]
    qpos = q_off + i * TQ + lax.broadcasted_iota(jnp.int32, (TQ, 1), 0)
    cidx = lax.broadcasted_iota(jnp.int32, (1, nch), 1)
    cstart = cidx * D_CMP
    cmask = jnp.logical_and(cstart + (L_CMP - 1) <= qpos, cidx < NC)
    cdist = (qpos - cstart).astype(F32) - 0.5 * (L_CMP - 1)
    kb = ck_ref[0].astype(BF16)
    vb = cv_ref[0].astype(BF16)
    q = q_ref[0]
    pcs = jnp.zeros((TQ, nch), F32)
    for j in range(NSA_GROUP):
        qj = (q[:, j * HD:(j + 1) * HD] * HD ** -0.5).astype(BF16)
        slope = slopes_ref[NSA_GROUP * g + j]
        lg = jnp.where(cmask, _dot_nt(qj, kb) - slope * cdist, NEG_INF)
        e = jnp.exp(lg - jnp.max(lg, axis=-1, keepdims=True))
        p = jnp.where(cmask, e / jnp.sum(e, axis=-1, keepdims=True), 0.0)
        pcs = pcs + p
        ocmp_ref[0, :, j * HD:(j + 1) * HD] = _dot(p.astype(BF16), vb)
    imp = _dot(pcs, ov_ref[...], precision=HI)
    sid = lax.broadcasted_iota(jnp.int32, (1, nsp), 1)
    cur = lax.shift_right_arithmetic(qpos, int(np.log2(L_SLC)))
    svalid = jnp.logical_and(sid * L_SLC <= qpos, sid < NS)
    forced = jnp.logical_or(sid == 0, jnp.logical_or(sid == cur, sid == cur - 1))
    score = jnp.where(svalid, jnp.where(forced, FORCE_SCORE, imp), NEG_INF)
    sidf = jnp.broadcast_to(sid, (TQ, nsp))
    sel = jnp.zeros((TQ, nsp), F32)
    for _ in range(n_pick):
        mx = jnp.max(score, axis=-1, keepdims=True)
        first = jnp.min(jnp.where(score == mx, sidf, nsp), axis=-1, keepdims=True)
        pick = sidf == first
        sel = jnp.where(jnp.logical_and(pick, mx > 0.5 * NEG_INF), 1.0, sel)
        score = jnp.where(pick, -jnp.inf, score)
    sel_ref[0, 0] = sel


def _nsa_select(q, cmp, overlap, slopes, *, TQ, NC, NS, q_off):
    b, t, _ = q.shape
    nch = cmp.shape[1]
    nsp = overlap.shape[1]
    kern = functools.partial(_nsa_sel_kernel, TQ=TQ, NC=NC, NS=NS, q_off=q_off, n_pick=min(N_SEL, NS))
    return pl.pallas_call(
        kern,
        grid=(b, t // TQ, NSA_KV),
        in_specs=[
            pl.BlockSpec(memory_space=pltpu.SMEM),
            pl.BlockSpec((1, TQ, NSA_GROUP * HD), lambda bi, i, g: (bi, i, g)),
            pl.BlockSpec((1, nch, HD), lambda bi, i, g: (bi, 0, g)),
            pl.BlockSpec((1, nch, HD), lambda bi, i, g: (bi, 0, NSA_KV + g)),
            pl.BlockSpec((nch, nsp), lambda bi, i, g: (0, 0)),
        ],
        out_specs=[
            pl.BlockSpec((1, TQ, NSA_GROUP * HD), lambda bi, i, g: (bi, i, g)),
            pl.BlockSpec((1, 1, TQ, nsp), lambda bi, i, g: (bi, g, i, 0)),
        ],
        out_shape=[jax.ShapeDtypeStruct((b, t, A_Q), F32), jax.ShapeDtypeStruct((b, NSA_KV, t, nsp), F32)],
        compiler_params=_cp(("parallel", "parallel", "parallel")),
        name="nsa_cmp_select",
    )(slopes, q, cmp, cmp, overlap)


def _stack_heads(q, tq):
    del tq
    return jnp.concatenate([q[:, j * HD:(j + 1) * HD] for j in range(NSA_GROUP)], axis=0)


def _group_slopes(slopes_ref, g, tq):
    return jnp.concatenate(
        [jnp.full((tq, 1), slopes_ref[NSA_GROUP * g + j], F32) for j in range(NSA_GROUP)], axis=0)


def _online_step(carry, lg, mask, v):
    m, l, acc = carry
    m_new = jnp.maximum(m, jnp.max(lg, axis=-1, keepdims=True))
    p = jnp.where(mask, jnp.exp(lg - m_new), 0.0)
    a = jnp.exp(m - m_new)
    l = a * l + jnp.sum(p, axis=-1, keepdims=True)
    acc = a * acc + _dot(p.astype(BF16), v)
    return m_new, l, acc


def _nsa_slc_kernel(slopes_ref, q_ref, sel_ref, k_ref, v_ref, o_ref, *, TQ, TK):
    g = pl.program_id(0)
    i = pl.program_id(1)
    nsp = sel_ref.shape[-1]
    rows = NSA_GROUP * TQ
    q4 = (_stack_heads(q_ref[...], TQ) * HD ** -0.5).astype(BF16)
    qpos = i * TQ + lax.broadcasted_iota(jnp.int32, (TQ, 1), 0)
    qpos4 = jnp.concatenate([qpos] * NSA_GROUP, axis=0)
    slope4 = _group_slopes(slopes_ref, g, TQ)
    selb = sel_ref[0].astype(BF16)
    blk_per_tile = TK // L_SLC
    srow = lax.broadcasted_iota(jnp.int32, (nsp, TK), 0)
    scol = lax.broadcasted_iota(jnp.int32, (nsp, TK), 1) // L_SLC
    kcol = lax.broadcasted_iota(jnp.int32, (1, TK), 1)

    def body(kt, carry):
        k0 = pl.multiple_of(kt * TK, TK)
        k = k_ref[pl.ds(k0, TK), :]
        v = v_ref[pl.ds(k0, TK), :]
        s = _dot_nt(q4, k)
        dist = qpos4 - (k0 + kcol)
        expand = jnp.where(srow == kt * blk_per_tile + scol, 1.0, 0.0).astype(BF16)
        se = _dot(selb, expand)
        se4 = jnp.concatenate([se] * NSA_GROUP, axis=0)
        mask = jnp.logical_and(se4 > 0.5, dist >= 0)
        lg = jnp.where(mask, s - slope4 * dist.astype(F32), NEG_INF)
        return _online_step(carry, lg, mask, v)

    ntile = (i * TQ + TQ + TK - 1) // TK
    init = (jnp.full((rows, 1), NEG_INF, F32), jnp.zeros((rows, 1), F32), jnp.zeros((rows, HD), F32))
    _, l, acc = lax.fori_loop(0, ntile, body, init)
    o = acc / l
    for j in range(NSA_GROUP):
        o_ref[:, j * HD:(j + 1) * HD] = o[j * TQ:(j + 1) * TQ]


def _nsa_slc_prompt(q, sel, kvb, slopes, *, TQ, TK):
    t = q.shape[0]
    nsp = sel.shape[-1]
    kb0 = A_KVW // HD
    return pl.pallas_call(
        functools.partial(_nsa_slc_kernel, TQ=TQ, TK=TK),
        grid=(NSA_KV, t // TQ),
        in_specs=[
            pl.BlockSpec(memory_space=pltpu.SMEM),
            pl.BlockSpec((TQ, NSA_GROUP * HD), lambda g, i: (i, g)),
            pl.BlockSpec((1, TQ, nsp), lambda g, i: (g, i, 0)),
            pl.BlockSpec((t, HD), lambda g, i: (0, kb0 + g)),
            pl.BlockSpec((t, HD), lambda g, i: (0, kb0 + NSA_KV + g)),
        ],
        out_specs=pl.BlockSpec((TQ, NSA_GROUP * HD), lambda g, i: (i, g)),
        out_shape=jax.ShapeDtypeStruct((t, A_Q), F32),
        compiler_params=_cp(("parallel", "parallel")),
        name="nsa_slc_prompt",
    )(slopes, q, sel, kvb, kvb)


def _nsa_win_kernel(slopes_ref, q_ref, k_ref, v_ref, ga_ref, ocmp_ref, oslc_ref, o_ref, *, TQ):
    g = pl.program_id(0)
    i = pl.program_id(1)
    rows = NSA_GROUP * TQ
    q4 = (_stack_heads(q_ref[...], TQ) * HD ** -0.5).astype(BF16)
    qpos = i * TQ + lax.broadcasted_iota(jnp.int32, (TQ, 1), 0)
    qpos4 = jnp.concatenate([qpos] * NSA_GROUP, axis=0)
    slope4 = _group_slopes(slopes_ref, g, TQ)
    kcol = lax.broadcasted_iota(jnp.int32, (1, TQ), 1)

    def body(kt, carry):
        k0 = pl.multiple_of(kt * TQ, TQ)
        k = k_ref[pl.ds(k0, TQ), :]
        v = v_ref[pl.ds(k0, TQ), :]
        s = _dot_nt(q4, k)
        dist = qpos4 - (k0 + kcol)
        mask = jnp.logical_and(dist >= 0, dist <= WINDOW)
        lg = jnp.where(mask, s - slope4 * dist.astype(F32), NEG_INF)
        return _online_step(carry, lg, mask, v)

    lo = jnp.maximum(i - WINDOW // TQ, 0)
    init = (jnp.full((rows, 1), NEG_INF, F32), jnp.zeros((rows, 1), F32), jnp.zeros((rows, HD), F32))
    _, l, acc = lax.fori_loop(lo, i + 1, body, init)
    o_win = acc / l
    gates = jax.nn.sigmoid(ga_ref[...])
    for j in range(NSA_GROUP):
        ca = 3 * j
        cb = 3 * (NSA_GROUP + j)

        def gate(c):
            return jnp.where(g == 0, gates[:, ca + c:ca + c + 1], gates[:, cb + c:cb + c + 1])

        sl = slice(j * HD, (j + 1) * HD)
        o_ref[:, sl] = (gate(0) * ocmp_ref[:, sl] + gate(1) * oslc_ref[:, sl]
                        + gate(2) * o_win[j * TQ:(j + 1) * TQ])


def _nsa_win_prompt(q, kvb, h_in, o_cmp, o_slc, slopes, *, TQ):
    t = q.shape[0]
    kb0 = 2 * A_KVW // HD
    hspec = pl.BlockSpec((TQ, NSA_GROUP * HD), lambda g, i: (i, g))
    return pl.pallas_call(
        functools.partial(_nsa_win_kernel, TQ=TQ),
        grid=(NSA_KV, t // TQ),
        in_specs=[
            pl.BlockSpec(memory_space=pltpu.SMEM),
            hspec,
            pl.BlockSpec((t, HD), lambda g, i: (0, kb0 + g)),
            pl.BlockSpec((t, HD), lambda g, i: (0, kb0 + NSA_KV + g)),
            pl.BlockSpec((TQ, LANE), lambda g, i: (i, COL_SM // LANE)),
            hspec,
            hspec,
        ],
        out_specs=hspec,
        out_shape=jax.ShapeDtypeStruct((t, A_Q), F32),
        compiler_params=_cp(("parallel", "parallel")),
        name="nsa_win_prompt",
    )(slopes, q, kvb, kvb, h_in, o_cmp, o_slc)


def _rows8(row, width):
    return jnp.concatenate([row[:, h * width:(h + 1) * width] for h in range(NSA_HEADS)], axis=0)


def _kv_rows8(kn, off):
    return jnp.concatenate(
        [kn[:, off + (h // NSA_GROUP) * HD: off + (h // NSA_GROUP + 1) * HD] for h in range(NSA_HEADS)], axis=0)


def _slope8(slopes_ref):
    hrow = lax.broadcasted_iota(jnp.int32, (NSA_HEADS, 1), 0)
    out = jnp.zeros((NSA_HEADS, 1), F32)
    for h in range(NSA_HEADS):
        out = jnp.where(hrow == h, slopes_ref[h], out)
    return out


def _bf(x):
    return x.astype(BF16).astype(F32)


def _nsa_slc_dec_kernel(pt_ref, slopes_ref, *refs, G, past):
    del pt_ref
    pages = refs[:G]
    q_ref, sel_ref, knew_ref, o_ref, kv_s, m_s, l_s, acc_s = refs[G:]
    pg = pl.program_id(1)
    nsp = sel_ref.shape[-1]
    tk = G * PAGE_ROWS
    q8 = _rows8(q_ref[0], HD) * HD ** -0.5
    q8b = q8.astype(BF16)
    hrow = lax.broadcasted_iota(jnp.int32, (NSA_HEADS, 1), 0)
    first_group = hrow < NSA_GROUP

    @pl.when(pg == 0)
    def _():
        kn = knew_ref[0]
        m_s[...] = jnp.sum(_bf(q8) * _bf(_kv_rows8(kn, 0)), axis=-1, keepdims=True)
        l_s[...] = jnp.ones_like(l_s)
        acc_s[...] = _bf(_kv_rows8(kn, NSA_KV * HD))

    for j in range(G):
        kv_s[j * PAGE_ROWS:(j + 1) * PAGE_ROWS, :] = pages[j][0]
    kpos = pg * tk + lax.broadcasted_iota(jnp.int32, (1, tk), 1)
    dist = (past - kpos).astype(F32)
    srow = lax.broadcasted_iota(jnp.int32, (nsp, tk), 0)
    scol = (pg * tk + lax.broadcasted_iota(jnp.int32, (nsp, tk), 1)) // L_SLC
    expand = jnp.where(srow == scol, 1.0, 0.0).astype(BF16)
    sel2 = sel_ref[0]
    sel8 = jnp.where(first_group, sel2[0:1, :], sel2[1:2, :]).astype(BF16)
    mask = _dot(sel8, expand) > 0.5
    s8 = jnp.where(first_group, _dot_nt(q8b, kv_s[:, 0:HD].astype(BF16)),
                   _dot_nt(q8b, kv_s[:, HD:2 * HD].astype(BF16)))
    lg = jnp.where(mask, s8 - _slope8(slopes_ref) * dist, NEG_INF)
    m = m_s[...]
    m_new = jnp.maximum(m, jnp.max(lg, axis=-1, keepdims=True))
    p = jnp.where(mask, jnp.exp(lg - m_new), 0.0)
    a = jnp.exp(m - m_new)
    pb = p.astype(BF16)
    pv = jnp.where(first_group, _dot(pb, kv_s[:, 2 * HD:3 * HD].astype(BF16)),
                   _dot(pb, kv_s[:, 3 * HD:4 * HD].astype(BF16)))
    m_s[...] = m_new
    l_s[...] = a * l_s[...] + jnp.sum(p, axis=-1, keepdims=True)
    acc_s[...] = a * acc_s[...] + pv

    @pl.when(pg == pl.num_programs(1) - 1)
    def _():
        o = acc_s[...] / l_s[...]
        for h in range(NSA_HEADS):
            o_ref[0, :, h * HD:(h + 1) * HD] = o[h:h + 1, :]


def _nsa_slc_sample(q, sel, knew, pool, page_table, slopes):
    b, n_pages = page_table.shape
    G = _tile(n_pages, (8, 4, 2, 1))
    nsp = sel.shape[-1]
    past = n_pages * PAGE_ROWS

    def page_map(j):
        return lambda bi, p, pt: (pt[bi, p * G + j], 0, 0)

    in_specs = [pl.BlockSpec(memory_space=pltpu.SMEM)]
    in_specs += [pl.BlockSpec((1, PAGE_ROWS, A_KVW), page_map(j)) for j in range(G)]
    in_specs += [
        pl.BlockSpec((1, 1, A_Q), lambda bi, p, pt: (bi, 0, 0)),
        pl.BlockSpec((1, NSA_KV, nsp), lambda bi, p, pt: (bi, 0, 0)),
        pl.BlockSpec((1, 1, A_KVW), lambda bi, p, pt: (bi, 0, 0)),
    ]
    grid_spec = pltpu.PrefetchScalarGridSpec(
        num_scalar_prefetch=1,
        grid=(b, n_pages // G),
        in_specs=in_specs,
        out_specs=pl.BlockSpec((1, 1, A_Q), lambda bi, p, pt: (bi, 0, 0)),
        scratch_shapes=[pltpu.VMEM((G * PAGE_ROWS, A_KVW), F32), pltpu.VMEM((NSA_HEADS, 1), F32),
                        pltpu.VMEM((NSA_HEADS, 1), F32), pltpu.VMEM((NSA_HEADS, HD), F32)],
    )
    return pl.pallas_call(
        functools.partial(_nsa_slc_dec_kernel, G=G, past=past),
        grid_spec=grid_spec,
        out_shape=jax.ShapeDtypeStruct((b, 1, A_Q), F32),
        compiler_params=_cp(("parallel", "arbitrary")),
        name="nsa_slc_sample",
    )(page_table, slopes, *([pool] * G), q, sel, knew)


def _nsa_win_dec_kernel(slopes_ref, q_ref, wb_ref, knew_ref, ga_ref, ocmp_ref, oslc_ref, o_ref, *, past):
    nb = wb_ref.shape[1]
    q8 = _rows8(q_ref[0], HD) * HD ** -0.5
    q8b = q8.astype(BF16)
    hrow = lax.broadcasted_iota(jnp.int32, (NSA_HEADS, 1), 0)
    first_group = hrow < NSA_GROUP
    kn = knew_ref[0]
    wb = wb_ref[0]
    kwpos = past - nb + lax.broadcasted_iota(jnp.int32, (1, nb), 1)
    wd = past - kwpos
    mask = jnp.logical_and(jnp.logical_and(wd >= 0, wd <= WINDOW), kwpos >= 0)
    s8 = jnp.where(first_group, _dot_nt(q8b, wb[:, 0:HD].astype(BF16)),
                   _dot_nt(q8b, wb[:, HD:2 * HD].astype(BF16)))
    lg = jnp.where(mask, s8 - _slope8(slopes_ref) * wd.astype(F32), NEG_INF)
    s_self = jnp.sum(_bf(q8) * _bf(_kv_rows8(kn, 0)), axis=-1, keepdims=True)
    m = jnp.maximum(jnp.max(lg, axis=-1, keepdims=True), s_self)
    p = jnp.where(mask, jnp.exp(lg - m), 0.0)
    p_self = jnp.exp(s_self - m)
    pb = p.astype(BF16)
    pv = jnp.where(first_group, _dot(pb, wb[:, 2 * HD:3 * HD].astype(BF16)),
                   _dot(pb, wb[:, 3 * HD:4 * HD].astype(BF16)))
    pv = pv + _bf(p_self) * _bf(_kv_rows8(kn, NSA_KV * HD))
    o_win = pv / (jnp.sum(p, axis=-1, keepdims=True) + p_self)
    gates = jax.nn.sigmoid(ga_ref[0])
    ocmp = ocmp_ref[0]
    oslc = oslc_ref[0]
    for h in range(NSA_HEADS):
        sl = slice(h * HD, (h + 1) * HD)
        o_ref[0, :, sl] = (gates[:, 3 * h:3 * h + 1] * ocmp[:, sl] + gates[:, 3 * h + 1:3 * h + 2] * oslc[:, sl]
                           + gates[:, 3 * h + 2:3 * h + 3] * o_win[h:h + 1, :])


def _nsa_win_sample(q, win_buf, knew, ga, o_cmp, o_slc, slopes, *, past):
    b = q.shape[0]
    nb = win_buf.shape[1]
    row = lambda w: pl.BlockSpec((1, 1, w), lambda bi: (bi, 0, 0))
    return pl.pallas_call(
        functools.partial(_nsa_win_dec_kernel, past=past),
        grid=(b,),
        in_specs=[pl.BlockSpec(memory_space=pltpu.SMEM), row(A_Q),
                  pl.BlockSpec((1, nb, A_KVW), lambda bi: (bi, 0, 0)), row(A_KVW), row(LANE), row(A_Q), row(A_Q)],
        out_specs=row(A_Q),
        out_shape=jax.ShapeDtypeStruct((b, 1, A_Q), F32),
        compiler_params=_cp(("parallel",)),
        name="nsa_win_sample",
    )(slopes, q, win_buf, knew, ga, o_cmp, o_slc)


def _dn_conv_kernel(x_ref, prev_ref, buf_ref, w_ref, o_ref, hist_ref):
    c = pl.program_id(0)
    t = pl.program_id(1)
    tt = x_ref.shape[0]
    hist_ref[0:8, :] = jnp.where(t == 0, buf_ref[...], prev_ref[...])
    hist_ref[8:8 + tt, :] = x_ref[...]
    w = w_ref[...]
    y = jnp.zeros((tt, LANE), F32)
    for i in range(CONV_W):
        y = y + w[i:i + 1, :] * hist_ref[8 - (CONV_W - 1) + i: 8 - (CONV_W - 1) + i + tt, :]
    y = _silu(y)
    nrm = y * lax.rsqrt(jnp.sum(y * y, axis=-1, keepdims=True) + 1e-6)
    o_ref[...] = jnp.where(c < DN_HEADS, nrm * DK ** -0.5, jnp.where(c < 2 * DN_HEADS, nrm, y))


def _dn_conv_prompt(h_in, buf8, conv_wt):
    t = h_in.shape[0]
    tt = _tile(t, (1024, 512, 256, 128, 64, 32, 16, 8))
    c0 = COL_DQKV // LANE
    return pl.pallas_call(
        _dn_conv_kernel,
        grid=(DN_QKV // LANE, t // tt),
        in_specs=[
            pl.BlockSpec((tt, LANE), lambda c, i: (i, c0 + c)),
            pl.BlockSpec((8, LANE), lambda c, i: (jnp.maximum(i * (tt // 8) - 1, 0), c0 + c)),
            pl.BlockSpec((8, LANE), lambda c, i: (0, c)),
            pl.BlockSpec((CONV_W, LANE), lambda c, i: (0, c)),
        ],
        out_specs=pl.BlockSpec((tt, LANE), lambda c, i: (i, c)),
        out_shape=jax.ShapeDtypeStruct((t, DN_QKV), F32),
        scratch_shapes=[pltpu.VMEM((tt + 8, LANE), F32)],
        compiler_params=_cp(("parallel", "parallel")),
        name="dn_conv",
    )(h_in, h_in, buf8, conv_wt)


def _split(x):
    hi = x.astype(BF16)
    return hi, (x - hi.astype(F32)).astype(BF16)


def _mm3(dotfn, a, b):
    return dotfn(a[0], b[0]) + (dotfn(a[0], b[1]) + dotfn(a[1], b[0]))


def _dot3(x, y):
    return _mm3(_dot, _split(x), _split(y))


def _dot_hi(x, y):
    return _dot(x, y, HI)


def _unit_lower_inverse(a, n, mm=_dot_hi):
    ri = lax.broadcasted_iota(jnp.int32, (n, n), 0)
    ci = lax.broadcasted_iota(jnp.int32, (n, n), 1)
    eye = jnp.where(ri == ci, 1.0, 0.0).astype(F32)
    base = min(16, n)

    def same_block(b):
        return (ri // b) == (ci // b)

    p = jnp.where(same_block(base), -a, 0.0)
    r = eye + p
    steps = int(np.log2(base)) - 1
    for _ in range(steps):
        p = mm(p, p)
        r = r + mm(r, p)
    b = base
    while b < n:
        off = jnp.where(jnp.logical_and(same_block(2 * b), jnp.logical_not(same_block(b))), a, 0.0)
        r = r - mm(r, mm(off, r))
        b *= 2
    return r


def _unit_lower_inverse_many(mats, n, mm):
    ri = lax.broadcasted_iota(jnp.int32, (n, n), 0)
    ci = lax.broadcasted_iota(jnp.int32, (n, n), 1)
    eye = jnp.where(ri == ci, 1.0, 0.0).astype(F32)
    base = min(16, n)

    def same_block(b):
        return (ri // b) == (ci // b)

    ps = [jnp.where(same_block(base), -a, 0.0) for a in mats]
    rs = [eye + p for p in ps]
    for _ in range(int(np.log2(base)) - 1):
        ps = [mm(p, p) for p in ps]
        rs = [r + mm(r, p) for r, p in zip(rs, ps)]
    b = base
    while b < n:
        offm = jnp.logical_and(same_block(2 * b), jnp.logical_not(same_block(b)))
        ts = [mm(jnp.where(offm, a, 0.0), r) for a, r in zip(mats, rs)]
        rs = [r - mm(r, t) for r, t in zip(rs, ts)]
        b *= 2
    return rs


def _dn_chunk_kernel(q_ref, k_ref, v_ref, sm_ref, smt_ref, z_ref, prow_ref, pcol_ref, nw_ref,
                     o_ref, s_out_ref, s_ref):
    c = pl.program_id(0)
    C = q_ref.shape[0]

    @pl.when(c == 0)
    def _():
        s_ref[...] = jnp.zeros_like(s_ref)

    ri = lax.broadcasted_iota(jnp.int32, (C, C), 0)
    ci = lax.broadcasted_iota(jnp.int32, (C, C), 1)
    lower = ri >= ci
    tril = jnp.where(lower, 1.0, 0.0).astype(F32)
    triu = jnp.where(ri <= ci, 1.0, 0.0).astype(F32)
    ri2 = lax.broadcasted_iota(jnp.int32, (2 * C, C), 0)
    ci2 = lax.broadcasted_iota(jnp.int32, (2 * C, C), 1)
    mask2 = jnp.where(ri2 < C, ri2, ri2 - (C - 1)) > ci2
    sm = sm_ref[...]
    smt = smt_ref[...]
    g_cols = -jnp.exp(prow_ref[0:1, :]) * _softplus(sm + prow_ref[1:2, :])
    g_rows = -jnp.exp(pcol_ref[:, 0:1]) * _softplus(smt + pcol_ref[:, 1:2])
    gcum_cols = _dot(tril, g_cols, HI)
    gcum_rows = _dot(g_rows, triu, HI)
    beta_cols = jax.nn.sigmoid(sm)
    nw = nw_ref[...]
    heads = []
    for h in range(DN_HEADS):
        sl = slice(h * DK, (h + 1) * DK)
        q = q_ref[:, sl]
        k = k_ref[:, sl]
        gc = gcum_cols[:, SM_A + h:SM_A + h + 1]
        gr = gcum_rows[SM_A + h:SM_A + h + 1, :]
        beta = beta_cols[:, SM_BETA + h:SM_BETA + h + 1]
        decay = jnp.where(lower, jnp.exp(jnp.where(lower, gc - gr, 0.0)), 0.0)
        kb = k * beta
        egc = jnp.exp(gc)
        gl = gc[C - 1:C, :]
        heads.append(dict(
            h=h, sl=sl, ks=_split(k), decay2=jnp.concatenate([decay, decay], axis=0),
            kbq=_split(jnp.concatenate([kb, q], axis=0)),
            rhs=_split(jnp.concatenate([v_ref[:, sl] * beta, kb * egc], axis=1)),
            qg=q * egc, kdec=_split(k * jnp.exp(gl - gc)), egl=jnp.exp(gl)))
    for d in heads:
        d["aa"] = jnp.where(mask2, _mm3(_dot_nt, d["kbq"], d["ks"]) * d["decay2"], 0.0)
    tinv = _unit_lower_inverse_many([d["aa"][:C] for d in heads], C, _dot3)
    for d, ti in zip(heads, tinv):
        d["sol"] = _mm3(_dot, _split(ti), d["rhs"])
    for d in heads:
        d["s"] = s_ref[d["h"]]
        d["ks2"] = _dot3(jnp.concatenate([d["sol"][:, DV:], d["qg"]], axis=0), d["s"])
    for d in heads:
        d["v_new"] = _split(d["sol"][:, :DV] - d["ks2"][:C])
    for d in heads:
        s_ref[d["h"]] = d["s"] * d["egl"] + _mm3(_dot_tn, d["kdec"], d["v_new"])
    for d in heads:
        o = d["ks2"][C:] + _mm3(_dot, _split(d["aa"][C:]), d["v_new"])
        o = o * lax.rsqrt(jnp.mean(o * o, axis=-1, keepdims=True) + RMS_EPS) * nw
        o_ref[:, d["sl"]] = o * _silu(z_ref[:, d["sl"]])

    @pl.when(c == pl.num_programs(0) - 1)
    def _():
        s_out_ref[...] = s_ref[...]


def _dn_chunk_prompt(qkvn, h_in, smt, prow, pcol, norm_w, *, C):
    t = qkvn.shape[0]
    hw = DN_HEADS * DK
    return pl.pallas_call(
        _dn_chunk_kernel,
        grid=(t // C,),
        in_specs=[
            pl.BlockSpec((C, hw), lambda c: (c, 0)),
            pl.BlockSpec((C, hw), lambda c: (c, 1)),
            pl.BlockSpec((C, hw), lambda c: (c, 2)),
            pl.BlockSpec((C, LANE), lambda c: (c, COL_SM // LANE)),
            pl.BlockSpec((LANE, C), lambda c: (0, c)),
            pl.BlockSpec((C, hw), lambda c: (c, COL_Z // hw)),
            pl.BlockSpec((2, LANE), lambda c: (0, 0)),
            pl.BlockSpec((LANE, 2), lambda c: (0, 0)),
            pl.BlockSpec((1, DV), lambda c: (0, 0)),
        ],
        out_specs=[pl.BlockSpec((C, hw), lambda c: (c, 0)),
                   pl.BlockSpec((DN_HEADS, DK, DV), lambda c: (0, 0, 0))],
        out_shape=[jax.ShapeDtypeStruct((t, hw), F32), jax.ShapeDtypeStruct((DN_HEADS, DK, DV), F32)],
        scratch_shapes=[pltpu.VMEM((DN_HEADS, DK, DV), F32)],
        compiler_params=_cp(("arbitrary",)),
        name="dn_chunk",
    )(qkvn, qkvn, qkvn, h_in, smt, h_in, prow, pcol, norm_w)


def _row_to_col(row, n):
    ri = lax.broadcasted_iota(jnp.int32, (n, n), 0)
    ci = lax.broadcasted_iota(jnp.int32, (n, n), 1)
    return jnp.sum(jnp.where(ri == ci, jnp.broadcast_to(row, (n, n)), 0.0), axis=1, keepdims=True)


def _col_to_row(col, n):
    ri = lax.broadcasted_iota(jnp.int32, (n, n), 0)
    ci = lax.broadcasted_iota(jnp.int32, (n, n), 1)
    return jnp.sum(jnp.where(ri == ci, jnp.broadcast_to(col, (n, n)), 0.0), axis=0, keepdims=True)


def _dn_dec_kernel(buf_ref, xq_ref, xk_ref, xv_ref, w_ref, sm_ref, z_ref, prow_ref, nw_ref, s0_ref,
                   o_ref, s_out_ref):
    hw = DN_HEADS * DK
    buf = buf_ref[0]
    w = w_ref[...]
    sm = sm_ref[0]
    g_row = -jnp.exp(prow_ref[0:1, :]) * _softplus(sm + prow_ref[1:2, :])
    beta_row = jax.nn.sigmoid(sm)
    nw = nw_ref[...]
    z = z_ref[0]
    parts = []
    for part, x_ref in enumerate((xq_ref, xk_ref, xv_ref)):
        sl = slice(part * hw, (part + 1) * hw)
        y = w[CONV_W - 1:CONV_W, sl] * x_ref[0]
        for i in range(CONV_W - 1):
            y = y + w[i:i + 1, sl] * buf[i:i + 1, sl]
        parts.append(_silu(y))
    for h in range(DN_HEADS):
        sl = slice(h * DK, (h + 1) * DK)
        q = parts[0][:, sl]
        k = parts[1][:, sl]
        v = parts[2][:, sl]
        q = q * lax.rsqrt(jnp.sum(q * q, axis=-1, keepdims=True) + 1e-6) * DK ** -0.5
        k = k * lax.rsqrt(jnp.sum(k * k, axis=-1, keepdims=True) + 1e-6)
        a = jnp.exp(g_row[:, SM_A + h:SM_A + h + 1])
        beta = beta_row[:, SM_BETA + h:SM_BETA + h + 1]
        k_col = _row_to_col(k, DK)
        q_col = _row_to_col(q, DK)
        s0 = s0_ref[0, h]
        u = beta * (v - a * jnp.sum(s0 * k_col, axis=0, keepdims=True))
        s_new = a * s0 + k_col * u
        s_out_ref[0, h] = s_new
        o = jnp.sum(s_new * q_col, axis=0, keepdims=True)
        o = o * lax.rsqrt(jnp.mean(o * o, axis=-1, keepdims=True) + RMS_EPS) * nw
        o_ref[0, :, sl] = o * _silu(z[:, sl])


def _dn_sample(conv_buf, h3, conv_wt, prow, norm_w, s0):
    b = h3.shape[0]
    hw = DN_HEADS * DK
    c0 = COL_DQKV // hw
    row = lambda w, j: pl.BlockSpec((1, 1, w), lambda bi: (bi, 0, j))
    return pl.pallas_call(
        _dn_dec_kernel,
        grid=(b,),
        in_specs=[
            pl.BlockSpec((1, CONV_W - 1, DN_QKV), lambda bi: (bi, 0, 0)),
            row(hw, c0), row(hw, c0 + 1), row(hw, c0 + 2),
            pl.BlockSpec((CONV_W, DN_QKV), lambda bi: (0, 0)),
            row(LANE, COL_SM // LANE),
            row(hw, COL_Z // hw),
            pl.BlockSpec((2, LANE), lambda bi: (0, 0)),
            pl.BlockSpec((1, DV), lambda bi: (0, 0)),
            pl.BlockSpec((1, DN_HEADS, DK, DV), lambda bi: (bi, 0, 0, 0)),
        ],
        out_specs=[row(hw, 0), pl.BlockSpec((1, DN_HEADS, DK, DV), lambda bi: (bi, 0, 0, 0))],
        out_shape=[jax.ShapeDtypeStruct((b, 1, hw), F32), jax.ShapeDtypeStruct(s0.shape, F32)],
        compiler_params=_cp(("parallel",)),
        name="dn_sample",
    )(conv_buf, h3, h3, h3, conv_wt, h3, h3, prow, norm_w, s0)


def _rwkv_prep(r, k, wl, al, w0, a0, k_k, k_a):
    w_log = -_softplus(-(w0 + wl)) - 0.5
    log_decay = -jnp.exp(w_log)
    a = jax.nn.sigmoid(a0 + al)
    kk_raw = k * k_k
    k_h = k * (1.0 + (a - 1.0) * k_a)
    del r
    return log_decay, a, kk_raw, k_h


def _rwkv_chunk_kernel(r_ref, k_ref, v_ref, wl_ref, al_ref, gate_ref, prm_ref, o_ref, s_out_ref, s_ref):
    c = pl.program_id(1)
    PP = r_ref.shape[0]
    C = r_ref.shape[1]
    N = RWKV_HS
    hpp = LANE // N

    @pl.when(c == 0)
    def _():
        s_ref[...] = jnp.zeros_like(s_ref)

    ri = lax.broadcasted_iota(jnp.int32, (2 * C, C), 0)
    ci = lax.broadcasted_iota(jnp.int32, (2 * C, C), 1)
    mask2 = jnp.where(ri < C, ri, ri - (C - 1)) > ci
    trilb = jnp.where(lax.broadcasted_iota(jnp.int32, (C, C), 0) >= lax.broadcasted_iota(jnp.int32, (C, C), 1),
                      1.0, 0.0).astype(BF16)
    heads = []
    for pp in range(PP):
        prm = prm_ref[pp]
        r2 = r_ref[pp]
        v2 = v_ref[pp]
        log_decay2, a2, kk_raw2, kh2 = _rwkv_prep(r2, k_ref[pp], wl_ref[pp], al_ref[pp],
                                                  prm[0:1], prm[1:2], prm[2:3], prm[3:4])
        ld_h = log_decay2.astype(BF16)
        rem = log_decay2 - ld_h.astype(F32)
        ld_m = rem.astype(BF16)
        ld_l = (rem - ld_m.astype(F32)).astype(BF16)
        gcum2 = _dot(trilb, ld_h) + (_dot(trilb, ld_m) + _dot(trilb, ld_l))
        for hh in range(hpp):
            sl = slice(hh * N, (hh + 1) * N)
            r = r2[:, sl]
            v = v2[:, sl]
            k_h = kh2[:, sl]
            kk = kk_raw2[:, sl]
            kk = kk / jnp.maximum(jnp.sqrt(jnp.sum(kk * kk, axis=-1, keepdims=True)), 1e-12)
            gc = gcum2[:, sl]
            p_incl = jnp.exp(gc)
            p_inv = jnp.exp(-gc)
            at = -kk * jnp.exp(gc - log_decay2[:, sl])
            bt = kk * a2[:, sl] * p_inv
            kt = k_h * p_inv
            rt = r * p_incl
            heads.append(dict(
                idx=pp * hpp + hh, v=v, p_last=p_incl[C - 1:C, :],
                lhs=_split(jnp.concatenate([at, rt], axis=0)),
                bk=_split(jnp.concatenate([bt, kt], axis=0)),
                bonus=jnp.sum(r * k_h * prm[4:5, sl], axis=-1, keepdims=True) * v,
                ln_w=prm[5:6, sl], ln_b=prm[6:7, sl]))
    for h in heads:
        h["g_b"] = jnp.where(mask2, _mm3(_dot_nt, h["lhs"], (h["bk"][0][:C], h["bk"][1][:C])), 0.0)
    tinv = _unit_lower_inverse_many([-h["g_b"][:C] for h in heads], C, _dot3)
    for h in heads:
        h["g_k"] = jnp.where(mask2, _mm3(_dot_nt, h["lhs"], (h["bk"][0][C:], h["bk"][1][C:])), 0.0)
    for h in heads:
        h["w_kv"] = _dot3(h["g_k"], h["v"])
    for h in heads:
        h["s0"] = s_ref[h["idx"]]
        h["g_s"] = _mm3(_dot_nt, h["lhs"], _split(h["s0"]))
    for h, ti in zip(heads, tinv):
        h["u"] = _dot3(ti, h["g_s"][:C] + h["w_kv"][:C])
    for h in heads:
        uv = _split(jnp.concatenate([h["u"], h["v"]], axis=0))
        s_ref[h["idx"]] = (h["s0"] + _mm3(_dot_tn, uv, h["bk"])) * h["p_last"]
    outs = []
    for h in heads:
        out = h["g_s"][C:] + h["w_kv"][C:] + _dot3(h["g_b"][C:], h["u"])
        mu = jnp.mean(out, axis=-1, keepdims=True)
        d = out - mu
        var = jnp.mean(d * d, axis=-1, keepdims=True)
        outs.append(d * lax.rsqrt(var + GN_EPS) * h["ln_w"] + h["ln_b"] + h["bonus"])
    for pp in range(PP):
        o_ref[pp] = jnp.concatenate(outs[pp * hpp:(pp + 1) * hpp], axis=1) * gate_ref[pp]

    @pl.when(c == pl.num_programs(1) - 1)
    def _():
        s_out_ref[...] = s_ref[...]


def _rwkv_chunk_prompt(r, k, v, wl, al, gate, prm, *, C, PP):
    npair, t, _ = r.shape
    hpp = LANE // RWKV_HS
    seq = pl.BlockSpec((PP, C, LANE), lambda p, c: (p, c, 0))
    return pl.pallas_call(
        _rwkv_chunk_kernel,
        grid=(npair // PP, t // C),
        in_specs=[seq] * 6 + [pl.BlockSpec((PP, 8, LANE), lambda p, c: (p, 0, 0))],
        out_specs=[seq, pl.BlockSpec((PP * hpp, RWKV_HS, RWKV_HS), lambda p, c: (p, 0, 0))],
        out_shape=[jax.ShapeDtypeStruct((npair, t, LANE), F32),
                   jax.ShapeDtypeStruct((npair * hpp, RWKV_HS, RWKV_HS), F32)],
        scratch_shapes=[pltpu.VMEM((PP * hpp, RWKV_HS, RWKV_HS), F32)],
        compiler_params=_cp(("parallel", "arbitrary")),
        name="rwkv_chunk",
    )(r, k, v, wl, al, gate, prm)


def _rwkv_dec_kernel(r_ref, k_ref, v_ref, wl_ref, al_ref, gate_ref, prm_ref, s0_ref, o_ref, s_out_ref):
    N = RWKV_HS
    prm = prm_ref[...]
    r2 = r_ref[0]
    k2 = k_ref[0]
    v2 = v_ref[0]
    log_decay2, a2, kk_raw2, kh2 = _rwkv_prep(r2, k2, wl_ref[0], al_ref[0], prm[0:1], prm[1:2], prm[2:3], prm[3:4])
    w2 = jnp.exp(log_decay2)
    gate = gate_ref[0]
    nh = r2.shape[1] // N
    for h in range(nh):
        sl = slice(h * N, (h + 1) * N)
        r = r2[:, sl]
        v = v2[:, sl]
        k_h = kh2[:, sl]
        kk = kk_raw2[:, sl]
        kk = kk / jnp.maximum(jnp.sqrt(jnp.sum(kk * kk, axis=-1, keepdims=True)), 1e-12)
        s0 = s0_ref[0, h]
        sa = jnp.sum(s0 * (-kk), axis=1, keepdims=True)
        s_new = s0 * w2[:, sl] + sa * (kk * a2[:, sl]) + _row_to_col(v, N) * k_h
        s_out_ref[0, h] = s_new
        out = _col_to_row(jnp.sum(s_new * r, axis=1, keepdims=True), N)
        mu = jnp.mean(out, axis=-1, keepdims=True)
        d = out - mu
        var = jnp.mean(d * d, axis=-1, keepdims=True)
        gn = d * lax.rsqrt(var + GN_EPS) * prm[5:6, sl] + prm[6:7, sl]
        bonus = jnp.sum(r * k_h * prm[4:5, sl], axis=-1, keepdims=True) * v
        o_ref[0, :, sl] = (gn + bonus) * gate[:, sl]


def _rwkv_sample(r, k, v, wl, al, gate, prm, s0):
    b, _, d = r.shape
    row = pl.BlockSpec((1, 1, d), lambda bi: (bi, 0, 0))
    st = pl.BlockSpec((1,) + s0.shape[1:], lambda bi: (bi, 0, 0, 0))
    return pl.pallas_call(
        _rwkv_dec_kernel,
        grid=(b,),
        in_specs=[row] * 6 + [pl.BlockSpec((8, d), lambda bi: (0, 0)), st],
        out_specs=[row, st],
        out_shape=[jax.ShapeDtypeStruct((b, 1, d), F32), jax.ShapeDtypeStruct(s0.shape, F32)],
        compiler_params=_cp(("parallel",)),
        name="rwkv_sample",
    )(r, k, v, wl, al, gate, prm, s0)


def _alibi_slopes():
    return jnp.asarray(2.0 ** (-8.0 * np.arange(1, NSA_HEADS + 1) / NSA_HEADS), dtype=F32)


def _overlap_matrix(nch, nsp):
    cstart = np.arange(nch)[:, None] * D_CMP
    sstart = np.arange(nsp)[None, :] * L_SLC
    return jnp.asarray(((cstart < sstart + L_SLC) & (cstart + L_CMP > sstart)).astype(np.float32))


def _pack_w_in(w):
    offs = np.concatenate([[0], np.cumsum(EVEN_SPLIT)])
    qa, kvc, kvs, kvw, ga, qkv, z, b, a = [w[:, offs[i]:offs[i + 1]] for i in range(len(EVEN_SPLIT))]
    used = COL_SM + A_GATES + 2 * DN_HEADS
    pad = jnp.zeros((w.shape[0], E_IN_PAD - used), w.dtype)
    return jnp.concatenate([qa, qkv, z, kvc, kvs, kvw, ga, b, a, pad], axis=1).astype(BF16)


def _pack_cmp_w1(w1):
    hid = w1.shape[-1]
    w = w1.reshape(2, 2, D_CMP, HD, hid).transpose(0, 2, 3, 1, 4)
    return w.reshape(2, D_CMP, HD, 2 * hid).astype(BF16)


def _pack_cmp_pe(pe):
    p = pe.reshape(2, 2, D_CMP, HD).transpose(0, 2, 1, 3)
    return jnp.concatenate([p, jnp.zeros((2, D_CMP, 6, HD), pe.dtype)], axis=2)


def _dn_gate_params(a_log, dt_bias):
    row = jnp.zeros((2, LANE), F32).at[0, SM_A:SM_A + DN_HEADS].set(a_log).at[1, SM_A:SM_A + DN_HEADS].set(dt_bias)
    return row, row.T


def _even_layer(xp, xs, w_in, w_out, pe, w1, b1, w2, conv_w, a_log, dt_bias, norm_w,
                cache_cmp, cache_slc, win_buf, conv_buf, dn_s0, page_table, g, b):
    t = xp.shape[0]
    bs = xs.shape[0]
    n_pages = page_table.shape[1]
    past = n_pages * PAGE_ROWS
    slopes = _alibi_slopes()
    w_in_p = _pack_w_in(w_in)
    w4 = _pack_cmp_w1(w1)
    pe8 = _pack_cmp_pe(pe)
    w2b = w2.astype(BF16)
    conv_wt = conv_w.T
    prow, pcol = _dn_gate_params(a_log, dt_bias)
    nw = norm_w.reshape(1, DV)

    hp = _mm(xp, w_in_p)
    hs = _mm(xs, w_in_p)
    kvc_p = hp[:, COL_KVC:COL_KVC + A_KVW]
    kvs_p = hp[:, COL_KVS:COL_KVS + A_KVW]
    kvw_p = hp[:, COL_KVW:COL_KVW + A_KVW]
    kvc_s = hs[:, COL_KVC:COL_KVC + A_KVW]
    kvs_s = hs[:, COL_KVS:COL_KVS + A_KVW]
    kvw_s = hs[:, COL_KVW:COL_KVW + A_KVW]

    TQ = _tile(t, (128, 64, 32, 16, 8))
    nch = t // D_CMP
    ns = t // L_SLC
    nsp = -(-ns // LANE) * LANE
    arange_pt = jnp.arange(t // PAGE_ROWS, dtype=jnp.int32)[None]
    h1 = _cmp_stage1(kvc_p.reshape(t // PAGE_ROWS, CH_PER_PAGE, CH_FEAT), arange_pt, w4)
    cmp_p = _cmp_stage2(h1, pe8, w4, b1, w2b)
    q_p = hp[:, COL_QA:COL_QA + A_Q]
    o_cmp, sel = _nsa_select(q_p[None], cmp_p, _overlap_matrix(nch, nsp), slopes,
                             TQ=TQ, NC=(t - L_CMP) // D_CMP + 1, NS=ns, q_off=0)
    kvb = hp[:, COL_KVC:COL_KVC + 3 * A_KVW].astype(BF16)
    o_slc = _nsa_slc_prompt(q_p, sel[0], kvb, slopes, TQ=TQ, TK=_tile(t, (512, 256, 128, 64)))
    o_a_p = _nsa_win_prompt(q_p, kvb, hp, o_cmp[0], o_slc, slopes, TQ=TQ)

    nch_s = n_pages * CH_PER_PAGE
    nc_s = (past + 1 - L_CMP) // D_CMP + 1
    ns_s = -(-(past + 1) // L_SLC)
    nsp_s = -(-ns_s // LANE) * LANE
    h1s = _cmp_stage1(cache_cmp.reshape(cache_cmp.shape[0], CH_PER_PAGE, CH_FEAT), page_table, w4)
    cmp_s = _cmp_stage2(h1s, pe8, w4, b1, w2b)
    hs3 = hs[:, None, :]
    q_s = hs3[:, :, COL_QA:COL_QA + A_Q]
    o_cmp_s, sel_s = _nsa_select(q_s, cmp_s, _overlap_matrix(nch_s, nsp_s), slopes,
                                 TQ=1, NC=nc_s, NS=ns_s, q_off=past)
    o_slc_s = _nsa_slc_sample(q_s, sel_s[:, :, 0, :], kvs_s[:, None, :],
                              cache_slc.reshape(cache_slc.shape[0], PAGE_ROWS, A_KVW), page_table, slopes)
    wb = win_buf.reshape(bs, win_buf.shape[1], A_KVW)
    o_a_s = _nsa_win_sample(q_s, wb, kvw_s[:, None, :], hs3[:, :, COL_SM:COL_SM + LANE], o_cmp_s, o_slc_s,
                            slopes, past=past)

    qkvn = _dn_conv_prompt(hp, jnp.zeros((8, DN_QKV), F32), conv_wt)
    smt = hp[:, COL_SM:COL_SM + LANE].T
    o_b_p, dn_s_p = _dn_chunk_prompt(qkvn, hp, smt, prow, pcol, nw, C=_tile(t, (128,)))
    o_b_s, dn_s_s = _dn_sample(conv_buf, hs3, conv_wt, prow, nw, dn_s0)

    w_out_b = w_out.astype(BF16)
    yp = _mm_ln(jnp.concatenate([o_a_p, o_b_p], axis=1), w_out_b, xp, g, b)
    ys = _mm_ln(jnp.concatenate([o_a_s[:, 0], o_b_s[:, 0]], axis=1), w_out_b, xs, g, b)

    kv6 = lambda a: a.reshape(a.shape[:-1] + (2, NSA_KV, HD))
    raw_p = hp[:, COL_DQKV:COL_DQKV + DN_QKV]
    raw_s = hs[:, COL_DQKV:COL_DQKV + DN_QKV]
    wlen = min(WINDOW, t)
    outs = dict(
        cmp_p=kv6(kvc_p)[None], cmp_s=kv6(kvc_s)[:, None],
        slc_p=kv6(kvs_p)[None], slc_s=kv6(kvs_s)[:, None],
        win_p=kv6(kvw_p[t - wlen:])[None],
        win_s=jnp.concatenate([win_buf, kv6(kvw_s)[:, None]], axis=1)[:, 1:],
        conv_p=jnp.concatenate([jnp.zeros((CONV_W - 1, DN_QKV), F32), raw_p], axis=0)[t:][None],
        conv_s=jnp.concatenate([conv_buf, raw_s[:, None]], axis=1)[:, 1:],
        dns_p=dn_s_p[None], dns_s=dn_s_s,
    )
    return yp, ys, outs


def _odd_layer(xp, xs, shift_s, s0_s, mix, wr, wk, wv, wo, w0, w1, w2, a0, a1, a2, g1, g2, k_k, k_a, r_k,
               ln_w, ln_b, g, b):
    t, d = xp.shape
    bs = xs.shape[0]
    npair = d // LANE
    xprev_p = jnp.concatenate([jnp.zeros((1, d), F32), xp[:-1]], axis=0)
    xprev_s = shift_s

    def padk(wa, wb_):
        r = wa.shape[1]
        rp = -(-r // LANE) * LANE
        return (jnp.pad(wa, ((0, 0), (0, rp - r))).astype(BF16), jnp.pad(wb_, ((0, rp - r), (0, 0))).astype(BF16))

    wrb, wkb, wvb, wob = (w.astype(BF16) for w in (wr, wk, wv, wo))
    w1b, w2b = padk(w1, w2)
    a1b, a2b = padk(a1, a2)
    g1b, g2b = padk(g1, g2)
    prm = jnp.stack([w0, a0, k_k, k_a, r_k.reshape(d), ln_w, ln_b, jnp.zeros((d,), F32)])

    def proj(x, xprev, pair_out):
        mr = lambda i: mix[i:i + 1]
        r = _mm(x, wrb, xprev=xprev, mixrow=mr(0), pair_out=pair_out)
        wl = _mm(_mm(x, w1b, xprev=xprev, mixrow=mr(1), act="tanh"), w2b, pair_out=pair_out)
        k = _mm(x, wkb, xprev=xprev, mixrow=mr(2), pair_out=pair_out)
        v = _mm(x, wvb, xprev=xprev, mixrow=mr(3), pair_out=pair_out)
        al = _mm(_mm(x, a1b, xprev=xprev, mixrow=mr(4)), a2b, pair_out=pair_out)
        gate = _mm(_mm(x, g1b, xprev=xprev, mixrow=mr(5), act="sigmoid"), g2b, pair_out=pair_out)
        return r, k, v, wl, al, gate

    pp = proj(xp, xprev_p, True)
    prm_pair = prm.reshape(8, npair, LANE).transpose(1, 0, 2)
    y_p, s_p = _rwkv_chunk_prompt(*pp, prm_pair, C=_tile(t, (64, 32, 16, 8)), PP=4)
    yp = _mm_ln(y_p, wob, xp, g, b, pair_in=True)

    ps = [a[:, None, :] for a in proj(xs, xprev_s, False)]
    y_s, s_s = _rwkv_sample(*ps, prm, s0_s)
    ys = _mm_ln(y_s[:, 0], wob, xs, g, b)
    return yp, ys, dict(shift_p=xp[t - 1:t], shift_s=xs, rs_p=s_p[None], rs_s=s_s)


def kernel(x_prompt, x_sample, cache_nsa_cmp, cache_nsa_slc, cache_nsa_win, state_dn_conv, state_dn_S, state_rwkv_shift, state_rwkv_S, page_table, ln_g, ln_b, ffn_wi, ffn_wo, mix_w_in, mix_w_out, nsa_cmp_pe, nsa_cmp_w1, nsa_cmp_b1, nsa_cmp_w2, dn_conv_w, dn_a_log, dn_dt_bias, dn_norm_w, rwkv_mix, rwkv_wr, rwkv_wk, rwkv_wv, rwkv_wo, rwkv_w0, rwkv_w1, rwkv_w2, rwkv_a0, rwkv_a1, rwkv_a2, rwkv_g1, rwkv_g2, rwkv_k_k, rwkv_k_a, rwkv_r_k, rwkv_ln_w, rwkv_ln_b):
    bp, t, d = x_prompt.shape
    assert bp == 1 and x_sample.shape[1] == 1
    depth = ffn_wi.shape[0]
    xp = x_prompt[0]
    xs = x_sample[:, 0]
    even, odd = [], []
    for l in range(depth):
        gl = lambda i: (ln_g[l, i][None], ln_b[l, i][None])
        wi = ffn_wi[l].astype(BF16)
        wo = ffn_wo[l].astype(BF16)
        xp = _ffn_ln(xp, wi[0], wo[0], *gl(0))
        xs = _ffn_ln(xs, wi[0], wo[0], *gl(0))
        if l % 2 == 0:
            e = l // 2
            xp, xs, o = _even_layer(
                xp, xs, mix_w_in[e], mix_w_out[e], nsa_cmp_pe[e], nsa_cmp_w1[e], nsa_cmp_b1[e], nsa_cmp_w2[e],
                dn_conv_w[e], dn_a_log[e], dn_dt_bias[e], dn_norm_w[e], cache_nsa_cmp[e], cache_nsa_slc[e],
                cache_nsa_win[e], state_dn_conv[e], state_dn_S[e], page_table, *gl(1))
            even.append(o)
        else:
            c = l // 2
            xp, xs, o = _odd_layer(
                xp, xs, state_rwkv_shift[c], state_rwkv_S[c], rwkv_mix[c], rwkv_wr[c], rwkv_wk[c], rwkv_wv[c],
                rwkv_wo[c], rwkv_w0[c], rwkv_w1[c], rwkv_w2[c], rwkv_a0[c], rwkv_a1[c], rwkv_a2[c], rwkv_g1[c],
                rwkv_g2[c], rwkv_k_k[c], rwkv_k_a[c], rwkv_r_k[c], rwkv_ln_w[c], rwkv_ln_b[c], *gl(1))
            odd.append(o)
        xp = _ffn_ln(xp, wi[1], wo[1], *gl(2))
        xs = _ffn_ln(xs, wi[1], wo[1], *gl(2))
    st = lambda lst, key: jnp.stack([o[key] for o in lst])
    return (xp[None], xs[:, None],
            st(even, "cmp_p"), st(even, "cmp_s"), st(even, "slc_p"), st(even, "slc_s"),
            st(even, "win_p"), st(even, "win_s"), st(even, "conv_p"), st(even, "conv_s"),
            st(even, "dns_p"), st(even, "dns_s"),
            st(odd, "shift_p"), st(odd, "shift_s"), st(odd, "rs_p"), st(odd, "rs_s"))
```

```python
import functools

import numpy as np
import jax
import jax.numpy as jnp
from jax import lax
from jax.experimental import pallas as pl
from jax.experimental.pallas import tpu as pltpu

F32 = jnp.float32
BF16 = jnp.bfloat16
HI = lax.Precision.HIGHEST

DEPTH = 2
ALPHA = (2 * DEPTH) ** 0.25
LN_EPS = 1e-5
RMS_EPS = 1e-6
NSA_HEADS = 8
NSA_KV = 2
NSA_GROUP = NSA_HEADS // NSA_KV
HD = 128
L_CMP = 32
D_CMP = 16
L_SLC = 64
N_SEL = 16
WINDOW = 512
NEG_INF = -1e30
FORCE_SCORE = 1e6
DN_HEADS = 8
DK = 128
DV = 128
CONV_W = 4
RWKV_HS = 64
GN_EPS = 64e-5

A_Q = NSA_HEADS * HD
A_KVW = 2 * NSA_KV * HD
A_GATES = 3 * NSA_HEADS
DN_QKV = DN_HEADS * (2 * DK + DV)
EVEN_SPLIT = (A_Q, A_KVW, A_KVW, A_KVW, A_GATES, DN_QKV, DN_HEADS * DV, DN_HEADS, DN_HEADS)
COL_QA = 0
COL_DQKV = 1024
COL_Z = 4096
COL_KVC = 5120
COL_KVS = 5632
COL_KVW = 6144
COL_SM = 6656
SM_BETA = A_GATES
SM_A = A_GATES + DN_HEADS
E_IN_PAD = 7168
LANE = 128
PAGE_ROWS = 128

VMEM_LIMIT = 56 * 1024 * 1024


def _cp(sem):
    return pltpu.CompilerParams(dimension_semantics=sem, vmem_limit_bytes=VMEM_LIMIT)


def _tile(n, prefs):
    for t in prefs:
        if n % t == 0:
            return t
    return n


def _dot(a, b, precision=None):
    return jnp.dot(a, b, preferred_element_type=F32, precision=precision)


def _dot_nt(a, b, precision=None):
    return lax.dot_general(a, b, (((1,), (1,)), ((), ())), preferred_element_type=F32, precision=precision)


def _dot_tn(a, b, precision=None):
    return lax.dot_general(a, b, (((0,), (0,)), ((), ())), preferred_element_type=F32, precision=precision)


def _layernorm_rows(y, g, b):
    mu = jnp.mean(y, axis=-1, keepdims=True)
    d = y - mu
    var = jnp.mean(d * d, axis=-1, keepdims=True)
    return d * lax.rsqrt(var + LN_EPS) * g + b


def _softplus(x):
    return jnp.maximum(x, 0.0) + jnp.log1p(jnp.exp(-jnp.abs(x)))


def _silu(x):
    return x * jax.nn.sigmoid(x)


def _ffn_kernel(x_ref, wg_ref, wu_ref, wo_ref, g_ref, b_ref, o_ref, xb_ref, acc_ref, *, nf):
    f = pl.program_id(1)

    @pl.when(f == 0)
    def _():
        xb_ref[...] = x_ref[...].astype(BF16)
        acc_ref[...] = jnp.zeros_like(acc_ref)

    xb = xb_ref[...]
    gate = _dot(xb, wg_ref[...])
    up = _dot(xb, wu_ref[...])
    act = (_silu(gate) * up).astype(BF16)
    acc_ref[...] += _dot(act, wo_ref[...])

    @pl.when(f == nf - 1)
    def _():
        y = ALPHA * x_ref[...] + 0.5 * acc_ref[...]
        o_ref[...] = _layernorm_rows(y, g_ref[...], b_ref[...])


def _ffn_ln(x, wi, wo, g, b):
    m, d = x.shape
    f = wo.shape[0]
    tm = _tile(m, (512, 256, 128, 64, 32, 16, 8))
    tf = _tile(f, (512, 256, 128))
    nf = f // tf
    return pl.pallas_call(
        functools.partial(_ffn_kernel, nf=nf),
        grid=(m // tm, nf),
        in_specs=[
            pl.BlockSpec((tm, d), lambda i, j: (i, 0)),
            pl.BlockSpec((d, tf), lambda i, j: (0, j)),
            pl.BlockSpec((d, tf), lambda i, j: (0, j + nf)),
            pl.BlockSpec((tf, d), lambda i, j: (j, 0)),
            pl.BlockSpec((1, d), lambda i, j: (0, 0)),
            pl.BlockSpec((1, d), lambda i, j: (0, 0)),
        ],
        out_specs=pl.BlockSpec((tm, d), lambda i, j: (i, 0)),
        out_shape=jax.ShapeDtypeStruct((m, d), F32),
        scratch_shapes=[pltpu.VMEM((tm, d), BF16), pltpu.VMEM((tm, d), F32)],
        compiler_params=_cp(("parallel", "arbitrary")),
        name="ffn_ln",
    )(x, wi, wi, wo, g, b)


def _mm_kernel(*refs, mix, act, pair_out):
    if mix:
        x_ref, xp_ref, m_ref, w_ref, o_ref, xb_ref = refs
    else:
        x_ref, w_ref, o_ref, xb_ref = refs

    @pl.when(pl.program_id(1) == 0)
    def _():
        x = x_ref[...]
        if mix:
            x = x + (xp_ref[...] - x) * m_ref[...]
        xb_ref[...] = x.astype(BF16)

    y = _dot(xb_ref[...], w_ref[...])
    if act == "tanh":
        y = jnp.tanh(y)
    elif act == "sigmoid":
        y = jax.nn.sigmoid(y)
    if pair_out:
        for p in range(o_ref.shape[0]):
            o_ref[p] = y[:, p * LANE:(p + 1) * LANE]
    else:
        o_ref[...] = y


def _mm(x, w, *, xprev=None, mixrow=None, act=None, pair_out=False, tn_prefs=(512, 256, 128)):
    m, k = x.shape
    n = w.shape[1]
    tm = _tile(m, (512, 256, 128, 64, 32, 16, 8))
    tn = _tile(n, tn_prefs)
    mix = xprev is not None
    in_specs = [pl.BlockSpec((tm, k), lambda i, j: (i, 0))]
    args = [x]
    if mix:
        in_specs += [pl.BlockSpec((tm, k), lambda i, j: (i, 0)), pl.BlockSpec((1, k), lambda i, j: (0, 0))]
        args += [xprev, mixrow]
    in_specs.append(pl.BlockSpec((k, tn), lambda i, j: (0, j)))
    args.append(w)
    if pair_out:
        npb = tn // LANE
        out_spec = pl.BlockSpec((npb, tm, LANE), lambda i, j: (j, i, 0))
        out_shape = jax.ShapeDtypeStruct((n // LANE, m, LANE), F32)
    else:
        out_spec = pl.BlockSpec((tm, tn), lambda i, j: (i, j))
        out_shape = jax.ShapeDtypeStruct((m, n), F32)
    return pl.pallas_call(
        functools.partial(_mm_kernel, mix=mix, act=act, pair_out=pair_out),
        grid=(m // tm, n // tn),
        in_specs=in_specs,
        out_specs=out_spec,
        out_shape=out_shape,
        scratch_shapes=[pltpu.VMEM((tm, k), BF16)],
        compiler_params=_cp(("parallel", "arbitrary")),
        name="matmul",
    )(*args)


def _mmln_kernel(a_ref, w_ref, x_ref, g_ref, b_ref, o_ref, *, pair_in):
    if pair_in:
        a = jnp.concatenate([a_ref[p].astype(BF16) for p in range(a_ref.shape[0])], axis=1)
    else:
        a = a_ref[...].astype(BF16)
    y = _dot(a, w_ref[...])
    o_ref[...] = _layernorm_rows(ALPHA * x_ref[...] + y, g_ref[...], b_ref[...])


def _mm_ln(a, w, x, g, b, *, pair_in=False):
    m, d = x.shape
    k = w.shape[0]
    tm = _tile(m, (512, 256, 128, 64, 32, 16, 8))
    if pair_in:
        a_spec = pl.BlockSpec((k // LANE, tm, LANE), lambda i: (0, i, 0))
    else:
        a_spec = pl.BlockSpec((tm, k), lambda i: (i, 0))
    return pl.pallas_call(
        functools.partial(_mmln_kernel, pair_in=pair_in),
        grid=(m // tm,),
        in_specs=[
            a_spec,
            pl.BlockSpec((k, d), lambda i: (0, 0)),
            pl.BlockSpec((tm, d), lambda i: (i, 0)),
            pl.BlockSpec((1, d), lambda i: (0, 0)),
            pl.BlockSpec((1, d), lambda i: (0, 0)),
        ],
        out_specs=pl.BlockSpec((tm, d), lambda i: (i, 0)),
        out_shape=jax.ShapeDtypeStruct((m, d), F32),
        compiler_params=_cp(("parallel",)),
        name="matmul_ln",
    )(a, w, x, g, b)


CH_PER_PAGE = PAGE_ROWS // D_CMP
KV_PARTS = 2 * NSA_KV
PAGE_SUBROWS = PAGE_ROWS * KV_PARTS


def _cmp1_kernel(pt_ref, *refs, G):
    del pt_ref
    page_refs = refs[:G]
    w_ref, o_ref, xs_ref = refs[G:]
    for j in range(G):
        for sg in range(KV_PARTS):
            for p in range(D_CMP):
                xs_ref[sg, p, j * CH_PER_PAGE:(j + 1) * CH_PER_PAGE, :] = (
                    page_refs[j][0, pl.ds(p * KV_PARTS + sg, CH_PER_PAGE, stride=D_CMP * KV_PARTS), :])
    rows = G * CH_PER_PAGE
    for sg in range(KV_PARTS):
        s = sg // NSA_KV
        acc = jnp.zeros((rows, 2 * HD), F32)
        for p in range(D_CMP):
            acc = acc + _dot(xs_ref[sg, p].astype(BF16), w_ref[s, p])
        o_ref[0, :, sg * 2 * HD:(sg + 1) * 2 * HD] = acc


def _cmp_stage1(pool, page_table, w4):
    b, n_pages = page_table.shape
    G = _tile(n_pages, (8, 4, 2, 1))

    def page_map(j):
        return lambda bi, p, pt: (pt[bi, p * G + j], 0, 0)

    in_specs = [pl.BlockSpec((1, PAGE_SUBROWS, HD), page_map(j)) for j in range(G)]
    in_specs.append(pl.BlockSpec(w4.shape, lambda bi, p, pt: (0, 0, 0, 0)))
    grid_spec = pltpu.PrefetchScalarGridSpec(
        num_scalar_prefetch=1,
        grid=(b, n_pages // G),
        in_specs=in_specs,
        out_specs=pl.BlockSpec((1, G * CH_PER_PAGE, 8 * HD), lambda bi, p, pt: (bi, p, 0)),
        scratch_shapes=[pltpu.VMEM((KV_PARTS, D_CMP, G * CH_PER_PAGE, HD), F32)],
    )
    return pl.pallas_call(
        functools.partial(_cmp1_kernel, G=G),
        grid_spec=grid_spec,
        out_shape=jax.ShapeDtypeStruct((b, n_pages * CH_PER_PAGE, 8 * HD), F32),
        compiler_params=_cp(("parallel", "arbitrary")),
        name="nsa_cmp_stage1",
    )(page_table, *([pool] * G), w4)


def _gelu_tanh(x):
    return 0.5 * x * (1.0 + jnp.tanh(np.sqrt(2.0 / np.pi).astype(np.float32) * (x + 0.044715 * (x * x * x))))


def _cmp2_kernel(h_ref, pe_ref, w4_ref, b1_ref, w2_ref, o_ref):
    nch = h_ref.shape[1]
    for s in range(2):
        pacc = jnp.zeros((8, 2 * HD), F32)
        for p in range(D_CMP):
            pacc = pacc + _dot(pe_ref[s, p].astype(BF16), w4_ref[s, p])
        const = pacc[0:1, 0:HD] + pacc[1:2, HD:2 * HD] + b1_ref[s:s + 1, :]
        for gi in range(NSA_KV):
            sg = s * NSA_KV + gi
            h0 = h_ref[0, :, sg * 2 * HD:sg * 2 * HD + HD]
            h1 = h_ref[0, :, sg * 2 * HD + HD:(sg + 1) * 2 * HD]
            hid = h0 + pltpu.roll(h1, nch - 1, 0) + const
            o_ref[0, :, sg * HD:(sg + 1) * HD] = _dot(_gelu_tanh(hid).astype(BF16), w2_ref[s])


def _cmp_stage2(h, pe8, w4, b1, w2):
    b, nch, _ = h.shape
    return pl.pallas_call(
        _cmp2_kernel,
        grid=(b,),
        in_specs=[
            pl.BlockSpec((1, nch, 8 * HD), lambda i: (i, 0, 0)),
            pl.BlockSpec(pe8.shape, lambda i: (0, 0, 0, 0)),
            pl.BlockSpec(w4.shape, lambda i: (0, 0, 0, 0)),
            pl.BlockSpec(b1.shape, lambda i: (0, 0)),
            pl.BlockSpec(w2.shape, lambda i: (0, 0, 0)),
        ],
        out_specs=pl.BlockSpec((1, nch, A_KVW), lambda i: (i, 0, 0)),
        out_shape=jax.ShapeDtypeStruct((b, nch, A_KVW), F32),
        compiler_params=_cp(("parallel",)),
        name="nsa_cmp_stage2",
    )(h, pe8, w4, b1, w2)


def _nsa_sel_kernel(slopes_ref, q_ref, cmp_ref, ov_ref, ocmp_ref, sel_ref, *, TQ, NC, NS, q_off, n_pick, RS):
    i = pl.program_id(1)
    bb = q_ref.shape[0]
    nch = cmp_ref.shape[1]
    nsp = ov_ref.shape[1]
    qpos = q_off + i * TQ + lax.broadcasted_iota(jnp.int32, (TQ, 1), 0)
    cidx = lax.broadcasted_iota(jnp.int32, (1, nch), 1)
    cstart = cidx * D_CMP
    cmask = jnp.logical_and(cstart + (L_CMP - 1) <= qpos, cidx < NC)
    mask_add = jnp.where(cmask, 0.0, NEG_INF)
    mask_mul = jnp.where(cmask, 1.0, 0.0)
    cdist = (qpos - cstart).astype(F32) - 0.5 * (L_CMP - 1)
    sid = lax.broadcasted_iota(jnp.int32, (1, nsp), 1)
    cur = lax.shift_right_arithmetic(qpos, int(np.log2(L_SLC)))
    svalid = jnp.logical_and(sid * L_SLC <= qpos, sid < NS)
    forced = jnp.logical_or(sid == 0, jnp.logical_or(sid == cur, sid == cur - 1))
    ov = ov_ref[...]
    scores = []
    for bi in range(bb):
        q = q_ref[bi]
        cm = cmp_ref[bi]
        for g in range(NSA_KV):
            kb = cm[:, g * HD:(g + 1) * HD].astype(BF16)
            vb = cm[:, (NSA_KV + g) * HD:(NSA_KV + g + 1) * HD].astype(BF16)
            pcs = jnp.zeros((TQ, nch), F32)
            for j in range(NSA_GROUP):
                h = NSA_GROUP * g + j
                qj = (q[:, h * HD:(h + 1) * HD] * HD ** -0.5).astype(BF16)
                lg = _dot_nt(qj, kb) - slopes_ref[h] * cdist + mask_add
                e = jnp.exp(lg - jnp.max(lg, axis=-1, keepdims=True))
                p = e * (mask_mul * (1.0 / jnp.sum(e, axis=-1, keepdims=True)))
                pcs = pcs + p
                ocmp_ref[bi, :, h * HD:(h + 1) * HD] = _dot(p.astype(BF16), vb)
            imp = _dot(pcs, ov, precision=HI)
            scores.append(jnp.where(svalid, jnp.where(forced, FORCE_SCORE, imp), NEG_INF))
    sidf = jnp.broadcast_to(sid, (RS, nsp)).astype(F32)
    chains = [sc[r0:r0 + RS] for sc in scores for r0 in range(0, TQ, RS)]
    sels = [jnp.zeros((RS, nsp), F32) for _ in chains]
    for _ in range(n_pick):
        mxs = [jnp.max(sc, axis=-1, keepdims=True) for sc in chains]
        firsts = [jnp.min(jnp.where(sc == mx, sidf, float(nsp)), axis=-1, keepdims=True)
                  for sc, mx in zip(chains, mxs)]
        picks = [sidf == f for f in firsts]
        sels = [jnp.where(jnp.logical_and(pk, mx > 0.5 * NEG_INF), 1.0, sl) for pk, mx, sl in zip(picks, mxs, sels)]
        chains = [jnp.where(pk, -jnp.inf, sc) for pk, sc in zip(picks, chains)]
    nsub = TQ // RS
    for bi in range(bb):
        for g in range(NSA_KV):
            for r in range(nsub):
                sel_ref[bi, g, r * RS:(r + 1) * RS, :] = sels[(bi * NSA_KV + g) * nsub + r]


def _nsa_select(q, cmp, overlap, slopes, *, TQ, NC, NS, q_off, BB):
    b, t, _ = q.shape
    nch = cmp.shape[1]
    nsp = overlap.shape[1]
    kern = functools.partial(_nsa_sel_kernel, TQ=TQ, NC=NC, NS=NS, q_off=q_off, n_pick=min(N_SEL, NS),
                             RS=_tile(TQ, (32, 16, 8)))
    return pl.pallas_call(
        kern,
        grid=(b // BB, t // TQ),
        in_specs=[
            pl.BlockSpec(memory_space=pltpu.SMEM),
            pl.BlockSpec((BB, TQ, A_Q), lambda bi, i: (bi, i, 0)),
            pl.BlockSpec((BB, nch, A_KVW), lambda bi, i: (bi, 0, 0)),
            pl.BlockSpec((nch, nsp), lambda bi, i: (0, 0)),
        ],
        out_specs=[
            pl.BlockSpec((BB, TQ, A_Q), lambda bi, i: (bi, i, 0)),
            pl.BlockSpec((BB, NSA_KV, TQ, nsp), lambda bi, i: (bi, 0, i, 0)),
        ],
        out_shape=[jax.ShapeDtypeStruct((b, t, A_Q), F32), jax.ShapeDtypeStruct((b, NSA_KV, t, nsp), F32)],
        compiler_params=_cp(("parallel", "parallel")),
        name="nsa_cmp_select",
    )(slopes, q, cmp, overlap)


def _stack_heads(q, tq):
    del tq
    return jnp.concatenate([q[:, j * HD:(j + 1) * HD] for j in range(NSA_GROUP)], axis=0)


def _online_step(carry, s, dist, mask_add, slopes, v):
    m, l, acc = carry
    lg = s + jnp.concatenate([mask_add - sl * dist for sl in slopes], axis=0)
    m_new = jnp.maximum(m, jnp.max(lg, axis=-1, keepdims=True))
    p = jnp.exp(lg - m_new)
    a = jnp.exp(m - m_new)
    l = a * l + jnp.sum(p, axis=-1, keepdims=True)
    acc = a * acc + _dot(p.astype(BF16), v)
    return m_new, l, acc


def _nsa_slc_kernel(slopes_ref, q_ref, sel_ref, k_ref, v_ref, o_ref, *, TQ, TK):
    g = pl.program_id(0)
    i = pl.program_id(1)
    nsp = sel_ref.shape[-1]
    rows = NSA_GROUP * TQ
    q4 = (_stack_heads(q_ref[...], TQ) * HD ** -0.5).astype(BF16)
    qposf = (i * TQ + lax.broadcasted_iota(jnp.int32, (TQ, 1), 0)).astype(F32)
    slopes = [slopes_ref[NSA_GROUP * g + j] for j in range(NSA_GROUP)]
    selb = sel_ref[0].astype(BF16)
    blk_per_tile = TK // L_SLC
    delta = (lax.broadcasted_iota(jnp.int32, (nsp, TK), 0)
             - lax.shift_right_arithmetic(lax.broadcasted_iota(jnp.int32, (nsp, TK), 1), int(np.log2(L_SLC))))
    kcol = lax.broadcasted_iota(jnp.int32, (1, TK), 1)

    def body(kt, carry):
        k0 = pl.multiple_of(kt * TK, TK)
        k = k_ref[pl.ds(k0, TK), :]
        v = v_ref[pl.ds(k0, TK), :]
        s = _dot_nt(q4, k)
        dist = qposf - (k0 + kcol).astype(F32)
        expand = jnp.where(delta == kt * blk_per_tile, 1.0, 0.0).astype(BF16)
        se = _dot(selb, expand)
        mask_add = jnp.where(jnp.logical_and(se > 0.5, dist >= 0.0), 0.0, NEG_INF)
        return _online_step(carry, s, dist, mask_add, slopes, v)

    ntile = (i * TQ + TQ + TK - 1) // TK
    init = (jnp.full((rows, 1), NEG_INF, F32), jnp.zeros((rows, 1), F32), jnp.zeros((rows, HD), F32))
    _, l, acc = lax.fori_loop(0, ntile, body, init)
    o = acc / l
    for j in range(NSA_GROUP):
        o_ref[:, j * HD:(j + 1) * HD] = o[j * TQ:(j + 1) * TQ]


def _nsa_slc_prompt(q, sel, kvb, slopes, *, TQ, TK):
    t = q.shape[0]
    nsp = sel.shape[-1]
    kb0 = A_KVW // HD
    return pl.pallas_call(
        functools.partial(_nsa_slc_kernel, TQ=TQ, TK=TK),
        grid=(NSA_KV, t // TQ),
        in_specs=[
            pl.BlockSpec(memory_space=pltpu.SMEM),
            pl.BlockSpec((TQ, NSA_GROUP * HD), lambda g, i: (i, g)),
            pl.BlockSpec((1, TQ, nsp), lambda g, i: (g, i, 0)),
            pl.BlockSpec((t, HD), lambda g, i: (0, kb0 + g)),
            pl.BlockSpec((t, HD), lambda g, i: (0, kb0 + NSA_KV + g)),
        ],
        out_specs=pl.BlockSpec((TQ, NSA_GROUP * HD), lambda g, i: (i, g)),
        out_shape=jax.ShapeDtypeStruct((t, A_Q), F32),
        compiler_params=_cp(("parallel", "parallel")),
        name="nsa_slc_prompt",
    )(slopes, q, sel, kvb, kvb)


def _nsa_win_kernel(slopes_ref, q_ref, k_ref, v_ref, ga_ref, ocmp_ref, oslc_ref, o_ref, *, TQ):
    g = pl.program_id(0)
    i = pl.program_id(1)
    rows = NSA_GROUP * TQ
    q4 = (_stack_heads(q_ref[...], TQ) * HD ** -0.5).astype(BF16)
    qposf = (i * TQ + lax.broadcasted_iota(jnp.int32, (TQ, 1), 0)).astype(F32)
    slopes = [slopes_ref[NSA_GROUP * g + j] for j in range(NSA_GROUP)]
    kcol = lax.broadcasted_iota(jnp.int32, (1, TQ), 1)

    def body(kt, carry):
        k0 = pl.multiple_of(kt * TQ, TQ)
        k = k_ref[pl.ds(k0, TQ), :]
        v = v_ref[pl.ds(k0, TQ), :]
        s = _dot_nt(q4, k)
        dist = qposf - (k0 + kcol).astype(F32)
        mask_add = jnp.where(jnp.logical_and(dist >= 0.0, dist <= float(WINDOW)), 0.0, NEG_INF)
        return _online_step(carry, s, dist, mask_add, slopes, v)

    lo = jnp.maximum(i - WINDOW // TQ, 0)
    init = (jnp.full((rows, 1), NEG_INF, F32), jnp.zeros((rows, 1), F32), jnp.zeros((rows, HD), F32))
    _, l, acc = lax.fori_loop(lo, i + 1, body, init)
    o_win = acc / l
    gates = jax.nn.sigmoid(ga_ref[...])
    for j in range(NSA_GROUP):
        ca = 3 * j
        cb = 3 * (NSA_GROUP + j)

        def gate(c):
            return jnp.where(g == 0, gates[:, ca + c:ca + c + 1], gates[:, cb + c:cb + c + 1])

        sl = slice(j * HD, (j + 1) * HD)
        o_ref[:, sl] = (gate(0) * ocmp_ref[:, sl] + gate(1) * oslc_ref[:, sl]
                        + gate(2) * o_win[j * TQ:(j + 1) * TQ])


def _nsa_win_prompt(q, kvb, h_in, o_cmp, o_slc, slopes, *, TQ):
    t = q.shape[0]
    kb0 = 2 * A_KVW // HD
    hspec = pl.BlockSpec((TQ, NSA_GROUP * HD), lambda g, i: (i, g))
    return pl.pallas_call(
        functools.partial(_nsa_win_kernel, TQ=TQ),
        grid=(NSA_KV, t // TQ),
        in_specs=[
            pl.BlockSpec(memory_space=pltpu.SMEM),
            hspec,
            pl.BlockSpec((t, HD), lambda g, i: (0, kb0 + g)),
            pl.BlockSpec((t, HD), lambda g, i: (0, kb0 + NSA_KV + g)),
            pl.BlockSpec((TQ, LANE), lambda g, i: (i, COL_SM // LANE)),
            hspec,
            hspec,
        ],
        out_specs=hspec,
        out_shape=jax.ShapeDtypeStruct((t, A_Q), F32),
        compiler_params=_cp(("parallel", "parallel")),
        name="nsa_win_prompt",
    )(slopes, q, kvb, kvb, h_in, o_cmp, o_slc)


def _rows8(row, width):
    return jnp.concatenate([row[:, h * width:(h + 1) * width] for h in range(NSA_HEADS)], axis=0)


def _kv_rows8(kn, off):
    return jnp.concatenate(
        [kn[:, off + (h // NSA_GROUP) * HD: off + (h // NSA_GROUP + 1) * HD] for h in range(NSA_HEADS)], axis=0)


def _slope8(slopes_ref):
    hrow = lax.broadcasted_iota(jnp.int32, (NSA_HEADS, 1), 0)
    out = jnp.zeros((NSA_HEADS, 1), F32)
    for h in range(NSA_HEADS):
        out = jnp.where(hrow == h, slopes_ref[h], out)
    return out


def _bf(x):
    return x.astype(BF16).astype(F32)


def _nsa_slc_dec_kernel(pt_ref, slopes_ref, *refs, G, past):
    del pt_ref
    pages = refs[:G]
    q_ref, sel_ref, knew_ref, o_ref, kv_s, m_s, l_s, acc_s = refs[G:]
    pg = pl.program_id(1)
    nsp = sel_ref.shape[-1]
    tk = G * PAGE_ROWS
    q8 = _rows8(q_ref[0], HD) * HD ** -0.5
    q8b = q8.astype(BF16)
    hrow = lax.broadcasted_iota(jnp.int32, (NSA_HEADS, 1), 0)
    first_group = hrow < NSA_GROUP

    @pl.when(pg == 0)
    def _():
        kn = knew_ref[0]
        m_s[...] = jnp.sum(_bf(q8) * _bf(_kv_rows8(kn, 0)), axis=-1, keepdims=True)
        l_s[...] = jnp.ones_like(l_s)
        acc_s[...] = _bf(_kv_rows8(kn, NSA_KV * HD))

    for j in range(G):
        for sg in range(KV_PARTS):
            kv_s[sg, j * PAGE_ROWS:(j + 1) * PAGE_ROWS, :] = pages[j][0, pl.ds(sg, PAGE_ROWS, stride=KV_PARTS), :]
    kpos = pg * tk + lax.broadcasted_iota(jnp.int32, (1, tk), 1)
    dist = (past - kpos).astype(F32)
    srow = lax.broadcasted_iota(jnp.int32, (nsp, tk), 0)
    scol = (pg * tk + lax.broadcasted_iota(jnp.int32, (nsp, tk), 1)) // L_SLC
    expand = jnp.where(srow == scol, 1.0, 0.0).astype(BF16)
    sel2 = sel_ref[0]
    sel8 = jnp.where(first_group, sel2[0:1, :], sel2[1:2, :]).astype(BF16)
    mask = _dot(sel8, expand) > 0.5
    s8 = jnp.where(first_group, _dot_nt(q8b, kv_s[0].astype(BF16)), _dot_nt(q8b, kv_s[1].astype(BF16)))
    lg = jnp.where(mask, s8 - _slope8(slopes_ref) * dist, NEG_INF)
    m = m_s[...]
    m_new = jnp.maximum(m, jnp.max(lg, axis=-1, keepdims=True))
    p = jnp.where(mask, jnp.exp(lg - m_new), 0.0)
    a = jnp.exp(m - m_new)
    pb = p.astype(BF16)
    pv = jnp.where(first_group, _dot(pb, kv_s[2].astype(BF16)), _dot(pb, kv_s[3].astype(BF16)))
    m_s[...] = m_new
    l_s[...] = a * l_s[...] + jnp.sum(p, axis=-1, keepdims=True)
    acc_s[...] = a * acc_s[...] + pv

    @pl.when(pg == pl.num_programs(1) - 1)
    def _():
        o = acc_s[...] / l_s[...]
        for h in range(NSA_HEADS):
            o_ref[0, :, h * HD:(h + 1) * HD] = o[h:h + 1, :]


def _nsa_slc_sample(q, sel, knew, pool, page_table, slopes):
    b, n_pages = page_table.shape
    G = _tile(n_pages, (8, 4, 2, 1))
    nsp = sel.shape[-1]
    past = n_pages * PAGE_ROWS

    def page_map(j):
        return lambda bi, p, pt: (pt[bi, p * G + j], 0, 0)

    in_specs = [pl.BlockSpec(memory_space=pltpu.SMEM)]
    in_specs += [pl.BlockSpec((1, PAGE_SUBROWS, HD), page_map(j)) for j in range(G)]
    in_specs += [
        pl.BlockSpec((1, 1, A_Q), lambda bi, p, pt: (bi, 0, 0)),
        pl.BlockSpec((1, NSA_KV, nsp), lambda bi, p, pt: (bi, 0, 0)),
        pl.BlockSpec((1, 1, A_KVW), lambda bi, p, pt: (bi, 0, 0)),
    ]
    grid_spec = pltpu.PrefetchScalarGridSpec(
        num_scalar_prefetch=1,
        grid=(b, n_pages // G),
        in_specs=in_specs,
        out_specs=pl.BlockSpec((1, 1, A_Q), lambda bi, p, pt: (bi, 0, 0)),
        scratch_shapes=[pltpu.VMEM((KV_PARTS, G * PAGE_ROWS, HD), F32), pltpu.VMEM((NSA_HEADS, 1), F32),
                        pltpu.VMEM((NSA_HEADS, 1), F32), pltpu.VMEM((NSA_HEADS, HD), F32)],
    )
    return pl.pallas_call(
        functools.partial(_nsa_slc_dec_kernel, G=G, past=past),
        grid_spec=grid_spec,
        out_shape=jax.ShapeDtypeStruct((b, 1, A_Q), F32),
        compiler_params=_cp(("parallel", "arbitrary")),
        name="nsa_slc_sample",
    )(page_table, slopes, *([pool] * G), q, sel, knew)


def _nsa_win_dec_kernel(slopes_ref, q_ref, wb_ref, knew_ref, ga_ref, ocmp_ref, oslc_ref, o_ref, *, past):
    nb = wb_ref.shape[1]
    q8 = _rows8(q_ref[0], HD) * HD ** -0.5
    q8b = q8.astype(BF16)
    hrow = lax.broadcasted_iota(jnp.int32, (NSA_HEADS, 1), 0)
    first_group = hrow < NSA_GROUP
    kn = knew_ref[0]
    wb = wb_ref[0]
    kwpos = past - nb + lax.broadcasted_iota(jnp.int32, (1, nb), 1)
    wd = past - kwpos
    mask = jnp.logical_and(jnp.logical_and(wd >= 0, wd <= WINDOW), kwpos >= 0)
    s8 = jnp.where(first_group, _dot_nt(q8b, wb[:, 0:HD].astype(BF16)),
                   _dot_nt(q8b, wb[:, HD:2 * HD].astype(BF16)))
    lg = jnp.where(mask, s8 - _slope8(slopes_ref) * wd.astype(F32), NEG_INF)
    s_self = jnp.sum(_bf(q8) * _bf(_kv_rows8(kn, 0)), axis=-1, keepdims=True)
    m = jnp.maximum(jnp.max(lg, axis=-1, keepdims=True), s_self)
    p = jnp.where(mask, jnp.exp(lg - m), 0.0)
    p_self = jnp.exp(s_self - m)
    pb = p.astype(BF16)
    pv = jnp.where(first_group, _dot(pb, wb[:, 2 * HD:3 * HD].astype(BF16)),
                   _dot(pb, wb[:, 3 * HD:4 * HD].astype(BF16)))
    pv = pv + _bf(p_self) * _bf(_kv_rows8(kn, NSA_KV * HD))
    o_win = pv / (jnp.sum(p, axis=-1, keepdims=True) + p_self)
    gates = jax.nn.sigmoid(ga_ref[0])
    ocmp = ocmp_ref[0]
    oslc = oslc_ref[0]
    for h in range(NSA_HEADS):
        sl = slice(h * HD, (h + 1) * HD)
        o_ref[0, :, sl] = (gates[:, 3 * h:3 * h + 1] * ocmp[:, sl] + gates[:, 3 * h + 1:3 * h + 2] * oslc[:, sl]
                           + gates[:, 3 * h + 2:3 * h + 3] * o_win[h:h + 1, :])


def _nsa_win_sample(q, win_buf, knew, ga, o_cmp, o_slc, slopes, *, past):
    b = q.shape[0]
    nb = win_buf.shape[1]
    row = lambda w: pl.BlockSpec((1, 1, w), lambda bi: (bi, 0, 0))
    return pl.pallas_call(
        functools.partial(_nsa_win_dec_kernel, past=past),
        grid=(b,),
        in_specs=[pl.BlockSpec(memory_space=pltpu.SMEM), row(A_Q),
                  pl.BlockSpec((1, nb, A_KVW), lambda bi: (bi, 0, 0)), row(A_KVW), row(LANE), row(A_Q), row(A_Q)],
        out_specs=row(A_Q),
        out_shape=jax.ShapeDtypeStruct((b, 1, A_Q), F32),
        compiler_params=_cp(("parallel",)),
        name="nsa_win_sample",
    )(slopes, q, win_buf, knew, ga, o_cmp, o_slc)


def _dn_conv_kernel(x_ref, prev_ref, buf_ref, w_ref, o_ref, hist_ref):
    c = pl.program_id(0)
    t = pl.program_id(1)
    tt = x_ref.shape[0]
    hist_ref[0:8, :] = jnp.where(t == 0, buf_ref[...], prev_ref[...])
    hist_ref[8:8 + tt, :] = x_ref[...]
    w = w_ref[...]
    y = jnp.zeros((tt, LANE), F32)
    for i in range(CONV_W):
        y = y + w[i:i + 1, :] * hist_ref[8 - (CONV_W - 1) + i: 8 - (CONV_W - 1) + i + tt, :]
    y = _silu(y)
    nrm = y * lax.rsqrt(jnp.sum(y * y, axis=-1, keepdims=True) + 1e-6)
    o_ref[...] = jnp.where(c < DN_HEADS, nrm * DK ** -0.5, jnp.where(c < 2 * DN_HEADS, nrm, y))


def _dn_conv_prompt(h_in, buf8, conv_wt):
    t = h_in.shape[0]
    tt = _tile(t, (1024, 512, 256, 128, 64, 32, 16, 8))
    c0 = COL_DQKV // LANE
    return pl.pallas_call(
        _dn_conv_kernel,
        grid=(DN_QKV // LANE, t // tt),
        in_specs=[
            pl.BlockSpec((tt, LANE), lambda c, i: (i, c0 + c)),
            pl.BlockSpec((8, LANE), lambda c, i: (jnp.maximum(i * (tt // 8) - 1, 0), c0 + c)),
            pl.BlockSpec((8, LANE), lambda c, i: (0, c)),
            pl.BlockSpec((CONV_W, LANE), lambda c, i: (0, c)),
        ],
        out_specs=pl.BlockSpec((tt, LANE), lambda c, i: (i, c)),
        out_shape=jax.ShapeDtypeStruct((t, DN_QKV), F32),
        scratch_shapes=[pltpu.VMEM((tt + 8, LANE), F32)],
        compiler_params=_cp(("parallel", "parallel")),
        name="dn_conv",
    )(h_in, h_in, buf8, conv_wt)


def _dot1(x, y):
    return _dot(x.astype(BF16), y.astype(BF16))


def _unit_lower_inverse_many(mats, n, mm):
    ri = lax.broadcasted_iota(jnp.int32, (n, n), 0)
    ci = lax.broadcasted_iota(jnp.int32, (n, n), 1)
    eye = jnp.where(ri == ci, 1.0, 0.0).astype(F32)
    base = min(16, n)

    def same_block(b):
        return (ri // b) == (ci // b)

    ps = [jnp.where(same_block(base), -a, 0.0) for a in mats]
    rs = [eye + p for p in ps]
    for _ in range(int(np.log2(base)) - 1):
        ps = [mm(p, p) for p in ps]
        rs = [r + mm(r, p) for r, p in zip(rs, ps)]
    b = base
    while b < n:
        offm = jnp.logical_and(same_block(2 * b), jnp.logical_not(same_block(b)))
        ts = [mm(jnp.where(offm, a, 0.0), r) for a, r in zip(mats, rs)]
        rs = [r - mm(r, t) for r, t in zip(rs, ts)]
        b *= 2
    return rs


def _dn_chunk_kernel(q_ref, k_ref, v_ref, sm_ref, smt_ref, z_ref, prow_ref, pcol_ref, nw_ref,
                     o_ref, s_out_ref, s_ref):
    c = pl.program_id(0)
    C = q_ref.shape[0]

    @pl.when(c == 0)
    def _():
        s_ref[...] = jnp.zeros_like(s_ref)

    ri = lax.broadcasted_iota(jnp.int32, (C, C), 0)
    ci = lax.broadcasted_iota(jnp.int32, (C, C), 1)
    lower = ri >= ci
    tril = jnp.where(lower, 1.0, 0.0).astype(F32)
    triu = jnp.where(ri <= ci, 1.0, 0.0).astype(F32)
    ri2 = lax.broadcasted_iota(jnp.int32, (2 * C, C), 0)
    ci2 = lax.broadcasted_iota(jnp.int32, (2 * C, C), 1)
    mask2 = jnp.where(ri2 < C, ri2, ri2 - (C - 1)) > ci2
    sm = sm_ref[...]
    smt = smt_ref[...]
    g_cols = -jnp.exp(prow_ref[0:1, :]) * _softplus(sm + prow_ref[1:2, :])
    g_rows = -jnp.exp(pcol_ref[:, 0:1]) * _softplus(smt + pcol_ref[:, 1:2])
    gcum_cols = _dot(tril, g_cols, HI)
    gcum_rows = _dot(g_rows, triu, HI)
    beta_cols = jax.nn.sigmoid(sm)
    nw = nw_ref[...]
    heads = []
    for h in range(DN_HEADS):
        sl = slice(h * DK, (h + 1) * DK)
        q = q_ref[:, sl]
        k = k_ref[:, sl]
        gc = gcum_cols[:, SM_A + h:SM_A + h + 1]
        gr = gcum_rows[SM_A + h:SM_A + h + 1, :]
        beta = beta_cols[:, SM_BETA + h:SM_BETA + h + 1]
        decay = jnp.where(lower, jnp.exp(jnp.where(lower, gc - gr, 0.0)), 0.0)
        kb = k * beta
        egc = jnp.exp(gc)
        gl = gc[C - 1:C, :]
        heads.append(dict(
            h=h, sl=sl, kb16=k.astype(BF16), decay2=jnp.concatenate([decay, decay], axis=0),
            kbq=jnp.concatenate([kb, q], axis=0).astype(BF16),
            rhs=jnp.concatenate([v_ref[:, sl] * beta, kb * egc], axis=1).astype(BF16),
            qg=q * egc, kdec=(k * jnp.exp(gl - gc)).astype(BF16), egl=jnp.exp(gl)))
    for d in heads:
        d["aa"] = jnp.where(mask2, _dot_nt(d["kbq"], d["kb16"]) * d["decay2"], 0.0)
    tinv = _unit_lower_inverse_many([d["aa"][:C] for d in heads], C, _dot1)
    for d, ti in zip(heads, tinv):
        d["sol"] = _dot(ti.astype(BF16), d["rhs"])
    for d in heads:
        d["s"] = s_ref[d["h"]]
        d["ks2"] = _dot1(jnp.concatenate([d["sol"][:, DV:], d["qg"]], axis=0), d["s"])
    for d in heads:
        d["v_new"] = (d["sol"][:, :DV] - d["ks2"][:C]).astype(BF16)
    for d in heads:
        s_ref[d["h"]] = d["s"] * d["egl"] + _dot_tn(d["kdec"], d["v_new"])
    for d in heads:
        o = d["ks2"][C:] + _dot(d["aa"][C:].astype(BF16), d["v_new"])
        o = o * lax.rsqrt(jnp.mean(o * o, axis=-1, keepdims=True) + RMS_EPS) * nw
        o_ref[:, d["sl"]] = o * _silu(z_ref[:, d["sl"]])

    @pl.when(c == pl.num_programs(0) - 1)
    def _():
        s_out_ref[...] = s_ref[...]


def _dn_chunk_prompt(qkvn, h_in, smt, prow, pcol, norm_w, *, C):
    t = qkvn.shape[0]
    hw = DN_HEADS * DK
    return pl.pallas_call(
        _dn_chunk_kernel,
        grid=(t // C,),
        in_specs=[
            pl.BlockSpec((C, hw), lambda c: (c, 0)),
            pl.BlockSpec((C, hw), lambda c: (c, 1)),
            pl.BlockSpec((C, hw), lambda c: (c, 2)),
            pl.BlockSpec((C, LANE), lambda c: (c, COL_SM // LANE)),
            pl.BlockSpec((LANE, C), lambda c: (0, c)),
            pl.BlockSpec((C, hw), lambda c: (c, COL_Z // hw)),
            pl.BlockSpec((2, LANE), lambda c: (0, 0)),
            pl.BlockSpec((LANE, 2), lambda c: (0, 0)),
            pl.BlockSpec((1, DV), lambda c: (0, 0)),
        ],
        out_specs=[pl.BlockSpec((C, hw), lambda c: (c, 0)),
                   pl.BlockSpec((DN_HEADS, DK, DV), lambda c: (0, 0, 0))],
        out_shape=[jax.ShapeDtypeStruct((t, hw), F32), jax.ShapeDtypeStruct((DN_HEADS, DK, DV), F32)],
        scratch_shapes=[pltpu.VMEM((DN_HEADS, DK, DV), F32)],
        compiler_params=_cp(("arbitrary",)),
        name="dn_chunk",
    )(qkvn, qkvn, qkvn, h_in, smt, h_in, prow, pcol, norm_w)


def _row_to_col(row, n):
    ri = lax.broadcasted_iota(jnp.int32, (n, n), 0)
    ci = lax.broadcasted_iota(jnp.int32, (n, n), 1)
    return jnp.sum(jnp.where(ri == ci, jnp.broadcast_to(row, (n, n)), 0.0), axis=1, keepdims=True)


def _col_to_row(col, n):
    ri = lax.broadcasted_iota(jnp.int32, (n, n), 0)
    ci = lax.broadcasted_iota(jnp.int32, (n, n), 1)
    return jnp.sum(jnp.where(ri == ci, jnp.broadcast_to(col, (n, n)), 0.0), axis=0, keepdims=True)


def _dn_dec_kernel(buf_ref, xq_ref, xk_ref, xv_ref, w_ref, sm_ref, z_ref, prow_ref, nw_ref, s0_ref,
                   o_ref, s_out_ref):
    hw = DN_HEADS * DK
    buf = buf_ref[0]
    w = w_ref[...]
    sm = sm_ref[0]
    g_row = -jnp.exp(prow_ref[0:1, :]) * _softplus(sm + prow_ref[1:2, :])
    beta_row = jax.nn.sigmoid(sm)
    nw = nw_ref[...]
    z = z_ref[0]
    parts = []
    for part, x_ref in enumerate((xq_ref, xk_ref, xv_ref)):
        sl = slice(part * hw, (part + 1) * hw)
        y = w[CONV_W - 1:CONV_W, sl] * x_ref[0]
        for i in range(CONV_W - 1):
            y = y + w[i:i + 1, sl] * buf[i:i + 1, sl]
        parts.append(_silu(y))
    for h in range(DN_HEADS):
        sl = slice(h * DK, (h + 1) * DK)
        q = parts[0][:, sl]
        k = parts[1][:, sl]
        v = parts[2][:, sl]
        q = q * lax.rsqrt(jnp.sum(q * q, axis=-1, keepdims=True) + 1e-6) * DK ** -0.5
        k = k * lax.rsqrt(jnp.sum(k * k, axis=-1, keepdims=True) + 1e-6)
        a = jnp.exp(g_row[:, SM_A + h:SM_A + h + 1])
        beta = beta_row[:, SM_BETA + h:SM_BETA + h + 1]
        k_col = _row_to_col(k, DK)
        q_col = _row_to_col(q, DK)
        s0 = s0_ref[0, h]
        u = beta * (v - a * jnp.sum(s0 * k_col, axis=0, keepdims=True))
        s_new = a * s0 + k_col * u
        s_out_ref[0, h] = s_new
        o = jnp.sum(s_new * q_col, axis=0, keepdims=True)
        o = o * lax.rsqrt(jnp.mean(o * o, axis=-1, keepdims=True) + RMS_EPS) * nw
        o_ref[0, :, sl] = o * _silu(z[:, sl])


def _dn_sample(conv_buf, h3, conv_wt, prow, norm_w, s0):
    b = h3.shape[0]
    hw = DN_HEADS * DK
    c0 = COL_DQKV // hw
    row = lambda w, j: pl.BlockSpec((1, 1, w), lambda bi: (bi, 0, j))
    return pl.pallas_call(
        _dn_dec_kernel,
        grid=(b,),
        in_specs=[
            pl.BlockSpec((1, CONV_W - 1, DN_QKV), lambda bi: (bi, 0, 0)),
            row(hw, c0), row(hw, c0 + 1), row(hw, c0 + 2),
            pl.BlockSpec((CONV_W, DN_QKV), lambda bi: (0, 0)),
            row(LANE, COL_SM // LANE),
            row(hw, COL_Z // hw),
            pl.BlockSpec((2, LANE), lambda bi: (0, 0)),
            pl.BlockSpec((1, DV), lambda bi: (0, 0)),
            pl.BlockSpec((1, DN_HEADS, DK, DV), lambda bi: (bi, 0, 0, 0)),
        ],
        out_specs=[row(hw, 0), pl.BlockSpec((1, DN_HEADS, DK, DV), lambda bi: (bi, 0, 0, 0))],
        out_shape=[jax.ShapeDtypeStruct((b, 1, hw), F32), jax.ShapeDtypeStruct(s0.shape, F32)],
        compiler_params=_cp(("parallel",)),
        name="dn_sample",
    )(conv_buf, h3, h3, h3, conv_wt, h3, h3, prow, norm_w, s0)


def _rwkv_prep(r, k, wl, al, w0, a0, k_k, k_a):
    w_log = -_softplus(-(w0 + wl)) - 0.5
    log_decay = -jnp.exp(w_log)
    a = jax.nn.sigmoid(a0 + al)
    kk_raw = k * k_k
    k_h = k * (1.0 + (a - 1.0) * k_a)
    del r
    return log_decay, a, kk_raw, k_h


def _rwkv_chunk_kernel(r_ref, k_ref, v_ref, wl_ref, al_ref, gate_ref, prm_ref, o_ref, s_out_ref, s_ref):
    c = pl.program_id(1)
    PP = r_ref.shape[0]
    C = r_ref.shape[1]
    N = RWKV_HS
    hpp = LANE // N

    @pl.when(c == 0)
    def _():
        s_ref[...] = jnp.zeros_like(s_ref)

    ri = lax.broadcasted_iota(jnp.int32, (2 * C, C), 0)
    ci = lax.broadcasted_iota(jnp.int32, (2 * C, C), 1)
    mask2 = jnp.where(ri < C, ri, ri - (C - 1)) > ci
    trilb = jnp.where(lax.broadcasted_iota(jnp.int32, (C, C), 0) >= lax.broadcasted_iota(jnp.int32, (C, C), 1),
                      1.0, 0.0).astype(BF16)
    heads = []
    for pp in range(PP):
        prm = prm_ref[pp]
        r2 = r_ref[pp]
        v2 = v_ref[pp]
        log_decay2, a2, kk_raw2, kh2 = _rwkv_prep(r2, k_ref[pp], wl_ref[pp], al_ref[pp],
                                                  prm[0:1], prm[1:2], prm[2:3], prm[3:4])
        ld_h = log_decay2.astype(BF16)
        rem = log_decay2 - ld_h.astype(F32)
        ld_m = rem.astype(BF16)
        ld_l = (rem - ld_m.astype(F32)).astype(BF16)
        gcum2 = _dot(trilb, ld_h) + (_dot(trilb, ld_m) + _dot(trilb, ld_l))
        for hh in range(hpp):
            sl = slice(hh * N, (hh + 1) * N)
            r = r2[:, sl]
            v = v2[:, sl]
            k_h = kh2[:, sl]
            kk = kk_raw2[:, sl]
            kk = kk / jnp.maximum(jnp.sqrt(jnp.sum(kk * kk, axis=-1, keepdims=True)), 1e-12)
            gc = gcum2[:, sl]
            p_incl = jnp.exp(gc)
            p_inv = jnp.exp(-gc)
            at = -kk * jnp.exp(gc - log_decay2[:, sl])
            bt = kk * a2[:, sl] * p_inv
            kt = k_h * p_inv
            rt = r * p_incl
            heads.append(dict(
                idx=pp * hpp + hh, v=v, p_last=p_incl[C - 1:C, :],
                lhs=jnp.concatenate([at, rt], axis=0).astype(BF16),
                bk=jnp.concatenate([bt, kt], axis=0).astype(BF16),
                bonus=jnp.sum(r * k_h * prm[4:5, sl], axis=-1, keepdims=True) * v,
                ln_w=prm[5:6, sl], ln_b=prm[6:7, sl]))
    for h in heads:
        h["g_b"] = jnp.where(mask2, _dot_nt(h["lhs"], h["bk"][:C]), 0.0)
    tinv = _unit_lower_inverse_many([-h["g_b"][:C] for h in heads], C, _dot1)
    for h in heads:
        h["g_k"] = jnp.where(mask2, _dot_nt(h["lhs"], h["bk"][C:]), 0.0)
    for h in heads:
        h["w_kv"] = _dot1(h["g_k"], h["v"])
    for h in heads:
        h["s0"] = s_ref[h["idx"]]
        h["g_s"] = _dot_nt(h["lhs"], h["s0"].astype(BF16))
    for h, ti in zip(heads, tinv):
        h["u"] = _dot1(ti, h["g_s"][:C] + h["w_kv"][:C])
    for h in heads:
        uv = jnp.concatenate([h["u"], h["v"]], axis=0).astype(BF16)
        s_ref[h["idx"]] = (h["s0"] + _dot_tn(uv, h["bk"])) * h["p_last"]
    outs = []
    for h in heads:
        out = h["g_s"][C:] + h["w_kv"][C:] + _dot1(h["g_b"][C:], h["u"])
        mu = jnp.mean(out, axis=-1, keepdims=True)
        d = out - mu
        var = jnp.mean(d * d, axis=-1, keepdims=True)
        outs.append(d * lax.rsqrt(var + GN_EPS) * h["ln_w"] + h["ln_b"] + h["bonus"])
    for pp in range(PP):
        o_ref[pp] = jnp.concatenate(outs[pp * hpp:(pp + 1) * hpp], axis=1) * gate_ref[pp]

    @pl.when(c == pl.num_programs(1) - 1)
    def _():
        s_out_ref[...] = s_ref[...]


def _rwkv_chunk_prompt(r, k, v, wl, al, gate, prm, *, C, PP):
    npair, t, _ = r.shape
    hpp = LANE // RWKV_HS
    seq = pl.BlockSpec((PP, C, LANE), lambda p, c: (p, c, 0))
    return pl.pallas_call(
        _rwkv_chunk_kernel,
        grid=(npair // PP, t // C),
        in_specs=[seq] * 6 + [pl.BlockSpec((PP, 8, LANE), lambda p, c: (p, 0, 0))],
        out_specs=[seq, pl.BlockSpec((PP * hpp, RWKV_HS, RWKV_HS), lambda p, c: (p, 0, 0))],
        out_shape=[jax.ShapeDtypeStruct((npair, t, LANE), F32),
                   jax.ShapeDtypeStruct((npair * hpp, RWKV_HS, RWKV_HS), F32)],
        scratch_shapes=[pltpu.VMEM((PP * hpp, RWKV_HS, RWKV_HS), F32)],
        compiler_params=_cp(("parallel", "arbitrary")),
        name="rwkv_chunk",
    )(r, k, v, wl, al, gate, prm)


def _rwkv_dec_kernel(r_ref, k_ref, v_ref, wl_ref, al_ref, gate_ref, prm_ref, s0_ref, o_ref, s_out_ref):
    N = RWKV_HS
    prm = prm_ref[...]
    r2 = r_ref[0]
    k2 = k_ref[0]
    v2 = v_ref[0]
    log_decay2, a2, kk_raw2, kh2 = _rwkv_prep(r2, k2, wl_ref[0], al_ref[0], prm[0:1], prm[1:2], prm[2:3], prm[3:4])
    w2 = jnp.exp(log_decay2)
    gate = gate_ref[0]
    nh = r2.shape[1] // N
    sls = [slice(h * N, (h + 1) * N) for h in range(nh)]
    lane_sum = lambda xs: [jnp.sum(x, axis=-1, keepdims=True) for x in xs]
    kks = [kk_raw2[:, sl] for sl in sls]
    kks = [kk / jnp.maximum(jnp.sqrt(n2), 1e-12) for kk, n2 in zip(kks, lane_sum([kk * kk for kk in kks]))]
    s0s = [s0_ref[0, h] for h in range(nh)]
    sas = lane_sum([s0 * (-kk) for s0, kk in zip(s0s, kks)])
    vcols = [_row_to_col(v2[:, sl], N) for sl in sls]
    s_news = [s0 * w2[:, sl] + sa * (kk * a2[:, sl]) + vc * kh2[:, sl]
              for s0, sl, sa, kk, vc in zip(s0s, sls, sas, kks, vcols)]
    for h in range(nh):
        s_out_ref[0, h] = s_news[h]
    outs = [_col_to_row(oc, N) for oc in lane_sum([sn * r2[:, sl] for sn, sl in zip(s_news, sls)])]
    mus = lane_sum(outs)
    ds = [o - mu * (1.0 / N) for o, mu in zip(outs, mus)]
    vars_ = lane_sum([d * d for d in ds])
    bon = lane_sum([r2[:, sl] * kh2[:, sl] * prm[4:5, sl] for sl in sls])
    for sl, d, var, bo in zip(sls, ds, vars_, bon):
        gn = d * lax.rsqrt(var * (1.0 / N) + GN_EPS) * prm[5:6, sl] + prm[6:7, sl]
        o_ref[0, :, sl] = (gn + bo * v2[:, sl]) * gate[:, sl]


def _rwkv_sample(r, k, v, wl, al, gate, prm, s0):
    b, _, d = r.shape
    row = pl.BlockSpec((1, 1, d), lambda bi: (bi, 0, 0))
    st = pl.BlockSpec((1,) + s0.shape[1:], lambda bi: (bi, 0, 0, 0))
    return pl.pallas_call(
        _rwkv_dec_kernel,
        grid=(b,),
        in_specs=[row] * 6 + [pl.BlockSpec((8, d), lambda bi: (0, 0)), st],
        out_specs=[row, st],
        out_shape=[jax.ShapeDtypeStruct((b, 1, d), F32), jax.ShapeDtypeStruct(s0.shape, F32)],
        compiler_params=_cp(("parallel",)),
        name="rwkv_sample",
    )(r, k, v, wl, al, gate, prm, s0)


def _alibi_slopes():
    return jnp.asarray(2.0 ** (-8.0 * np.arange(1, NSA_HEADS + 1) / NSA_HEADS), dtype=F32)


def _overlap_matrix(nch, nsp):
    cstart = np.arange(nch)[:, None] * D_CMP
    sstart = np.arange(nsp)[None, :] * L_SLC
    return jnp.asarray(((cstart < sstart + L_SLC) & (cstart + L_CMP > sstart)).astype(np.float32))


def _pack_w_in(w):
    offs = np.concatenate([[0], np.cumsum(EVEN_SPLIT)])
    qa, kvc, kvs, kvw, ga, qkv, z, b, a = [w[:, offs[i]:offs[i + 1]] for i in range(len(EVEN_SPLIT))]
    used = COL_SM + A_GATES + 2 * DN_HEADS
    pad = jnp.zeros((w.shape[0], E_IN_PAD - used), w.dtype)
    return jnp.concatenate([qa, qkv, z, kvc, kvs, kvw, ga, b, a, pad], axis=1).astype(BF16)


def _pack_cmp_w1(w1):
    hid = w1.shape[-1]
    w = w1.reshape(2, 2, D_CMP, HD, hid).transpose(0, 2, 3, 1, 4)
    return w.reshape(2, D_CMP, HD, 2 * hid).astype(BF16)


def _pack_cmp_pe(pe):
    p = pe.reshape(2, 2, D_CMP, HD).transpose(0, 2, 1, 3)
    return jnp.concatenate([p, jnp.zeros((2, D_CMP, 6, HD), pe.dtype)], axis=2)


def _dn_gate_params(a_log, dt_bias):
    row = jnp.zeros((2, LANE), F32).at[0, SM_A:SM_A + DN_HEADS].set(a_log).at[1, SM_A:SM_A + DN_HEADS].set(dt_bias)
    return row, row.T


def _even_layer(xp, xs, w_in, w_out, pe, w1, b1, w2, conv_w, a_log, dt_bias, norm_w,
                cache_cmp, cache_slc, win_buf, conv_buf, dn_s0, page_table, g, b):
    t = xp.shape[0]
    bs = xs.shape[0]
    n_pages = page_table.shape[1]
    past = n_pages * PAGE_ROWS
    slopes = _alibi_slopes()
    w_in_p = _pack_w_in(w_in)
    w4 = _pack_cmp_w1(w1)
    pe8 = _pack_cmp_pe(pe)
    w2b = w2.astype(BF16)
    conv_wt = conv_w.T
    prow, pcol = _dn_gate_params(a_log, dt_bias)
    nw = norm_w.reshape(1, DV)

    hp = _mm(xp, w_in_p)
    hs = _mm(xs, w_in_p)
    kvc_p = hp[:, COL_KVC:COL_KVC + A_KVW]
    kvs_p = hp[:, COL_KVS:COL_KVS + A_KVW]
    kvw_p = hp[:, COL_KVW:COL_KVW + A_KVW]
    kvc_s = hs[:, COL_KVC:COL_KVC + A_KVW]
    kvs_s = hs[:, COL_KVS:COL_KVS + A_KVW]
    kvw_s = hs[:, COL_KVW:COL_KVW + A_KVW]

    TQ = _tile(t, (128, 64, 32, 16, 8))
    nch = t // D_CMP
    ns = t // L_SLC
    nsp = -(-ns // LANE) * LANE
    arange_pt = jnp.arange(t // PAGE_ROWS, dtype=jnp.int32)[None]
    h1 = _cmp_stage1(kvc_p.reshape(t // PAGE_ROWS, PAGE_SUBROWS, HD), arange_pt, w4)
    cmp_p = _cmp_stage2(h1, pe8, w4, b1, w2b)
    q_p = hp[:, COL_QA:COL_QA + A_Q]
    o_cmp, sel = _nsa_select(q_p[None], cmp_p, _overlap_matrix(nch, nsp), slopes,
                             TQ=TQ, NC=(t - L_CMP) // D_CMP + 1, NS=ns, q_off=0, BB=1)
    kvb = hp[:, COL_KVC:COL_KVC + 3 * A_KVW].astype(BF16)
    o_slc = _nsa_slc_prompt(q_p, sel[0], kvb, slopes, TQ=TQ, TK=_tile(t, (512, 256, 128, 64)))
    o_a_p = _nsa_win_prompt(q_p, kvb, hp, o_cmp[0], o_slc, slopes, TQ=TQ)

    nch_s = n_pages * CH_PER_PAGE
    nc_s = (past + 1 - L_CMP) // D_CMP + 1
    ns_s = -(-(past + 1) // L_SLC)
    nsp_s = -(-ns_s // LANE) * LANE
    h1s = _cmp_stage1(cache_cmp.reshape(cache_cmp.shape[0], PAGE_SUBROWS, HD), page_table, w4)
    cmp_s = _cmp_stage2(h1s, pe8, w4, b1, w2b)
    hs3 = hs[:, None, :]
    q_s = hs3[:, :, COL_QA:COL_QA + A_Q]
    o_cmp_s, sel_s = _nsa_select(q_s, cmp_s, _overlap_matrix(nch_s, nsp_s), slopes,
                                 TQ=1, NC=nc_s, NS=ns_s, q_off=past, BB=_tile(bs, (4, 2, 1)))
    o_slc_s = _nsa_slc_sample(q_s, sel_s[:, :, 0, :], kvs_s[:, None, :],
                              cache_slc.reshape(cache_slc.shape[0], PAGE_SUBROWS, HD), page_table, slopes)
    wb = win_buf.reshape(bs, win_buf.shape[1], A_KVW)
    o_a_s = _nsa_win_sample(q_s, wb, kvw_s[:, None, :], hs3[:, :, COL_SM:COL_SM + LANE], o_cmp_s, o_slc_s,
                            slopes, past=past)

    qkvn = _dn_conv_prompt(hp, jnp.zeros((8, DN_QKV), F32), conv_wt)
    smt = hp[:, COL_SM:COL_SM + LANE].T
    o_b_p, dn_s_p = _dn_chunk_prompt(qkvn, hp, smt, prow, pcol, nw, C=_tile(t, (128,)))
    o_b_s, dn_s_s = _dn_sample(conv_buf, hs3, conv_wt, prow, nw, dn_s0)

    w_out_b = w_out.astype(BF16)
    yp = _mm_ln(jnp.concatenate([o_a_p, o_b_p], axis=1), w_out_b, xp, g, b)
    ys = _mm_ln(jnp.concatenate([o_a_s[:, 0], o_b_s[:, 0]], axis=1), w_out_b, xs, g, b)

    kv6 = lambda a: a.reshape(a.shape[:-1] + (2, NSA_KV, HD))
    raw_p = hp[:, COL_DQKV:COL_DQKV + DN_QKV]
    raw_s = hs[:, COL_DQKV:COL_DQKV + DN_QKV]
    wlen = min(WINDOW, t)
    outs = dict(
        cmp_p=kv6(kvc_p)[None], cmp_s=kv6(kvc_s)[:, None],
        slc_p=kv6(kvs_p)[None], slc_s=kv6(kvs_s)[:, None],
        win_p=kv6(kvw_p[t - wlen:])[None],
        win_s=jnp.concatenate([win_buf, kv6(kvw_s)[:, None]], axis=1)[:, 1:],
        conv_p=jnp.concatenate([jnp.zeros((CONV_W - 1, DN_QKV), F32), raw_p], axis=0)[t:][None],
        conv_s=jnp.concatenate([conv_buf, raw_s[:, None]], axis=1)[:, 1:],
        dns_p=dn_s_p[None], dns_s=dn_s_s,
    )
    return yp, ys, outs


def _odd_layer(xp, xs, shift_s, s0_s, mix, wr, wk, wv, wo, w0, w1, w2, a0, a1, a2, g1, g2, k_k, k_a, r_k,
               ln_w, ln_b, g, b):
    t, d = xp.shape
    bs = xs.shape[0]
    npair = d // LANE
    xprev_p = jnp.concatenate([jnp.zeros((1, d), F32), xp[:-1]], axis=0)
    xprev_s = shift_s

    def padk(wa, wb_):
        r = wa.shape[1]
        rp = -(-r // LANE) * LANE
        return (jnp.pad(wa, ((0, 0), (0, rp - r))).astype(BF16), jnp.pad(wb_, ((0, rp - r), (0, 0))).astype(BF16))

    wrb, wkb, wvb, wob = (w.astype(BF16) for w in (wr, wk, wv, wo))
    w1b, w2b = padk(w1, w2)
    a1b, a2b = padk(a1, a2)
    g1b, g2b = padk(g1, g2)
    prm = jnp.stack([w0, a0, k_k, k_a, r_k.reshape(d), ln_w, ln_b, jnp.zeros((d,), F32)])

    def proj(x, xprev, pair_out):
        mr = lambda i: mix[i:i + 1]
        r = _mm(x, wrb, xprev=xprev, mixrow=mr(0), pair_out=pair_out)
        wl = _mm(_mm(x, w1b, xprev=xprev, mixrow=mr(1), act="tanh"), w2b, pair_out=pair_out)
        k = _mm(x, wkb, xprev=xprev, mixrow=mr(2), pair_out=pair_out)
        v = _mm(x, wvb, xprev=xprev, mixrow=mr(3), pair_out=pair_out)
        al = _mm(_mm(x, a1b, xprev=xprev, mixrow=mr(4)), a2b, pair_out=pair_out)
        gate = _mm(_mm(x, g1b, xprev=xprev, mixrow=mr(5), act="sigmoid"), g2b, pair_out=pair_out)
        return r, k, v, wl, al, gate

    pp = proj(xp, xprev_p, True)
    prm_pair = prm.reshape(8, npair, LANE).transpose(1, 0, 2)
    y_p, s_p = _rwkv_chunk_prompt(*pp, prm_pair, C=_tile(t, (64, 32, 16, 8)), PP=8)
    yp = _mm_ln(y_p, wob, xp, g, b, pair_in=True)

    ps = [a[:, None, :] for a in proj(xs, xprev_s, False)]
    y_s, s_s = _rwkv_sample(*ps, prm, s0_s)
    ys = _mm_ln(y_s[:, 0], wob, xs, g, b)
    return yp, ys, dict(shift_p=xp[t - 1:t], shift_s=xs, rs_p=s_p[None], rs_s=s_s)


def kernel(x_prompt, x_sample, cache_nsa_cmp, cache_nsa_slc, cache_nsa_win, state_dn_conv, state_dn_S, state_rwkv_shift, state_rwkv_S, page_table, ln_g, ln_b, ffn_wi, ffn_wo, mix_w_in, mix_w_out, nsa_cmp_pe, nsa_cmp_w1, nsa_cmp_b1, nsa_cmp_w2, dn_conv_w, dn_a_log, dn_dt_bias, dn_norm_w, rwkv_mix, rwkv_wr, rwkv_wk, rwkv_wv, rwkv_wo, rwkv_w0, rwkv_w1, rwkv_w2, rwkv_a0, rwkv_a1, rwkv_a2, rwkv_g1, rwkv_g2, rwkv_k_k, rwkv_k_a, rwkv_r_k, rwkv_ln_w, rwkv_ln_b):
    bp, t, d = x_prompt.shape
    assert bp == 1 and x_sample.shape[1] == 1
    depth = ffn_wi.shape[0]
    xp = x_prompt[0]
    xs = x_sample[:, 0]
    even, odd = [], []
    for l in range(depth):
        gl = lambda i: (ln_g[l, i][None], ln_b[l, i][None])
        wi = ffn_wi[l].astype(BF16)
        wo = ffn_wo[l].astype(BF16)
        xp = _ffn_ln(xp, wi[0], wo[0], *gl(0))
        xs = _ffn_ln(xs, wi[0], wo[0], *gl(0))
        if l % 2 == 0:
            e = l // 2
            xp, xs, o = _even_layer(
                xp, xs, mix_w_in[e], mix_w_out[e], nsa_cmp_pe[e], nsa_cmp_w1[e], nsa_cmp_b1[e], nsa_cmp_w2[e],
                dn_conv_w[e], dn_a_log[e], dn_dt_bias[e], dn_norm_w[e], cache_nsa_cmp[e], cache_nsa_slc[e],
                cache_nsa_win[e], state_dn_conv[e], state_dn_S[e], page_table, *gl(1))
            even.append(o)
        else:
            c = l // 2
            xp, xs, o = _odd_layer(
                xp, xs, state_rwkv_shift[c], state_rwkv_S[c], rwkv_mix[c], rwkv_wr[c], rwkv_wk[c], rwkv_wv[c],
                rwkv_wo[c], rwkv_w0[c], rwkv_w1[c], rwkv_w2[c], rwkv_a0[c], rwkv_a1[c], rwkv_a2[c], rwkv_g1[c],
                rwkv_g2[c], rwkv_k_k[c], rwkv_k_a[c], rwkv_r_k[c], rwkv_ln_w[c], rwkv_ln_b[c], *gl(1))
            odd.append(o)
        xp = _ffn_ln(xp, wi[1], wo[1], *gl(2))
        xs = _ffn_ln(xs, wi[1], wo[1], *gl(2))
    st = lambda lst, key: jnp.stack([o[key] for o in lst])
    return (xp[None], xs[:, None],
            st(even, "cmp_p"), st(even, "cmp_s"), st(even, "slc_p"), st(even, "slc_s"),
            st(even, "win_p"), st(even, "win_s"), st(even, "conv_p"), st(even, "conv_s"),
            st(even, "dns_p"), st(even, "dns_s"),
            st(odd, "shift_p"), st(odd, "shift_s"), st(odd, "rs_p"), st(odd, "rs_s"))
```

```python
import functools

import numpy as np
import jax
import jax.numpy as jnp
from jax import lax
from jax.experimental import pallas as pl
from jax.experimental.pallas import tpu as pltpu

F32 = jnp.float32
BF16 = jnp.bfloat16
HI = lax.Precision.HIGHEST

DEPTH = 2
ALPHA = (2 * DEPTH) ** 0.25
LN_EPS = 1e-5
RMS_EPS = 1e-6
NSA_HEADS = 8
NSA_KV = 2
NSA_GROUP = NSA_HEADS // NSA_KV
HD = 128
L_CMP = 32
D_CMP = 16
L_SLC = 64
N_SEL = 16
WINDOW = 512
NEG_INF = -1e30
FORCE_SCORE = 1e6
DN_HEADS = 8
DK = 128
DV = 128
CONV_W = 4
RWKV_HS = 64
GN_EPS = 64e-5

A_Q = NSA_HEADS * HD
A_KVW = 2 * NSA_KV * HD
A_GATES = 3 * NSA_HEADS
DN_QKV = DN_HEADS * (2 * DK + DV)
EVEN_SPLIT = (A_Q, A_KVW, A_KVW, A_KVW, A_GATES, DN_QKV, DN_HEADS * DV, DN_HEADS, DN_HEADS)
COL_QA = 0
COL_DQKV = 1024
COL_Z = 4096
COL_KVC = 5120
COL_KVS = 5632
COL_KVW = 6144
COL_SM = 6656
SM_BETA = A_GATES
SM_A = A_GATES + DN_HEADS
E_IN_PAD = 7168
LANE = 128
PAGE_ROWS = 128

VMEM_LIMIT = 56 * 1024 * 1024


def _cp(sem):
    return pltpu.CompilerParams(dimension_semantics=sem, vmem_limit_bytes=VMEM_LIMIT)


def _tile(n, prefs):
    for t in prefs:
        if n % t == 0:
            return t
    return n


def _dot(a, b, precision=None):
    return jnp.dot(a, b, preferred_element_type=F32, precision=precision)


def _dot_nt(a, b, precision=None):
    return lax.dot_general(a, b, (((1,), (1,)), ((), ())), preferred_element_type=F32, precision=precision)


def _dot_tn(a, b, precision=None):
    return lax.dot_general(a, b, (((0,), (0,)), ((), ())), preferred_element_type=F32, precision=precision)


def _layernorm_rows(y, g, b):
    mu = jnp.mean(y, axis=-1, keepdims=True)
    d = y - mu
    var = jnp.mean(d * d, axis=-1, keepdims=True)
    return d * lax.rsqrt(var + LN_EPS) * g + b


def _softplus(x):
    return jnp.maximum(x, 0.0) + jnp.log1p(jnp.exp(-jnp.abs(x)))


def _silu(x):
    return x * jax.nn.sigmoid(x)


def _ffn_kernel(x_ref, wg_ref, wu_ref, wo_ref, g_ref, b_ref, o_ref, xb_ref, acc_ref, *, nf):
    f = pl.program_id(1)

    @pl.when(f == 0)
    def _():
        xb_ref[...] = x_ref[...].astype(BF16)
        acc_ref[...] = jnp.zeros_like(acc_ref)

    xb = xb_ref[...]
    gate = _dot(xb, wg_ref[...])
    up = _dot(xb, wu_ref[...])
    act = (_silu(gate) * up).astype(BF16)
    acc_ref[...] += _dot(act, wo_ref[...])

    @pl.when(f == nf - 1)
    def _():
        y = ALPHA * x_ref[...] + 0.5 * acc_ref[...]
        o_ref[...] = _layernorm_rows(y, g_ref[...], b_ref[...])


def _ffn_ln(x, wi, wo, layer, which, g, b):
    m, d = x.shape
    f = wo.shape[2]
    tm = _tile(m, (512, 256, 128, 64, 32, 16, 8))
    tf = _tile(f, (512, 256, 128))
    nf = f // tf
    return pl.pallas_call(
        functools.partial(_ffn_kernel, nf=nf),
        grid=(m // tm, nf),
        in_specs=[
            pl.BlockSpec((tm, d), lambda i, j: (i, 0)),
            pl.BlockSpec((None, None, d, tf), lambda i, j: (layer, which, 0, j)),
            pl.BlockSpec((None, None, d, tf), lambda i, j: (layer, which, 0, j + nf)),
            pl.BlockSpec((None, None, tf, d), lambda i, j: (layer, which, j, 0)),
            pl.BlockSpec((1, d), lambda i, j: (0, 0)),
            pl.BlockSpec((1, d), lambda i, j: (0, 0)),
        ],
        out_specs=pl.BlockSpec((tm, d), lambda i, j: (i, 0)),
        out_shape=jax.ShapeDtypeStruct((m, d), F32),
        scratch_shapes=[pltpu.VMEM((tm, d), BF16), pltpu.VMEM((tm, d), F32)],
        compiler_params=_cp(("parallel", "arbitrary")),
        name="ffn_ln",
    )(x, wi, wi, wo, g, b)


def _mm_kernel(*refs, mix, act, pair_out):
    if mix:
        x_ref, xp_ref, m_ref, w_ref, o_ref, xb_ref = refs
    else:
        x_ref, w_ref, o_ref, xb_ref = refs

    @pl.when(pl.program_id(1) == 0)
    def _():
        x = x_ref[...]
        if mix:
            x = x + (xp_ref[...] - x) * m_ref[...]
        xb_ref[...] = x.astype(BF16)

    y = _dot(xb_ref[...], w_ref[...])
    if act == "tanh":
        y = jnp.tanh(y)
    elif act == "sigmoid":
        y = jax.nn.sigmoid(y)
    if pair_out:
        for p in range(o_ref.shape[0]):
            o_ref[p] = y[:, p * LANE:(p + 1) * LANE]
    else:
        o_ref[...] = y


def _mm(x, w, *, xprev=None, mixrow=None, act=None, pair_out=False, tn_prefs=(512, 256, 128)):
    m, k = x.shape
    n = w.shape[1]
    tm = _tile(m, (1024, 512, 256, 128, 64, 32, 16, 8))
    tn = _tile(n, tn_prefs)
    mix = xprev is not None
    in_specs = [pl.BlockSpec((tm, k), lambda i, j: (i, 0))]
    args = [x]
    if mix:
        in_specs += [pl.BlockSpec((tm, k), lambda i, j: (i, 0)), pl.BlockSpec((1, k), lambda i, j: (0, 0))]
        args += [xprev, mixrow]
    in_specs.append(pl.BlockSpec((k, tn), lambda i, j: (0, j)))
    args.append(w)
    if pair_out:
        npb = tn // LANE
        out_spec = pl.BlockSpec((npb, tm, LANE), lambda i, j: (j, i, 0))
        out_shape = jax.ShapeDtypeStruct((n // LANE, m, LANE), F32)
    else:
        out_spec = pl.BlockSpec((tm, tn), lambda i, j: (i, j))
        out_shape = jax.ShapeDtypeStruct((m, n), F32)
    return pl.pallas_call(
        functools.partial(_mm_kernel, mix=mix, act=act, pair_out=pair_out),
        grid=(m // tm, n // tn),
        in_specs=in_specs,
        out_specs=out_spec,
        out_shape=out_shape,
        scratch_shapes=[pltpu.VMEM((tm, k), BF16)],
        compiler_params=_cp(("parallel", "arbitrary")),
        name="matmul",
    )(*args)


def _mmln_kernel(*refs, pair_in, n_parts):
    a_refs = refs[:n_parts]
    w_ref, x_ref, g_ref, b_ref, o_ref = refs[n_parts:]
    if pair_in:
        a_ref = a_refs[0]
        y = _dot(jnp.concatenate([a_ref[p].astype(BF16) for p in range(a_ref.shape[0])], axis=1), w_ref[...])
    else:
        y = None
        k0 = 0
        for a_ref in a_refs:
            kw = a_ref.shape[1]
            part = _dot(a_ref[...].astype(BF16), w_ref[k0:k0 + kw, :])
            y = part if y is None else y + part
            k0 += kw
    o_ref[...] = _layernorm_rows(ALPHA * x_ref[...] + y, g_ref[...], b_ref[...])


def _mm_ln(a_parts, w, x, g, b, *, pair_in=False):
    m, d = x.shape
    k = w.shape[0]
    tm = _tile(m, (512, 256, 128, 64, 32, 16, 8))
    if pair_in:
        a_specs = [pl.BlockSpec((k // LANE, tm, LANE), lambda i: (0, i, 0))]
    else:
        a_specs = [pl.BlockSpec((tm, a.shape[1]), lambda i: (i, 0)) for a in a_parts]
    return pl.pallas_call(
        functools.partial(_mmln_kernel, pair_in=pair_in, n_parts=len(a_parts)),
        grid=(m // tm,),
        in_specs=a_specs + [
            pl.BlockSpec((k, d), lambda i: (0, 0)),
            pl.BlockSpec((tm, d), lambda i: (i, 0)),
            pl.BlockSpec((1, d), lambda i: (0, 0)),
            pl.BlockSpec((1, d), lambda i: (0, 0)),
        ],
        out_specs=pl.BlockSpec((tm, d), lambda i: (i, 0)),
        out_shape=jax.ShapeDtypeStruct((m, d), F32),
        compiler_params=_cp(("parallel",)),
        name="matmul_ln",
    )(*a_parts, w, x, g, b)


CH_PER_PAGE = PAGE_ROWS // D_CMP
KV_PARTS = 2 * NSA_KV
PAGE_SUBROWS = PAGE_ROWS * KV_PARTS


def _cmp1_kernel(pt_ref, *refs, G):
    del pt_ref
    page_refs = refs[:G]
    w_ref, o_ref = refs[G:]
    rows = G * CH_PER_PAGE
    for sg in range(KV_PARTS):
        s = sg // NSA_KV
        acc = jnp.zeros((rows, 2 * HD), F32)
        for p in range(D_CMP):
            x = jnp.concatenate(
                [page_refs[j][0, pl.ds(p * KV_PARTS + sg, CH_PER_PAGE, stride=D_CMP * KV_PARTS), :]
                 for j in range(G)], axis=0)
            acc = acc + _dot(x.astype(BF16), w_ref[s, p])
        o_ref[0, :, sg * 2 * HD:(sg + 1) * 2 * HD] = acc


def _cmp_stage1(pool, page_table, w4):
    b, n_pages = page_table.shape
    G = _tile(n_pages, (16, 8, 4, 2, 1))

    def page_map(j):
        return lambda bi, p, pt: (pt[bi, p * G + j], 0, 0)

    in_specs = [pl.BlockSpec((1, PAGE_SUBROWS, HD), page_map(j)) for j in range(G)]
    in_specs.append(pl.BlockSpec(w4.shape, lambda bi, p, pt: (0, 0, 0, 0)))
    grid_spec = pltpu.PrefetchScalarGridSpec(
        num_scalar_prefetch=1,
        grid=(b, n_pages // G),
        in_specs=in_specs,
        out_specs=pl.BlockSpec((1, G * CH_PER_PAGE, 8 * HD), lambda bi, p, pt: (bi, p, 0)),
    )
    return pl.pallas_call(
        functools.partial(_cmp1_kernel, G=G),
        grid_spec=grid_spec,
        out_shape=jax.ShapeDtypeStruct((b, n_pages * CH_PER_PAGE, 8 * HD), F32),
        compiler_params=_cp(("parallel", "arbitrary")),
        name="nsa_cmp_stage1",
    )(page_table, *([pool] * G), w4)


def _gelu_tanh(x):
    return 0.5 * x * (1.0 + jnp.tanh(np.sqrt(2.0 / np.pi).astype(np.float32) * (x + 0.044715 * (x * x * x))))


def _cmp2_kernel(h_ref, pe_ref, w4_ref, b1_ref, w2_ref, o_ref):
    nch = h_ref.shape[1]
    for s in range(2):
        pacc = jnp.zeros((8, 2 * HD), F32)
        for p in range(D_CMP):
            pacc = pacc + _dot(pe_ref[s, p].astype(BF16), w4_ref[s, p])
        const = pacc[0:1, 0:HD] + pacc[1:2, HD:2 * HD] + b1_ref[s:s + 1, :]
        for gi in range(NSA_KV):
            sg = s * NSA_KV + gi
            h0 = h_ref[0, :, sg * 2 * HD:sg * 2 * HD + HD]
            h1 = h_ref[0, :, sg * 2 * HD + HD:(sg + 1) * 2 * HD]
            hid = h0 + pltpu.roll(h1, nch - 1, 0) + const
            o_ref[0, :, sg * HD:(sg + 1) * HD] = _dot(_gelu_tanh(hid).astype(BF16), w2_ref[s])


def _cmp_stage2(h, pe8, w4, b1, w2):
    b, nch, _ = h.shape
    return pl.pallas_call(
        _cmp2_kernel,
        grid=(b,),
        in_specs=[
            pl.BlockSpec((1, nch, 8 * HD), lambda i: (i, 0, 0)),
            pl.BlockSpec(pe8.shape, lambda i: (0, 0, 0, 0)),
            pl.BlockSpec(w4.shape, lambda i: (0, 0, 0, 0)),
            pl.BlockSpec(b1.shape, lambda i: (0, 0)),
            pl.BlockSpec(w2.shape, lambda i: (0, 0, 0)),
        ],
        out_specs=pl.BlockSpec((1, nch, A_KVW), lambda i: (i, 0, 0)),
        out_shape=jax.ShapeDtypeStruct((b, nch, A_KVW), F32),
        compiler_params=_cp(("parallel",)),
        name="nsa_cmp_stage2",
    )(h, pe8, w4, b1, w2)


def _nsa_sel_kernel(slopes_ref, q_ref, cmp_ref, ov_ref, ocmp_ref, sel_ref, *, TQ, NC, NS, q_off, n_pick, RS):
    i = pl.program_id(1)
    bb = q_ref.shape[0]
    nch = cmp_ref.shape[1]
    nsp = ov_ref.shape[1]
    qpos = q_off + i * TQ + lax.broadcasted_iota(jnp.int32, (TQ, 1), 0)
    cidx = lax.broadcasted_iota(jnp.int32, (1, nch), 1)
    cstart = cidx * D_CMP
    cmask = jnp.logical_and(cstart + (L_CMP - 1) <= qpos, cidx < NC)
    mask_add = jnp.where(cmask, 0.0, NEG_INF)
    mask_mul = jnp.where(cmask, 1.0, 0.0)
    cdist = (qpos - cstart).astype(F32) - 0.5 * (L_CMP - 1)
    sid = lax.broadcasted_iota(jnp.int32, (1, nsp), 1)
    cur = lax.shift_right_arithmetic(qpos, int(np.log2(L_SLC)))
    svalid = jnp.logical_and(sid * L_SLC <= qpos, sid < NS)
    forced = jnp.logical_or(sid == 0, jnp.logical_or(sid == cur, sid == cur - 1))
    ov = ov_ref[...]
    scores = []
    for bi in range(bb):
        q = q_ref[bi]
        cm = cmp_ref[bi]
        for g in range(NSA_KV):
            kb = cm[:, g * HD:(g + 1) * HD].astype(BF16)
            vb = cm[:, (NSA_KV + g) * HD:(NSA_KV + g + 1) * HD].astype(BF16)
            pcs = jnp.zeros((TQ, nch), F32)
            for j in range(NSA_GROUP):
                h = NSA_GROUP * g + j
                qj = (q[:, h * HD:(h + 1) * HD] * HD ** -0.5).astype(BF16)
                lg = _dot_nt(qj, kb) - slopes_ref[h] * cdist + mask_add
                e = jnp.exp(lg - jnp.max(lg, axis=-1, keepdims=True))
                p = e * (mask_mul * (1.0 / jnp.sum(e, axis=-1, keepdims=True)))
                pcs = pcs + p
                ocmp_ref[bi, :, h * HD:(h + 1) * HD] = _dot(p.astype(BF16), vb)
            imp = _dot(pcs, ov, precision=HI)
            scores.append(jnp.where(svalid, jnp.where(forced, FORCE_SCORE, imp), NEG_INF))
    sidf = jnp.broadcast_to(sid, (RS, nsp)).astype(F32)
    chains = [sc[r0:r0 + RS] for sc in scores for r0 in range(0, TQ, RS)]
    sels = [jnp.zeros((RS, nsp), F32) for _ in chains]
    for _ in range(n_pick):
        mxs = [jnp.max(sc, axis=-1, keepdims=True) for sc in chains]
        firsts = [jnp.min(jnp.where(sc == mx, sidf, float(nsp)), axis=-1, keepdims=True)
                  for sc, mx in zip(chains, mxs)]
        picks = [sidf == f for f in firsts]
        sels = [jnp.where(jnp.logical_and(pk, mx > 0.5 * NEG_INF), 1.0, sl) for pk, mx, sl in zip(picks, mxs, sels)]
        chains = [jnp.where(pk, -jnp.inf, sc) for pk, sc in zip(picks, chains)]
    nsub = TQ // RS
    for bi in range(bb):
        for g in range(NSA_KV):
            for r in range(nsub):
                sel_ref[bi, g, r * RS:(r + 1) * RS, :] = sels[(bi * NSA_KV + g) * nsub + r]


def _nsa_select(q, cmp, overlap, slopes, *, TQ, NC, NS, q_off, BB):
    b, t, _ = q.shape
    nch = cmp.shape[1]
    nsp = overlap.shape[1]
    kern = functools.partial(_nsa_sel_kernel, TQ=TQ, NC=NC, NS=NS, q_off=q_off, n_pick=min(N_SEL, NS),
                             RS=_tile(TQ, (32, 16, 8)))
    return pl.pallas_call(
        kern,
        grid=(b // BB, t // TQ),
        in_specs=[
            pl.BlockSpec(memory_space=pltpu.SMEM),
            pl.BlockSpec((BB, TQ, A_Q), lambda bi, i: (bi, i, 0)),
            pl.BlockSpec((BB, nch, A_KVW), lambda bi, i: (bi, 0, 0)),
            pl.BlockSpec((nch, nsp), lambda bi, i: (0, 0)),
        ],
        out_specs=[
            pl.BlockSpec((BB, TQ, A_Q), lambda bi, i: (bi, i, 0)),
            pl.BlockSpec((BB, NSA_KV, TQ, nsp), lambda bi, i: (bi, 0, i, 0)),
        ],
        out_shape=[jax.ShapeDtypeStruct((b, t, A_Q), F32), jax.ShapeDtypeStruct((b, NSA_KV, t, nsp), F32)],
        compiler_params=_cp(("parallel", "parallel")),
        name="nsa_cmp_select",
    )(slopes, q, cmp, overlap)


def _stack_heads(q, tq):
    del tq
    return jnp.concatenate([q[:, j * HD:(j + 1) * HD] for j in range(NSA_GROUP)], axis=0)


def _online_step(carry, s, dist, mask_add, slopes, v):
    m, l, acc = carry
    lg = s + jnp.concatenate([mask_add - sl * dist for sl in slopes], axis=0)
    m_new = jnp.maximum(m, jnp.max(lg, axis=-1, keepdims=True))
    p = jnp.exp(lg - m_new)
    a = jnp.exp(m - m_new)
    l = a * l + jnp.sum(p, axis=-1, keepdims=True)
    acc = a * acc + _dot(p.astype(BF16), v)
    return m_new, l, acc


def _nsa_slc_kernel(slopes_ref, q_ref, sel_ref, k_ref, v_ref, o_ref, *, TQ, TK):
    g = pl.program_id(0)
    i = pl.program_id(1)
    nsp = sel_ref.shape[-1]
    rows = NSA_GROUP * TQ
    q4 = (_stack_heads(q_ref[...], TQ) * HD ** -0.5).astype(BF16)
    qposf = (i * TQ + lax.broadcasted_iota(jnp.int32, (TQ, 1), 0)).astype(F32)
    slopes = [slopes_ref[NSA_GROUP * g + j] for j in range(NSA_GROUP)]
    selb = sel_ref[0].astype(BF16)
    blk_per_tile = TK // L_SLC
    delta = (lax.broadcasted_iota(jnp.int32, (nsp, TK), 0)
             - lax.shift_right_arithmetic(lax.broadcasted_iota(jnp.int32, (nsp, TK), 1), int(np.log2(L_SLC))))
    kcol = lax.broadcasted_iota(jnp.int32, (1, TK), 1)

    def body(kt, carry):
        k0 = pl.multiple_of(kt * TK, TK)
        k = k_ref[pl.ds(k0, TK), :]
        v = v_ref[pl.ds(k0, TK), :]
        s = _dot_nt(q4, k)
        dist = qposf - (k0 + kcol).astype(F32)
        expand = jnp.where(delta == kt * blk_per_tile, 1.0, 0.0).astype(BF16)
        se = _dot(selb, expand)
        mask_add = jnp.where(jnp.logical_and(se > 0.5, dist >= 0.0), 0.0, NEG_INF)
        return _online_step(carry, s, dist, mask_add, slopes, v)

    ntile = (i * TQ + TQ + TK - 1) // TK
    init = (jnp.full((rows, 1), NEG_INF, F32), jnp.zeros((rows, 1), F32), jnp.zeros((rows, HD), F32))
    _, l, acc = lax.fori_loop(0, ntile, body, init)
    o = acc / l
    for j in range(NSA_GROUP):
        o_ref[:, j * HD:(j + 1) * HD] = o[j * TQ:(j + 1) * TQ]


def _nsa_slc_prompt(q, sel, kvb, slopes, *, TQ, TK):
    t = q.shape[0]
    nsp = sel.shape[-1]
    kb0 = A_KVW // HD
    return pl.pallas_call(
        functools.partial(_nsa_slc_kernel, TQ=TQ, TK=TK),
        grid=(NSA_KV, t // TQ),
        in_specs=[
            pl.BlockSpec(memory_space=pltpu.SMEM),
            pl.BlockSpec((TQ, NSA_GROUP * HD), lambda g, i: (i, g)),
            pl.BlockSpec((1, TQ, nsp), lambda g, i: (g, i, 0)),
            pl.BlockSpec((t, HD), lambda g, i: (0, kb0 + g)),
            pl.BlockSpec((t, HD), lambda g, i: (0, kb0 + NSA_KV + g)),
        ],
        out_specs=pl.BlockSpec((TQ, NSA_GROUP * HD), lambda g, i: (i, g)),
        out_shape=jax.ShapeDtypeStruct((t, A_Q), F32),
        compiler_params=_cp(("parallel", "parallel")),
        name="nsa_slc_prompt",
    )(slopes, q, sel, kvb, kvb)


def _nsa_win_kernel(slopes_ref, q_ref, k_ref, v_ref, ga_ref, ocmp_ref, oslc_ref, o_ref, *, TQ):
    g = pl.program_id(0)
    i = pl.program_id(1)
    rows = NSA_GROUP * TQ
    q4 = (_stack_heads(q_ref[...], TQ) * HD ** -0.5).astype(BF16)
    qposf = (i * TQ + lax.broadcasted_iota(jnp.int32, (TQ, 1), 0)).astype(F32)
    slopes = [slopes_ref[NSA_GROUP * g + j] for j in range(NSA_GROUP)]
    kcol = lax.broadcasted_iota(jnp.int32, (1, TQ), 1)

    def body(kt, carry):
        k0 = pl.multiple_of(kt * TQ, TQ)
        k = k_ref[pl.ds(k0, TQ), :]
        v = v_ref[pl.ds(k0, TQ), :]
        s = _dot_nt(q4, k)
        dist = qposf - (k0 + kcol).astype(F32)
        mask_add = jnp.where(jnp.logical_and(dist >= 0.0, dist <= float(WINDOW)), 0.0, NEG_INF)
        return _online_step(carry, s, dist, mask_add, slopes, v)

    lo = jnp.maximum(i - WINDOW // TQ, 0)
    init = (jnp.full((rows, 1), NEG_INF, F32), jnp.zeros((rows, 1), F32), jnp.zeros((rows, HD), F32))
    _, l, acc = lax.fori_loop(lo, i + 1, body, init)
    o_win = acc / l
    gates = jax.nn.sigmoid(ga_ref[...])
    for j in range(NSA_GROUP):
        ca = 3 * j
        cb = 3 * (NSA_GROUP + j)

        def gate(c):
            return jnp.where(g == 0, gates[:, ca + c:ca + c + 1], gates[:, cb + c:cb + c + 1])

        sl = slice(j * HD, (j + 1) * HD)
        o_ref[:, sl] = (gate(0) * ocmp_ref[:, sl] + gate(1) * oslc_ref[:, sl]
                        + gate(2) * o_win[j * TQ:(j + 1) * TQ])


def _nsa_win_prompt(q, kvb, h_in, o_cmp, o_slc, slopes, *, TQ):
    t = q.shape[0]
    kb0 = 2 * A_KVW // HD
    hspec = pl.BlockSpec((TQ, NSA_GROUP * HD), lambda g, i: (i, g))
    return pl.pallas_call(
        functools.partial(_nsa_win_kernel, TQ=TQ),
        grid=(NSA_KV, t // TQ),
        in_specs=[
            pl.BlockSpec(memory_space=pltpu.SMEM),
            hspec,
            pl.BlockSpec((t, HD), lambda g, i: (0, kb0 + g)),
            pl.BlockSpec((t, HD), lambda g, i: (0, kb0 + NSA_KV + g)),
            pl.BlockSpec((TQ, LANE), lambda g, i: (i, COL_SM // LANE)),
            hspec,
            hspec,
        ],
        out_specs=hspec,
        out_shape=jax.ShapeDtypeStruct((t, A_Q), F32),
        compiler_params=_cp(("parallel", "parallel")),
        name="nsa_win_prompt",
    )(slopes, q, kvb, kvb, h_in, o_cmp, o_slc)


def _rows8(row, width):
    return jnp.concatenate([row[:, h * width:(h + 1) * width] for h in range(NSA_HEADS)], axis=0)


def _kv_rows8(kn, off):
    return jnp.concatenate(
        [kn[:, off + (h // NSA_GROUP) * HD: off + (h // NSA_GROUP + 1) * HD] for h in range(NSA_HEADS)], axis=0)


def _slope8(slopes_ref):
    hrow = lax.broadcasted_iota(jnp.int32, (NSA_HEADS, 1), 0)
    out = jnp.zeros((NSA_HEADS, 1), F32)
    for h in range(NSA_HEADS):
        out = jnp.where(hrow == h, slopes_ref[h], out)
    return out


def _bf(x):
    return x.astype(BF16).astype(F32)


def _nsa_slc_dec_kernel(pt_ref, slopes_ref, *refs, G, past):
    del pt_ref
    pages = refs[:G]
    q_ref, sel_ref, knew_ref, o_ref, kv_s, m_s, l_s, acc_s = refs[G:]
    pg = pl.program_id(1)
    nsp = sel_ref.shape[-1]
    tk = G * PAGE_ROWS
    q8 = _rows8(q_ref[0], HD) * HD ** -0.5
    q8b = q8.astype(BF16)
    hrow = lax.broadcasted_iota(jnp.int32, (NSA_HEADS, 1), 0)
    first_group = hrow < NSA_GROUP

    @pl.when(pg == 0)
    def _():
        kn = knew_ref[0]
        m_s[...] = jnp.sum(_bf(q8) * _bf(_kv_rows8(kn, 0)), axis=-1, keepdims=True)
        l_s[...] = jnp.ones_like(l_s)
        acc_s[...] = _bf(_kv_rows8(kn, NSA_KV * HD))

    for j in range(G):
        for sg in range(KV_PARTS):
            kv_s[sg, j * PAGE_ROWS:(j + 1) * PAGE_ROWS, :] = pages[j][0, pl.ds(sg, PAGE_ROWS, stride=KV_PARTS), :]
    kpos = pg * tk + lax.broadcasted_iota(jnp.int32, (1, tk), 1)
    dist = (past - kpos).astype(F32)
    srow = lax.broadcasted_iota(jnp.int32, (nsp, tk), 0)
    scol = (pg * tk + lax.broadcasted_iota(jnp.int32, (nsp, tk), 1)) // L_SLC
    expand = jnp.where(srow == scol, 1.0, 0.0).astype(BF16)
    sel2 = sel_ref[0]
    sel8 = jnp.where(first_group, sel2[0:1, :], sel2[1:2, :]).astype(BF16)
    mask = _dot(sel8, expand) > 0.5
    s8 = jnp.where(first_group, _dot_nt(q8b, kv_s[0].astype(BF16)), _dot_nt(q8b, kv_s[1].astype(BF16)))
    lg = jnp.where(mask, s8 - _slope8(slopes_ref) * dist, NEG_INF)
    m = m_s[...]
    m_new = jnp.maximum(m, jnp.max(lg, axis=-1, keepdims=True))
    p = jnp.where(mask, jnp.exp(lg - m_new), 0.0)
    a = jnp.exp(m - m_new)
    pb = p.astype(BF16)
    pv = jnp.where(first_group, _dot(pb, kv_s[2].astype(BF16)), _dot(pb, kv_s[3].astype(BF16)))
    m_s[...] = m_new
    l_s[...] = a * l_s[...] + jnp.sum(p, axis=-1, keepdims=True)
    acc_s[...] = a * acc_s[...] + pv

    @pl.when(pg == pl.num_programs(1) - 1)
    def _():
        o = acc_s[...] / l_s[...]
        for h in range(NSA_HEADS):
            o_ref[0, :, h * HD:(h + 1) * HD] = o[h:h + 1, :]


def _nsa_slc_sample(q, sel, knew, pool, page_table, slopes):
    b, n_pages = page_table.shape
    G = _tile(n_pages, (8, 4, 2, 1))
    nsp = sel.shape[-1]
    past = n_pages * PAGE_ROWS

    def page_map(j):
        return lambda bi, p, pt: (pt[bi, p * G + j], 0, 0)

    in_specs = [pl.BlockSpec(memory_space=pltpu.SMEM)]
    in_specs += [pl.BlockSpec((1, PAGE_SUBROWS, HD), page_map(j)) for j in range(G)]
    in_specs += [
        pl.BlockSpec((1, 1, A_Q), lambda bi, p, pt: (bi, 0, 0)),
        pl.BlockSpec((1, NSA_KV, nsp), lambda bi, p, pt: (bi, 0, 0)),
        pl.BlockSpec((1, 1, A_KVW), lambda bi, p, pt: (bi, 0, 0)),
    ]
    grid_spec = pltpu.PrefetchScalarGridSpec(
        num_scalar_prefetch=1,
        grid=(b, n_pages // G),
        in_specs=in_specs,
        out_specs=pl.BlockSpec((1, 1, A_Q), lambda bi, p, pt: (bi, 0, 0)),
        scratch_shapes=[pltpu.VMEM((KV_PARTS, G * PAGE_ROWS, HD), F32), pltpu.VMEM((NSA_HEADS, 1), F32),
                        pltpu.VMEM((NSA_HEADS, 1), F32), pltpu.VMEM((NSA_HEADS, HD), F32)],
    )
    return pl.pallas_call(
        functools.partial(_nsa_slc_dec_kernel, G=G, past=past),
        grid_spec=grid_spec,
        out_shape=jax.ShapeDtypeStruct((b, 1, A_Q), F32),
        compiler_params=_cp(("parallel", "arbitrary")),
        name="nsa_slc_sample",
    )(page_table, slopes, *([pool] * G), q, sel, knew)


def _nsa_win_dec_kernel(slopes_ref, q_ref, wb_ref, knew_ref, ga_ref, ocmp_ref, oslc_ref, o_ref, *, past):
    nb = wb_ref.shape[1] // KV_PARTS
    q8 = _rows8(q_ref[0], HD) * HD ** -0.5
    q8b = q8.astype(BF16)
    hrow = lax.broadcasted_iota(jnp.int32, (NSA_HEADS, 1), 0)
    first_group = hrow < NSA_GROUP
    kn = knew_ref[0]
    wb = [wb_ref[0, pl.ds(sg, nb, stride=KV_PARTS), :].astype(BF16) for sg in range(KV_PARTS)]
    kwpos = past - nb + lax.broadcasted_iota(jnp.int32, (1, nb), 1)
    wd = past - kwpos
    mask = jnp.logical_and(jnp.logical_and(wd >= 0, wd <= WINDOW), kwpos >= 0)
    s8 = jnp.where(first_group, _dot_nt(q8b, wb[0]), _dot_nt(q8b, wb[1]))
    lg = jnp.where(mask, s8 - _slope8(slopes_ref) * wd.astype(F32), NEG_INF)
    s_self = jnp.sum(_bf(q8) * _bf(_kv_rows8(kn, 0)), axis=-1, keepdims=True)
    m = jnp.maximum(jnp.max(lg, axis=-1, keepdims=True), s_self)
    p = jnp.where(mask, jnp.exp(lg - m), 0.0)
    p_self = jnp.exp(s_self - m)
    pb = p.astype(BF16)
    pv = jnp.where(first_group, _dot(pb, wb[2]), _dot(pb, wb[3]))
    pv = pv + _bf(p_self) * _bf(_kv_rows8(kn, NSA_KV * HD))
    o_win = pv / (jnp.sum(p, axis=-1, keepdims=True) + p_self)
    gates = jax.nn.sigmoid(ga_ref[0])
    ocmp = ocmp_ref[0]
    oslc = oslc_ref[0]
    for h in range(NSA_HEADS):
        sl = slice(h * HD, (h + 1) * HD)
        o_ref[0, :, sl] = (gates[:, 3 * h:3 * h + 1] * ocmp[:, sl] + gates[:, 3 * h + 1:3 * h + 2] * oslc[:, sl]
                           + gates[:, 3 * h + 2:3 * h + 3] * o_win[h:h + 1, :])


def _nsa_win_sample(q, win_buf, knew, ga, o_cmp, o_slc, slopes, *, past):
    b = q.shape[0]
    nsub = win_buf.shape[1]
    row = lambda w: pl.BlockSpec((1, 1, w), lambda bi: (bi, 0, 0))
    return pl.pallas_call(
        functools.partial(_nsa_win_dec_kernel, past=past),
        grid=(b,),
        in_specs=[pl.BlockSpec(memory_space=pltpu.SMEM), row(A_Q),
                  pl.BlockSpec((1, nsub, HD), lambda bi: (bi, 0, 0)), row(A_KVW), row(LANE), row(A_Q), row(A_Q)],
        out_specs=row(A_Q),
        out_shape=jax.ShapeDtypeStruct((b, 1, A_Q), F32),
        compiler_params=_cp(("parallel",)),
        name="nsa_win_sample",
    )(slopes, q, win_buf, knew, ga, o_cmp, o_slc)


def _dn_conv_kernel(x_ref, prev_ref, buf_ref, w_ref, o_ref, hist_ref):
    c = pl.program_id(0)
    t = pl.program_id(1)
    tt = x_ref.shape[0]
    hist_ref[0:8, :] = jnp.where(t == 0, buf_ref[...], prev_ref[...])
    hist_ref[8:8 + tt, :] = x_ref[...]
    w = w_ref[...]
    y = jnp.zeros((tt, LANE), F32)
    for i in range(CONV_W):
        y = y + w[i:i + 1, :] * hist_ref[8 - (CONV_W - 1) + i: 8 - (CONV_W - 1) + i + tt, :]
    y = _silu(y)
    nrm = y * lax.rsqrt(jnp.sum(y * y, axis=-1, keepdims=True) + 1e-6)
    o_ref[...] = jnp.where(c < DN_HEADS, nrm * DK ** -0.5, jnp.where(c < 2 * DN_HEADS, nrm, y))


def _dn_conv_prompt(h_in, buf8, conv_wt):
    t = h_in.shape[0]
    tt = _tile(t, (1024, 512, 256, 128, 64, 32, 16, 8))
    c0 = COL_DQKV // LANE
    return pl.pallas_call(
        _dn_conv_kernel,
        grid=(DN_QKV // LANE, t // tt),
        in_specs=[
            pl.BlockSpec((tt, LANE), lambda c, i: (i, c0 + c)),
            pl.BlockSpec((8, LANE), lambda c, i: (jnp.maximum(i * (tt // 8) - 1, 0), c0 + c)),
            pl.BlockSpec((8, LANE), lambda c, i: (0, c)),
            pl.BlockSpec((CONV_W, LANE), lambda c, i: (0, c)),
        ],
        out_specs=pl.BlockSpec((tt, LANE), lambda c, i: (i, c)),
        out_shape=jax.ShapeDtypeStruct((t, DN_QKV), F32),
        scratch_shapes=[pltpu.VMEM((tt + 8, LANE), F32)],
        compiler_params=_cp(("parallel", "parallel")),
        name="dn_conv",
    )(h_in, h_in, buf8, conv_wt)


def _dot1(x, y):
    return _dot(x.astype(BF16), y.astype(BF16))


def _unit_lower_inverse_many(mats, n, mm):
    ri = lax.broadcasted_iota(jnp.int32, (n, n), 0)
    ci = lax.broadcasted_iota(jnp.int32, (n, n), 1)
    eye = jnp.where(ri == ci, 1.0, 0.0).astype(F32)
    base = min(16, n)

    def same_block(b):
        return (ri // b) == (ci // b)

    ps = [jnp.where(same_block(base), -a, 0.0) for a in mats]
    rs = [eye + p for p in ps]
    for _ in range(int(np.log2(base)) - 1):
        ps = [mm(p, p) for p in ps]
        rs = [r + mm(r, p) for r, p in zip(rs, ps)]
    b = base
    while b < n:
        offm = jnp.logical_and(same_block(2 * b), jnp.logical_not(same_block(b)))
        ts = [mm(jnp.where(offm, a, 0.0), r) for a, r in zip(mats, rs)]
        rs = [r - mm(r, t) for r, t in zip(rs, ts)]
        b *= 2
    return rs


def _dn_chunk_kernel(q_ref, k_ref, v_ref, sm_ref, smt_ref, z_ref, prow_ref, pcol_ref, nw_ref,
                     o_ref, s_out_ref, s_ref):
    c = pl.program_id(0)
    C = q_ref.shape[0]

    @pl.when(c == 0)
    def _():
        s_ref[...] = jnp.zeros_like(s_ref)

    ri = lax.broadcasted_iota(jnp.int32, (C, C), 0)
    ci = lax.broadcasted_iota(jnp.int32, (C, C), 1)
    lower = ri >= ci
    tril = jnp.where(lower, 1.0, 0.0).astype(F32)
    triu = jnp.where(ri <= ci, 1.0, 0.0).astype(F32)
    ri2 = lax.broadcasted_iota(jnp.int32, (2 * C, C), 0)
    ci2 = lax.broadcasted_iota(jnp.int32, (2 * C, C), 1)
    mask2 = jnp.where(ri2 < C, ri2, ri2 - (C - 1)) > ci2
    sm = sm_ref[...]
    smt = smt_ref[...]
    g_cols = -jnp.exp(prow_ref[0:1, :]) * _softplus(sm + prow_ref[1:2, :])
    g_rows = -jnp.exp(pcol_ref[:, 0:1]) * _softplus(smt + pcol_ref[:, 1:2])
    gcum_cols = _dot(tril, g_cols, HI)
    gcum_rows = _dot(g_rows, triu, HI)
    beta_cols = jax.nn.sigmoid(sm)
    nw = nw_ref[...]
    heads = []
    for h in range(DN_HEADS):
        sl = slice(h * DK, (h + 1) * DK)
        q = q_ref[:, sl]
        k = k_ref[:, sl]
        gc = gcum_cols[:, SM_A + h:SM_A + h + 1]
        gr = gcum_rows[SM_A + h:SM_A + h + 1, :]
        beta = beta_cols[:, SM_BETA + h:SM_BETA + h + 1]
        decay = jnp.where(lower, jnp.exp(jnp.where(lower, gc - gr, 0.0)), 0.0)
        kb = k * beta
        egc = jnp.exp(gc)
        gl = gc[C - 1:C, :]
        heads.append(dict(
            h=h, sl=sl, kb16=k.astype(BF16), decay2=jnp.concatenate([decay, decay], axis=0),
            kbq=jnp.concatenate([kb, q], axis=0).astype(BF16),
            rhs=jnp.concatenate([v_ref[:, sl] * beta, kb * egc], axis=1).astype(BF16),
            qg=q * egc, kdec=(k * jnp.exp(gl - gc)).astype(BF16), egl=jnp.exp(gl)))
    for d in heads:
        d["aa"] = jnp.where(mask2, _dot_nt(d["kbq"], d["kb16"]) * d["decay2"], 0.0)
    tinv = _unit_lower_inverse_many([d["aa"][:C] for d in heads], C, _dot1)
    for d, ti in zip(heads, tinv):
        d["sol"] = _dot(ti.astype(BF16), d["rhs"])
    for d in heads:
        d["s"] = s_ref[d["h"]]
        d["ks2"] = _dot1(jnp.concatenate([d["sol"][:, DV:], d["qg"]], axis=0), d["s"])
    for d in heads:
        d["v_new"] = (d["sol"][:, :DV] - d["ks2"][:C]).astype(BF16)
    for d in heads:
        s_ref[d["h"]] = d["s"] * d["egl"] + _dot_tn(d["kdec"], d["v_new"])
    for d in heads:
        o = d["ks2"][C:] + _dot(d["aa"][C:].astype(BF16), d["v_new"])
        o = o * lax.rsqrt(jnp.mean(o * o, axis=-1, keepdims=True) + RMS_EPS) * nw
        o_ref[:, d["sl"]] = o * _silu(z_ref[:, d["sl"]])

    @pl.when(c == pl.num_programs(0) - 1)
    def _():
        s_out_ref[...] = s_ref[...]


def _dn_chunk_prompt(qkvn, h_in, smt, prow, pcol, norm_w, *, C):
    t = qkvn.shape[0]
    hw = DN_HEADS * DK
    return pl.pallas_call(
        _dn_chunk_kernel,
        grid=(t // C,),
        in_specs=[
            pl.BlockSpec((C, hw), lambda c: (c, 0)),
            pl.BlockSpec((C, hw), lambda c: (c, 1)),
            pl.BlockSpec((C, hw), lambda c: (c, 2)),
            pl.BlockSpec((C, LANE), lambda c: (c, COL_SM // LANE)),
            pl.BlockSpec((LANE, C), lambda c: (0, c)),
            pl.BlockSpec((C, hw), lambda c: (c, COL_Z // hw)),
            pl.BlockSpec((2, LANE), lambda c: (0, 0)),
            pl.BlockSpec((LANE, 2), lambda c: (0, 0)),
            pl.BlockSpec((1, DV), lambda c: (0, 0)),
        ],
        out_specs=[pl.BlockSpec((C, hw), lambda c: (c, 0)),
                   pl.BlockSpec((DN_HEADS, DK, DV), lambda c: (0, 0, 0))],
        out_shape=[jax.ShapeDtypeStruct((t, hw), F32), jax.ShapeDtypeStruct((DN_HEADS, DK, DV), F32)],
        scratch_shapes=[pltpu.VMEM((DN_HEADS, DK, DV), F32)],
        compiler_params=_cp(("arbitrary",)),
        name="dn_chunk",
    )(qkvn, qkvn, qkvn, h_in, smt, h_in, prow, pcol, norm_w)


def _row_to_col(row, n):
    ri = lax.broadcasted_iota(jnp.int32, (n, n), 0)
    ci = lax.broadcasted_iota(jnp.int32, (n, n), 1)
    return jnp.sum(jnp.where(ri == ci, jnp.broadcast_to(row, (n, n)), 0.0), axis=1, keepdims=True)


def _col_to_row(col, n):
    ri = lax.broadcasted_iota(jnp.int32, (n, n), 0)
    ci = lax.broadcasted_iota(jnp.int32, (n, n), 1)
    return jnp.sum(jnp.where(ri == ci, jnp.broadcast_to(col, (n, n)), 0.0), axis=0, keepdims=True)


def _dn_dec_kernel(buf_ref, xq_ref, xk_ref, xv_ref, w_ref, sm_ref, z_ref, prow_ref, nw_ref, s0_ref,
                   o_ref, s_out_ref):
    hw = DN_HEADS * DK
    buf = buf_ref[0]
    w = w_ref[...]
    sm = sm_ref[0]
    g_row = -jnp.exp(prow_ref[0:1, :]) * _softplus(sm + prow_ref[1:2, :])
    beta_row = jax.nn.sigmoid(sm)
    nw = nw_ref[...]
    z = z_ref[0]
    parts = []
    for part, x_ref in enumerate((xq_ref, xk_ref, xv_ref)):
        sl = slice(part * hw, (part + 1) * hw)
        y = w[CONV_W - 1:CONV_W, sl] * x_ref[0]
        for i in range(CONV_W - 1):
            y = y + w[i:i + 1, sl] * buf[i:i + 1, sl]
        parts.append(_silu(y))
    for h in range(DN_HEADS):
        sl = slice(h * DK, (h + 1) * DK)
        q = parts[0][:, sl]
        k = parts[1][:, sl]
        v = parts[2][:, sl]
        q = q * lax.rsqrt(jnp.sum(q * q, axis=-1, keepdims=True) + 1e-6) * DK ** -0.5
        k = k * lax.rsqrt(jnp.sum(k * k, axis=-1, keepdims=True) + 1e-6)
        a = jnp.exp(g_row[:, SM_A + h:SM_A + h + 1])
        beta = beta_row[:, SM_BETA + h:SM_BETA + h + 1]
        k_col = _row_to_col(k, DK)
        q_col = _row_to_col(q, DK)
        s0 = s0_ref[0, h]
        u = beta * (v - a * jnp.sum(s0 * k_col, axis=0, keepdims=True))
        s_new = a * s0 + k_col * u
        s_out_ref[0, h] = s_new
        o = jnp.sum(s_new * q_col, axis=0, keepdims=True)
        o = o * lax.rsqrt(jnp.mean(o * o, axis=-1, keepdims=True) + RMS_EPS) * nw
        o_ref[0, :, sl] = o * _silu(z[:, sl])


def _dn_sample(conv_buf, h3, conv_wt, prow, norm_w, s0):
    b = h3.shape[0]
    hw = DN_HEADS * DK
    c0 = COL_DQKV // hw
    row = lambda w, j: pl.BlockSpec((1, 1, w), lambda bi: (bi, 0, j))
    return pl.pallas_call(
        _dn_dec_kernel,
        grid=(b,),
        in_specs=[
            pl.BlockSpec((1, CONV_W - 1, DN_QKV), lambda bi: (bi, 0, 0)),
            row(hw, c0), row(hw, c0 + 1), row(hw, c0 + 2),
            pl.BlockSpec((CONV_W, DN_QKV), lambda bi: (0, 0)),
            row(LANE, COL_SM // LANE),
            row(hw, COL_Z // hw),
            pl.BlockSpec((2, LANE), lambda bi: (0, 0)),
            pl.BlockSpec((1, DV), lambda bi: (0, 0)),
            pl.BlockSpec((1, DN_HEADS, DK, DV), lambda bi: (bi, 0, 0, 0)),
        ],
        out_specs=[row(hw, 0), pl.BlockSpec((1, DN_HEADS, DK, DV), lambda bi: (bi, 0, 0, 0))],
        out_shape=[jax.ShapeDtypeStruct((b, 1, hw), F32), jax.ShapeDtypeStruct(s0.shape, F32)],
        compiler_params=_cp(("parallel",)),
        name="dn_sample",
    )(conv_buf, h3, h3, h3, conv_wt, h3, h3, prow, norm_w, s0)


def _rwkv_prep(r, k, wl, al, w0, a0, k_k, k_a):
    w_log = -_softplus(-(w0 + wl)) - 0.5
    log_decay = -jnp.exp(w_log)
    a = jax.nn.sigmoid(a0 + al)
    kk_raw = k * k_k
    k_h = k * (1.0 + (a - 1.0) * k_a)
    del r
    return log_decay, a, kk_raw, k_h


def _rwkv_chunk_kernel(r_ref, k_ref, v_ref, wl_ref, al_ref, gate_ref, prm_ref, o_ref, s_out_ref, s_ref):
    c = pl.program_id(1)
    PP = r_ref.shape[0]
    C = r_ref.shape[1]
    N = RWKV_HS
    hpp = LANE // N

    @pl.when(c == 0)
    def _():
        s_ref[...] = jnp.zeros_like(s_ref)

    ri = lax.broadcasted_iota(jnp.int32, (2 * C, C), 0)
    ci = lax.broadcasted_iota(jnp.int32, (2 * C, C), 1)
    mask2 = jnp.where(ri < C, ri, ri - (C - 1)) > ci
    trilb = jnp.where(lax.broadcasted_iota(jnp.int32, (C, C), 0) >= lax.broadcasted_iota(jnp.int32, (C, C), 1),
                      1.0, 0.0).astype(BF16)
    heads = []
    for pp in range(PP):
        prm = prm_ref[pp]
        r2 = r_ref[pp]
        v2 = v_ref[pp]
        log_decay2, a2, kk_raw2, kh2 = _rwkv_prep(r2, k_ref[pp], wl_ref[pp], al_ref[pp],
                                                  prm[0:1], prm[1:2], prm[2:3], prm[3:4])
        ld_h = log_decay2.astype(BF16)
        rem = log_decay2 - ld_h.astype(F32)
        ld_m = rem.astype(BF16)
        ld_l = (rem - ld_m.astype(F32)).astype(BF16)
        gcum2 = _dot(trilb, ld_h) + (_dot(trilb, ld_m) + _dot(trilb, ld_l))
        for hh in range(hpp):
            sl = slice(hh * N, (hh + 1) * N)
            r = r2[:, sl]
            v = v2[:, sl]
            k_h = kh2[:, sl]
            kk = kk_raw2[:, sl]
            kk = kk / jnp.maximum(jnp.sqrt(jnp.sum(kk * kk, axis=-1, keepdims=True)), 1e-12)
            gc = gcum2[:, sl]
            p_incl = jnp.exp(gc)
            p_inv = jnp.exp(-gc)
            at = -kk * jnp.exp(gc - log_decay2[:, sl])
            bt = kk * a2[:, sl] * p_inv
            kt = k_h * p_inv
            rt = r * p_incl
            heads.append(dict(
                idx=pp * hpp + hh, v=v, p_last=p_incl[C - 1:C, :],
                lhs=jnp.concatenate([at, rt], axis=0).astype(BF16),
                bk=jnp.concatenate([bt, kt], axis=0).astype(BF16),
                bonus=jnp.sum(r * k_h * prm[4:5, sl], axis=-1, keepdims=True) * v,
                ln_w=prm[5:6, sl], ln_b=prm[6:7, sl]))
    for h in heads:
        h["g_b"] = jnp.where(mask2, _dot_nt(h["lhs"], h["bk"][:C]), 0.0)
    tinv = _unit_lower_inverse_many([-h["g_b"][:C] for h in heads], C, _dot1)
    for h in heads:
        h["g_k"] = jnp.where(mask2, _dot_nt(h["lhs"], h["bk"][C:]), 0.0)
    for h in heads:
        h["w_kv"] = _dot1(h["g_k"], h["v"])
    for h in heads:
        h["s0"] = s_ref[h["idx"]]
        h["g_s"] = _dot_nt(h["lhs"], h["s0"].astype(BF16))
    for h, ti in zip(heads, tinv):
        h["u"] = _dot1(ti, h["g_s"][:C] + h["w_kv"][:C])
    for h in heads:
        uv = jnp.concatenate([h["u"], h["v"]], axis=0).astype(BF16)
        s_ref[h["idx"]] = (h["s0"] + _dot_tn(uv, h["bk"])) * h["p_last"]
    outs = []
    for h in heads:
        out = h["g_s"][C:] + h["w_kv"][C:] + _dot1(h["g_b"][C:], h["u"])
        mu = jnp.mean(out, axis=-1, keepdims=True)
        d = out - mu
        var = jnp.mean(d * d, axis=-1, keepdims=True)
        outs.append(d * lax.rsqrt(var + GN_EPS) * h["ln_w"] + h["ln_b"] + h["bonus"])
    for pp in range(PP):
        o_ref[pp] = jnp.concatenate(outs[pp * hpp:(pp + 1) * hpp], axis=1) * gate_ref[pp]

    @pl.when(c == pl.num_programs(1) - 1)
    def _():
        s_out_ref[...] = s_ref[...]


def _rwkv_chunk_prompt(r, k, v, wl, al, gate, prm, *, C, PP):
    npair, t, _ = r.shape
    hpp = LANE // RWKV_HS
    seq = pl.BlockSpec((PP, C, LANE), lambda p, c: (p, c, 0))
    return pl.pallas_call(
        _rwkv_chunk_kernel,
        grid=(npair // PP, t // C),
        in_specs=[seq] * 6 + [pl.BlockSpec((PP, 8, LANE), lambda p, c: (p, 0, 0))],
        out_specs=[seq, pl.BlockSpec((PP * hpp, RWKV_HS, RWKV_HS), lambda p, c: (p, 0, 0))],
        out_shape=[jax.ShapeDtypeStruct((npair, t, LANE), F32),
                   jax.ShapeDtypeStruct((npair * hpp, RWKV_HS, RWKV_HS), F32)],
        scratch_shapes=[pltpu.VMEM((PP * hpp, RWKV_HS, RWKV_HS), F32)],
        compiler_params=_cp(("parallel", "arbitrary")),
        name="rwkv_chunk",
    )(r, k, v, wl, al, gate, prm)


def _rwkv_dec_kernel(r_ref, k_ref, v_ref, wl_ref, al_ref, gate_ref, prm_ref, s0_ref, o_ref, s_out_ref):
    N = RWKV_HS
    prm = prm_ref[...]
    r2 = r_ref[0]
    k2 = k_ref[0]
    v2 = v_ref[0]
    log_decay2, a2, kk_raw2, kh2 = _rwkv_prep(r2, k2, wl_ref[0], al_ref[0], prm[0:1], prm[1:2], prm[2:3], prm[3:4])
    w2 = jnp.exp(log_decay2)
    gate = gate_ref[0]
    nh = r2.shape[1] // N
    sls = [slice(h * N, (h + 1) * N) for h in range(nh)]
    lane_sum = lambda xs: [jnp.sum(x, axis=-1, keepdims=True) for x in xs]
    kks = [kk_raw2[:, sl] for sl in sls]
    kks = [kk / jnp.maximum(jnp.sqrt(n2), 1e-12) for kk, n2 in zip(kks, lane_sum([kk * kk for kk in kks]))]
    s0s = [s0_ref[0, h] for h in range(nh)]
    sas = lane_sum([s0 * (-kk) for s0, kk in zip(s0s, kks)])
    vcols = [_row_to_col(v2[:, sl], N) for sl in sls]
    s_news = [s0 * w2[:, sl] + sa * (kk * a2[:, sl]) + vc * kh2[:, sl]
              for s0, sl, sa, kk, vc in zip(s0s, sls, sas, kks, vcols)]
    for h in range(nh):
        s_out_ref[0, h] = s_news[h]
    outs = [_col_to_row(oc, N) for oc in lane_sum([sn * r2[:, sl] for sn, sl in zip(s_news, sls)])]
    mus = lane_sum(outs)
    ds = [o - mu * (1.0 / N) for o, mu in zip(outs, mus)]
    vars_ = lane_sum([d * d for d in ds])
    bon = lane_sum([r2[:, sl] * kh2[:, sl] * prm[4:5, sl] for sl in sls])
    for sl, d, var, bo in zip(sls, ds, vars_, bon):
        gn = d * lax.rsqrt(var * (1.0 / N) + GN_EPS) * prm[5:6, sl] + prm[6:7, sl]
        o_ref[0, :, sl] = (gn + bo * v2[:, sl]) * gate[:, sl]


def _rwkv_sample(r, k, v, wl, al, gate, prm, s0):
    b, _, d = r.shape
    row = pl.BlockSpec((1, 1, d), lambda bi: (bi, 0, 0))
    st = pl.BlockSpec((1,) + s0.shape[1:], lambda bi: (bi, 0, 0, 0))
    return pl.pallas_call(
        _rwkv_dec_kernel,
        grid=(b,),
        in_specs=[row] * 6 + [pl.BlockSpec((8, d), lambda bi: (0, 0)), st],
        out_specs=[row, st],
        out_shape=[jax.ShapeDtypeStruct((b, 1, d), F32), jax.ShapeDtypeStruct(s0.shape, F32)],
        compiler_params=_cp(("parallel",)),
        name="rwkv_sample",
    )(r, k, v, wl, al, gate, prm, s0)


def _alibi_slopes():
    return jnp.asarray(2.0 ** (-8.0 * np.arange(1, NSA_HEADS + 1) / NSA_HEADS), dtype=F32)


def _overlap_matrix(nch, nsp):
    cstart = np.arange(nch)[:, None] * D_CMP
    sstart = np.arange(nsp)[None, :] * L_SLC
    return jnp.asarray(((cstart < sstart + L_SLC) & (cstart + L_CMP > sstart)).astype(np.float32))


def _pack_w_in(w):
    offs = np.concatenate([[0], np.cumsum(EVEN_SPLIT)])
    qa, kvc, kvs, kvw, ga, qkv, z, b, a = [w[:, offs[i]:offs[i + 1]] for i in range(len(EVEN_SPLIT))]
    used = COL_SM + A_GATES + 2 * DN_HEADS
    pad = jnp.zeros((w.shape[0], E_IN_PAD - used), w.dtype)
    return jnp.concatenate([qa, qkv, z, kvc, kvs, kvw, ga, b, a, pad], axis=1).astype(BF16)


def _pack_cmp_w1(w1):
    hid = w1.shape[-1]
    w = w1.reshape(2, 2, D_CMP, HD, hid).transpose(0, 2, 3, 1, 4)
    return w.reshape(2, D_CMP, HD, 2 * hid).astype(BF16)


def _pack_cmp_pe(pe):
    p = pe.reshape(2, 2, D_CMP, HD).transpose(0, 2, 1, 3)
    return jnp.concatenate([p, jnp.zeros((2, D_CMP, 6, HD), pe.dtype)], axis=2)


def _dn_gate_params(a_log, dt_bias):
    row = jnp.zeros((2, LANE), F32).at[0, SM_A:SM_A + DN_HEADS].set(a_log).at[1, SM_A:SM_A + DN_HEADS].set(dt_bias)
    return row, row.T


def _even_layer(xp, xs, w_in, w_out, pe, w1, b1, w2, conv_w, a_log, dt_bias, norm_w,
                cache_cmp, cache_slc, win_buf, conv_buf, dn_s0, page_table, g, b):
    t = xp.shape[0]
    bs = xs.shape[0]
    n_pages = page_table.shape[1]
    past = n_pages * PAGE_ROWS
    slopes = _alibi_slopes()
    w_in_p = _pack_w_in(w_in)
    w4 = _pack_cmp_w1(w1)
    pe8 = _pack_cmp_pe(pe)
    w2b = w2.astype(BF16)
    conv_wt = conv_w.T
    prow, pcol = _dn_gate_params(a_log, dt_bias)
    nw = norm_w.reshape(1, DV)

    hp = _mm(xp, w_in_p)
    hs = _mm(xs, w_in_p)
    kvc_p = hp[:, COL_KVC:COL_KVC + A_KVW]
    kvs_p = hp[:, COL_KVS:COL_KVS + A_KVW]
    kvw_p = hp[:, COL_KVW:COL_KVW + A_KVW]
    kvc_s = hs[:, COL_KVC:COL_KVC + A_KVW]
    kvs_s = hs[:, COL_KVS:COL_KVS + A_KVW]
    kvw_s = hs[:, COL_KVW:COL_KVW + A_KVW]

    TQ = _tile(t, (128, 64, 32, 16, 8))
    nch = t // D_CMP
    ns = t // L_SLC
    nsp = -(-ns // LANE) * LANE
    arange_pt = jnp.arange(t // PAGE_ROWS, dtype=jnp.int32)[None]
    h1 = _cmp_stage1(kvc_p.reshape(t // PAGE_ROWS, PAGE_SUBROWS, HD), arange_pt, w4)
    cmp_p = _cmp_stage2(h1, pe8, w4, b1, w2b)
    q_p = hp
    o_cmp, sel = _nsa_select(q_p[None], cmp_p, _overlap_matrix(nch, nsp), slopes,
                             TQ=TQ, NC=(t - L_CMP) // D_CMP + 1, NS=ns, q_off=0, BB=1)
    kvb = hp[:, COL_KVC:COL_KVC + 3 * A_KVW].astype(BF16)
    o_slc = _nsa_slc_prompt(q_p, sel[0], kvb, slopes, TQ=TQ, TK=_tile(t, (512, 256, 128, 64)))
    o_a_p = _nsa_win_prompt(q_p, kvb, hp, o_cmp[0], o_slc, slopes, TQ=TQ)

    nch_s = n_pages * CH_PER_PAGE
    nc_s = (past + 1 - L_CMP) // D_CMP + 1
    ns_s = -(-(past + 1) // L_SLC)
    nsp_s = -(-ns_s // LANE) * LANE
    h1s = _cmp_stage1(cache_cmp.reshape(cache_cmp.shape[0], PAGE_SUBROWS, HD), page_table, w4)
    cmp_s = _cmp_stage2(h1s, pe8, w4, b1, w2b)
    hs3 = hs[:, None, :]
    q_s = hs3[:, :, COL_QA:COL_QA + A_Q]
    o_cmp_s, sel_s = _nsa_select(q_s, cmp_s, _overlap_matrix(nch_s, nsp_s), slopes,
                                 TQ=1, NC=nc_s, NS=ns_s, q_off=past, BB=_tile(bs, (4, 2, 1)))
    o_slc_s = _nsa_slc_sample(q_s, sel_s[:, :, 0, :], kvs_s[:, None, :],
                              cache_slc.reshape(cache_slc.shape[0], PAGE_SUBROWS, HD), page_table, slopes)
    wb = win_buf.reshape(bs, win_buf.shape[1] * KV_PARTS, HD)
    o_a_s = _nsa_win_sample(q_s, wb, kvw_s[:, None, :], hs3[:, :, COL_SM:COL_SM + LANE], o_cmp_s, o_slc_s,
                            slopes, past=past)

    qkvn = _dn_conv_prompt(hp, jnp.zeros((8, DN_QKV), F32), conv_wt)
    smt = hp[:, COL_SM:COL_SM + LANE].T
    o_b_p, dn_s_p = _dn_chunk_prompt(qkvn, hp, smt, prow, pcol, nw, C=_tile(t, (128,)))
    o_b_s, dn_s_s = _dn_sample(conv_buf, hs3, conv_wt, prow, nw, dn_s0)

    w_out_b = w_out.astype(BF16)
    yp = _mm_ln([o_a_p, o_b_p], w_out_b, xp, g, b)
    ys = _mm_ln([o_a_s[:, 0], o_b_s[:, 0]], w_out_b, xs, g, b)

    kv6 = lambda a: a.reshape(a.shape[:-1] + (2, NSA_KV, HD))
    raw_p = hp[:, COL_DQKV:COL_DQKV + DN_QKV]
    raw_s = hs[:, COL_DQKV:COL_DQKV + DN_QKV]
    wlen = min(WINDOW, t)
    outs = dict(
        cmp_p=kv6(kvc_p)[None], cmp_s=kv6(kvc_s)[:, None],
        slc_p=kv6(kvs_p)[None], slc_s=kv6(kvs_s)[:, None],
        win_p=kv6(kvw_p[t - wlen:])[None],
        win_s=jnp.concatenate([win_buf, kv6(kvw_s)[:, None]], axis=1)[:, 1:],
        conv_p=jnp.concatenate([jnp.zeros((CONV_W - 1, DN_QKV), F32), raw_p], axis=0)[t:][None],
        conv_s=jnp.concatenate([conv_buf, raw_s[:, None]], axis=1)[:, 1:],
        dns_p=dn_s_p[None], dns_s=dn_s_s,
    )
    return yp, ys, outs


def _odd_layer(xp, xs, shift_s, s0_s, mix, wr, wk, wv, wo, w0, w1, w2, a0, a1, a2, g1, g2, k_k, k_a, r_k,
               ln_w, ln_b, g, b):
    t, d = xp.shape
    bs = xs.shape[0]
    npair = d // LANE
    xprev_p = jnp.concatenate([jnp.zeros((1, d), F32), xp[:-1]], axis=0)
    xprev_s = shift_s

    def padk(wa, wb_):
        r = wa.shape[1]
        rp = -(-r // LANE) * LANE
        return (jnp.pad(wa, ((0, 0), (0, rp - r))).astype(BF16), jnp.pad(wb_, ((0, rp - r), (0, 0))).astype(BF16))

    wrb, wkb, wvb, wob = (w.astype(BF16) for w in (wr, wk, wv, wo))
    w1b, w2b = padk(w1, w2)
    a1b, a2b = padk(a1, a2)
    g1b, g2b = padk(g1, g2)
    prm = jnp.stack([w0, a0, k_k, k_a, r_k.reshape(d), ln_w, ln_b, jnp.zeros((d,), F32)])

    def proj(x, xprev, pair_out):
        mr = lambda i: mix[i:i + 1]
        r = _mm(x, wrb, xprev=xprev, mixrow=mr(0), pair_out=pair_out)
        wl = _mm(_mm(x, w1b, xprev=xprev, mixrow=mr(1), act="tanh"), w2b, pair_out=pair_out)
        k = _mm(x, wkb, xprev=xprev, mixrow=mr(2), pair_out=pair_out)
        v = _mm(x, wvb, xprev=xprev, mixrow=mr(3), pair_out=pair_out)
        al = _mm(_mm(x, a1b, xprev=xprev, mixrow=mr(4)), a2b, pair_out=pair_out)
        gate = _mm(_mm(x, g1b, xprev=xprev, mixrow=mr(5), act="sigmoid"), g2b, pair_out=pair_out)
        return r, k, v, wl, al, gate

    pp = proj(xp, xprev_p, True)
    prm_pair = prm.reshape(8, npair, LANE).transpose(1, 0, 2)
    y_p, s_p = _rwkv_chunk_prompt(*pp, prm_pair, C=_tile(t, (64, 32, 16, 8)), PP=16)
    yp = _mm_ln([y_p], wob, xp, g, b, pair_in=True)

    ps = [a[:, None, :] for a in proj(xs, xprev_s, False)]
    y_s, s_s = _rwkv_sample(*ps, prm, s0_s)
    ys = _mm_ln([y_s[:, 0]], wob, xs, g, b)
    return yp, ys, dict(shift_p=xp[t - 1:t], shift_s=xs, rs_p=s_p[None], rs_s=s_s)


def kernel(x_prompt, x_sample, cache_nsa_cmp, cache_nsa_slc, cache_nsa_win, state_dn_conv, state_dn_S, state_rwkv_shift, state_rwkv_S, page_table, ln_g, ln_b, ffn_wi, ffn_wo, mix_w_in, mix_w_out, nsa_cmp_pe, nsa_cmp_w1, nsa_cmp_b1, nsa_cmp_w2, dn_conv_w, dn_a_log, dn_dt_bias, dn_norm_w, rwkv_mix, rwkv_wr, rwkv_wk, rwkv_wv, rwkv_wo, rwkv_w0, rwkv_w1, rwkv_w2, rwkv_a0, rwkv_a1, rwkv_a2, rwkv_g1, rwkv_g2, rwkv_k_k, rwkv_k_a, rwkv_r_k, rwkv_ln_w, rwkv_ln_b):
    bp, t, d = x_prompt.shape
    assert bp == 1 and x_sample.shape[1] == 1
    depth = ffn_wi.shape[0]
    xp = x_prompt[0]
    xs = x_sample[:, 0]
    even, odd = [], []
    wi = ffn_wi.astype(BF16)
    wo = ffn_wo.astype(BF16)
    for l in range(depth):
        gl = lambda i: (ln_g[l, i][None], ln_b[l, i][None])
        xp = _ffn_ln(xp, wi, wo, l, 0, *gl(0))
        xs = _ffn_ln(xs, wi, wo, l, 0, *gl(0))
        if l % 2 == 0:
            e = l // 2
            xp, xs, o = _even_layer(
                xp, xs, mix_w_in[e], mix_w_out[e], nsa_cmp_pe[e], nsa_cmp_w1[e], nsa_cmp_b1[e], nsa_cmp_w2[e],
                dn_conv_w[e], dn_a_log[e], dn_dt_bias[e], dn_norm_w[e], cache_nsa_cmp[e], cache_nsa_slc[e],
                cache_nsa_win[e], state_dn_conv[e], state_dn_S[e], page_table, *gl(1))
            even.append(o)
        else:
            c = l // 2
            xp, xs, o = _odd_layer(
                xp, xs, state_rwkv_shift[c], state_rwkv_S[c], rwkv_mix[c], rwkv_wr[c], rwkv_wk[c], rwkv_wv[c],
                rwkv_wo[c], rwkv_w0[c], rwkv_w1[c], rwkv_w2[c], rwkv_a0[c], rwkv_a1[c], rwkv_a2[c], rwkv_g1[c],
                rwkv_g2[c], rwkv_k_k[c], rwkv_k_a[c], rwkv_r_k[c], rwkv_ln_w[c], rwkv_ln_b[c], *gl(1))
            odd.append(o)
        xp = _ffn_ln(xp, wi, wo, l, 1, *gl(2))
        xs = _ffn_ln(xs, wi, wo, l, 1, *gl(2))
    st = lambda lst, key: jnp.stack([o[key] for o in lst])
    return (xp[None], xs[:, None],
            st(even, "cmp_p"), st(even, "cmp_s"), st(even, "slc_p"), st(even, "slc_s"),
            st(even, "win_p"), st(even, "win_s"), st(even, "conv_p"), st(even, "conv_s"),
            st(even, "dns_p"), st(even, "dns_s"),
            st(odd, "shift_p"), st(odd, "shift_s"), st(odd, "rs_p"), st(odd, "rs_s"))
```

```python
import functools

import numpy as np
import jax
import jax.numpy as jnp
from jax import lax
from jax.experimental import pallas as pl
from jax.experimental.pallas import tpu as pltpu

F32 = jnp.float32
BF16 = jnp.bfloat16
HI = lax.Precision.HIGHEST

DEPTH = 2
ALPHA = (2 * DEPTH) ** 0.25
LN_EPS = 1e-5
RMS_EPS = 1e-6
NSA_HEADS = 8
NSA_KV = 2
NSA_GROUP = NSA_HEADS // NSA_KV
HD = 128
L_CMP = 32
D_CMP = 16
L_SLC = 64
N_SEL = 16
WINDOW = 512
NEG_INF = -1e30
FORCE_SCORE = 1e6
DN_HEADS = 8
DK = 128
DV = 128
CONV_W = 4
RWKV_HS = 64
GN_EPS = 64e-5

A_Q = NSA_HEADS * HD
A_KVW = 2 * NSA_KV * HD
A_GATES = 3 * NSA_HEADS
DN_QKV = DN_HEADS * (2 * DK + DV)
EVEN_SPLIT = (A_Q, A_KVW, A_KVW, A_KVW, A_GATES, DN_QKV, DN_HEADS * DV, DN_HEADS, DN_HEADS)
COL_QA = 0
COL_DQKV = 1024
COL_Z = 4096
COL_KVC = 5120
COL_KVS = 5632
COL_KVW = 6144
COL_SM = 6656
SM_BETA = A_GATES
SM_A = A_GATES + DN_HEADS
E_IN_PAD = 7168
LANE = 128
PAGE_ROWS = 128

VMEM_LIMIT = 56 * 1024 * 1024


def _cp(sem):
    return pltpu.CompilerParams(dimension_semantics=sem, vmem_limit_bytes=VMEM_LIMIT)


def _tile(n, prefs):
    for t in prefs:
        if n % t == 0:
            return t
    return n


def _dot(a, b, precision=None):
    return jnp.dot(a, b, preferred_element_type=F32, precision=precision)


def _dot_nt(a, b, precision=None):
    return lax.dot_general(a, b, (((1,), (1,)), ((), ())), preferred_element_type=F32, precision=precision)


def _dot_tn(a, b, precision=None):
    return lax.dot_general(a, b, (((0,), (0,)), ((), ())), preferred_element_type=F32, precision=precision)


def _layernorm_rows(y, g, b):
    mu = jnp.mean(y, axis=-1, keepdims=True)
    d = y - mu
    var = jnp.mean(d * d, axis=-1, keepdims=True)
    return d * lax.rsqrt(var + LN_EPS) * g + b


def _softplus(x):
    return jnp.maximum(x, 0.0) + jnp.log1p(jnp.exp(-jnp.abs(x)))


def _silu(x):
    return x * jax.nn.sigmoid(x)


def _ffn_kernel(x_ref, wg_ref, wu_ref, wo_ref, g_ref, b_ref, o_ref, xb_ref, *, nf):
    f = pl.program_id(1)

    @pl.when(f == 0)
    def _():
        xb_ref[...] = x_ref[...].astype(BF16)
        o_ref[...] = jnp.zeros_like(o_ref)

    xb = xb_ref[...]
    gate = _dot(xb, wg_ref[...].astype(BF16))
    up = _dot(xb, wu_ref[...].astype(BF16))
    act = (_silu(gate) * up).astype(BF16)
    o_ref[...] += _dot(act, wo_ref[...].astype(BF16))

    @pl.when(f == nf - 1)
    def _():
        y = ALPHA * x_ref[...] + 0.5 * o_ref[...]
        o_ref[...] = _layernorm_rows(y, g_ref[...], b_ref[...])


def _ffn_ln(x, wi, wo, layer, which, g, b):
    m, d = x.shape
    f = wo.shape[2]
    tm = _tile(m, (1024, 512, 256, 128, 64, 32, 16, 8))
    tf = _tile(f, (256, 128))
    nf = f // tf
    return pl.pallas_call(
        functools.partial(_ffn_kernel, nf=nf),
        grid=(m // tm, nf),
        in_specs=[
            pl.BlockSpec((tm, d), lambda i, j: (i, 0), pipeline_mode=pl.Buffered(1)),
            pl.BlockSpec((None, None, d, tf), lambda i, j: (layer, which, 0, j)),
            pl.BlockSpec((None, None, d, tf), lambda i, j: (layer, which, 0, j + nf)),
            pl.BlockSpec((None, None, tf, d), lambda i, j: (layer, which, j, 0)),
            pl.BlockSpec((1, d), lambda i, j: (0, 0)),
            pl.BlockSpec((1, d), lambda i, j: (0, 0)),
        ],
        out_specs=pl.BlockSpec((tm, d), lambda i, j: (i, 0)),
        out_shape=jax.ShapeDtypeStruct((m, d), F32),
        scratch_shapes=[pltpu.VMEM((tm, d), BF16)],
        compiler_params=_cp(("parallel", "arbitrary")),
        name="ffn_ln",
    )(x, wi, wi, wo, g, b)


def _mm_kernel(*refs, mix, act, pair_out):
    if mix == "array":
        x_ref, xp_ref, m_ref, w_ref, o_ref, xb_ref = refs
    elif mix == "shift":
        x_ref, prev_ref, first_ref, m_ref, w_ref, o_ref, xb_ref, hist_ref = refs
    else:
        x_ref, w_ref, o_ref, xb_ref = refs

    @pl.when(pl.program_id(1) == 0)
    def _():
        x = x_ref[...]
        if mix == "array":
            x = x + (xp_ref[...] - x) * m_ref[...]
        elif mix == "shift":
            tm = x_ref.shape[0]
            hist_ref[0:8, :] = jnp.where(pl.program_id(0) == 0, first_ref[...], prev_ref[...])
            hist_ref[8:8 + tm, :] = x
            x = x + (hist_ref[7:7 + tm, :] - x) * m_ref[...]
        xb_ref[...] = x.astype(BF16)

    y = _dot(xb_ref[...], w_ref[...])
    if act == "tanh":
        y = jnp.tanh(y)
    elif act == "sigmoid":
        y = jax.nn.sigmoid(y)
    if pair_out:
        for p in range(o_ref.shape[0]):
            o_ref[p] = y[:, p * LANE:(p + 1) * LANE]
    else:
        o_ref[...] = y


def _mm(x, w, *, xprev=None, first8=None, mixrow=None, act=None, pair_out=False, tn_prefs=(512, 256, 128)):
    m, k = x.shape
    n = w.shape[1]
    tm = _tile(m, (1024, 512, 256, 128, 64, 32, 16, 8))
    tn = _tile(n, tn_prefs)
    mix = "array" if xprev is not None else ("shift" if first8 is not None else None)
    in_specs = [pl.BlockSpec((tm, k), lambda i, j: (i, 0))]
    args = [x]
    scratch = [pltpu.VMEM((tm, k), BF16)]
    if mix == "array":
        in_specs += [pl.BlockSpec((tm, k), lambda i, j: (i, 0)), pl.BlockSpec((1, k), lambda i, j: (0, 0))]
        args += [xprev, mixrow]
    elif mix == "shift":
        in_specs += [pl.BlockSpec((8, k), lambda i, j: (jnp.maximum(i * (tm // 8) - 1, 0), 0)),
                     pl.BlockSpec((8, k), lambda i, j: (0, 0)), pl.BlockSpec((1, k), lambda i, j: (0, 0))]
        args += [x, first8, mixrow]
        scratch.append(pltpu.VMEM((tm + 8, k), F32))
    in_specs.append(pl.BlockSpec((k, tn), lambda i, j: (0, j)))
    args.append(w)
    if pair_out:
        npb = tn // LANE
        out_spec = pl.BlockSpec((npb, tm, LANE), lambda i, j: (j, i, 0))
        out_shape = jax.ShapeDtypeStruct((n // LANE, m, LANE), F32)
    else:
        out_spec = pl.BlockSpec((tm, tn), lambda i, j: (i, j))
        out_shape = jax.ShapeDtypeStruct((m, n), F32)
    return pl.pallas_call(
        functools.partial(_mm_kernel, mix=mix, act=act, pair_out=pair_out),
        grid=(m // tm, n // tn),
        in_specs=in_specs,
        out_specs=out_spec,
        out_shape=out_shape,
        scratch_shapes=scratch,
        compiler_params=_cp(("parallel", "arbitrary")),
        name="matmul",
    )(*args)


def _mmln_kernel(*refs, pair_in, n_parts):
    a_refs = refs[:n_parts]
    w_ref, x_ref, g_ref, b_ref, o_ref = refs[n_parts:]
    if pair_in:
        a_ref = a_refs[0]
        y = _dot(jnp.concatenate([a_ref[p].astype(BF16) for p in range(a_ref.shape[0])], axis=1), w_ref[...])
    else:
        y = None
        k0 = 0
        for a_ref in a_refs:
            kw = a_ref.shape[1]
            part = _dot(a_ref[...].astype(BF16), w_ref[k0:k0 + kw, :])
            y = part if y is None else y + part
            k0 += kw
    o_ref[...] = _layernorm_rows(ALPHA * x_ref[...] + y, g_ref[...], b_ref[...])


def _mm_ln(a_parts, w, x, g, b, *, pair_in=False):
    m, d = x.shape
    k = w.shape[0]
    tm = _tile(m, (512, 256, 128, 64, 32, 16, 8))
    if pair_in:
        a_specs = [pl.BlockSpec((k // LANE, tm, LANE), lambda i: (0, i, 0))]
    else:
        a_specs = [pl.BlockSpec((tm, a.shape[1]), lambda i: (i, 0)) for a in a_parts]
    return pl.pallas_call(
        functools.partial(_mmln_kernel, pair_in=pair_in, n_parts=len(a_parts)),
        grid=(m // tm,),
        in_specs=a_specs + [
            pl.BlockSpec((k, d), lambda i: (0, 0)),
            pl.BlockSpec((tm, d), lambda i: (i, 0)),
            pl.BlockSpec((1, d), lambda i: (0, 0)),
            pl.BlockSpec((1, d), lambda i: (0, 0)),
        ],
        out_specs=pl.BlockSpec((tm, d), lambda i: (i, 0)),
        out_shape=jax.ShapeDtypeStruct((m, d), F32),
        compiler_params=_cp(("parallel",)),
        name="matmul_ln",
    )(*a_parts, w, x, g, b)


CH_PER_PAGE = PAGE_ROWS // D_CMP
KV_PARTS = 2 * NSA_KV
PAGE_SUBROWS = PAGE_ROWS * KV_PARTS


def _cmp1_kernel(pt_ref, *refs, G):
    del pt_ref
    page_refs = refs[:G]
    w_ref, o_ref = refs[G:]
    rows = G * CH_PER_PAGE
    for sg in range(KV_PARTS):
        s = sg // NSA_KV
        acc = jnp.zeros((rows, 2 * HD), F32)
        for p in range(D_CMP):
            x = jnp.concatenate(
                [page_refs[j][0, pl.ds(p * KV_PARTS + sg, CH_PER_PAGE, stride=D_CMP * KV_PARTS), :]
                 for j in range(G)], axis=0)
            acc = acc + _dot(x.astype(BF16), w_ref[s, p])
        o_ref[0, :, sg * 2 * HD:(sg + 1) * 2 * HD] = acc


def _cmp_stage1(pool, page_table, w4):
    b, n_pages = page_table.shape
    G = _tile(n_pages, (16, 8, 4, 2, 1))

    def page_map(j):
        return lambda bi, p, pt: (pt[bi, p * G + j], 0, 0)

    in_specs = [pl.BlockSpec((1, PAGE_SUBROWS, HD), page_map(j)) for j in range(G)]
    in_specs.append(pl.BlockSpec(w4.shape, lambda bi, p, pt: (0, 0, 0, 0)))
    grid_spec = pltpu.PrefetchScalarGridSpec(
        num_scalar_prefetch=1,
        grid=(b, n_pages // G),
        in_specs=in_specs,
        out_specs=pl.BlockSpec((1, G * CH_PER_PAGE, 8 * HD), lambda bi, p, pt: (bi, p, 0)),
    )
    return pl.pallas_call(
        functools.partial(_cmp1_kernel, G=G),
        grid_spec=grid_spec,
        out_shape=jax.ShapeDtypeStruct((b, n_pages * CH_PER_PAGE, 8 * HD), F32),
        compiler_params=_cp(("parallel", "arbitrary")),
        name="nsa_cmp_stage1",
    )(page_table, *([pool] * G), w4)


def _gelu_tanh(x):
    return 0.5 * x * (1.0 + jnp.tanh(np.sqrt(2.0 / np.pi).astype(np.float32) * (x + 0.044715 * (x * x * x))))


def _cmp2_kernel(h_ref, pe_ref, w4_ref, b1_ref, w2_ref, o_ref):
    nch = h_ref.shape[1]
    for s in range(2):
        pacc = jnp.zeros((8, 2 * HD), F32)
        for p in range(D_CMP):
            pacc = pacc + _dot(pe_ref[s, p].astype(BF16), w4_ref[s, p])
        const = pacc[0:1, 0:HD] + pacc[1:2, HD:2 * HD] + b1_ref[s:s + 1, :]
        for gi in range(NSA_KV):
            sg = s * NSA_KV + gi
            h0 = h_ref[0, :, sg * 2 * HD:sg * 2 * HD + HD]
            h1 = h_ref[0, :, sg * 2 * HD + HD:(sg + 1) * 2 * HD]
            hid = h0 + pltpu.roll(h1, nch - 1, 0) + const
            o_ref[0, :, sg * HD:(sg + 1) * HD] = _dot(_gelu_tanh(hid).astype(BF16), w2_ref[s])


def _cmp_stage2(h, pe8, w4, b1, w2):
    b, nch, _ = h.shape
    return pl.pallas_call(
        _cmp2_kernel,
        grid=(b,),
        in_specs=[
            pl.BlockSpec((1, nch, 8 * HD), lambda i: (i, 0, 0)),
            pl.BlockSpec(pe8.shape, lambda i: (0, 0, 0, 0)),
            pl.BlockSpec(w4.shape, lambda i: (0, 0, 0, 0)),
            pl.BlockSpec(b1.shape, lambda i: (0, 0)),
            pl.BlockSpec(w2.shape, lambda i: (0, 0, 0)),
        ],
        out_specs=pl.BlockSpec((1, nch, A_KVW), lambda i: (i, 0, 0)),
        out_shape=jax.ShapeDtypeStruct((b, nch, A_KVW), F32),
        compiler_params=_cp(("parallel",)),
        name="nsa_cmp_stage2",
    )(h, pe8, w4, b1, w2)


def _nsa_sel_kernel(slopes_ref, q_ref, cmp_ref, ov_ref, ocmp_ref, sel_ref, *, TQ, NC, NS, q_off, n_pick, RS):
    i = pl.program_id(1)
    bb = q_ref.shape[0]
    nch = cmp_ref.shape[1]
    nsp = ov_ref.shape[1]
    qpos = q_off + i * TQ + lax.broadcasted_iota(jnp.int32, (TQ, 1), 0)
    cidx = lax.broadcasted_iota(jnp.int32, (1, nch), 1)
    cstart = cidx * D_CMP
    cmask = jnp.logical_and(cstart + (L_CMP - 1) <= qpos, cidx < NC)
    mask_add = jnp.where(cmask, 0.0, NEG_INF)
    mask_mul = jnp.where(cmask, 1.0, 0.0)
    cdist = (qpos - cstart).astype(F32) - 0.5 * (L_CMP - 1)
    sid = lax.broadcasted_iota(jnp.int32, (1, nsp), 1)
    cur = lax.shift_right_arithmetic(qpos, int(np.log2(L_SLC)))
    svalid = jnp.logical_and(sid * L_SLC <= qpos, sid < NS)
    forced = jnp.logical_or(sid == 0, jnp.logical_or(sid == cur, sid == cur - 1))
    ov = ov_ref[...]
    scores = []
    for bi in range(bb):
        q = q_ref[bi]
        cm = cmp_ref[bi]
        for g in range(NSA_KV):
            kb = cm[:, g * HD:(g + 1) * HD].astype(BF16)
            vb = cm[:, (NSA_KV + g) * HD:(NSA_KV + g + 1) * HD].astype(BF16)
            pcs = jnp.zeros((TQ, nch), F32)
            for j in range(NSA_GROUP):
                h = NSA_GROUP * g + j
                qj = (q[:, h * HD:(h + 1) * HD] * HD ** -0.5).astype(BF16)
                lg = _dot_nt(qj, kb) - slopes_ref[h] * cdist + mask_add
                e = jnp.exp(lg - jnp.max(lg, axis=-1, keepdims=True))
                p = e * (mask_mul * (1.0 / jnp.sum(e, axis=-1, keepdims=True)))
                pcs = pcs + p
                ocmp_ref[bi, :, h * HD:(h + 1) * HD] = _dot(p.astype(BF16), vb)
            imp = _dot(pcs, ov, precision=HI)
            scores.append(jnp.where(svalid, jnp.where(forced, FORCE_SCORE, imp), NEG_INF))
    sidb = jnp.broadcast_to(sid, (RS, nsp))
    validb = [jnp.logical_and(sid * L_SLC <= qpos[r0:r0 + RS], sid < NS) for _ in scores for r0 in range(0, TQ, RS)]
    chains = [sc[r0:r0 + RS] for sc in scores for r0 in range(0, TQ, RS)]
    sels = [jnp.zeros((RS, nsp), F32) for _ in chains]
    for _ in range(n_pick):
        firsts = [jnp.argmax(sc, axis=-1, keepdims=True).astype(jnp.int32) for sc in chains]
        picks = [sidb == f for f in firsts]
        sels = [jnp.where(jnp.logical_and(pk, vb), 1.0, sl) for pk, vb, sl in zip(picks, validb, sels)]
        chains = [jnp.where(pk, -jnp.inf, sc) for pk, sc in zip(picks, chains)]
    nsub = TQ // RS
    for bi in range(bb):
        for g in range(NSA_KV):
            for r in range(nsub):
                sel_ref[bi, g, r * RS:(r + 1) * RS, :] = sels[(bi * NSA_KV + g) * nsub + r]


def _nsa_select(q, cmp, overlap, slopes, *, TQ, NC, NS, q_off, BB):
    b, t, _ = q.shape
    nch = cmp.shape[1]
    nsp = overlap.shape[1]
    kern = functools.partial(_nsa_sel_kernel, TQ=TQ, NC=NC, NS=NS, q_off=q_off, n_pick=min(N_SEL, NS),
                             RS=_tile(TQ, (32, 16, 8)))
    return pl.pallas_call(
        kern,
        grid=(b // BB, t // TQ),
        in_specs=[
            pl.BlockSpec(memory_space=pltpu.SMEM),
            pl.BlockSpec((BB, TQ, A_Q), lambda bi, i: (bi, i, 0)),
            pl.BlockSpec((BB, nch, A_KVW), lambda bi, i: (bi, 0, 0)),
            pl.BlockSpec((nch, nsp), lambda bi, i: (0, 0)),
        ],
        out_specs=[
            pl.BlockSpec((BB, TQ, A_Q), lambda bi, i: (bi, i, 0)),
            pl.BlockSpec((BB, NSA_KV, TQ, nsp), lambda bi, i: (bi, 0, i, 0)),
        ],
        out_shape=[jax.ShapeDtypeStruct((b, t, A_Q), F32), jax.ShapeDtypeStruct((b, NSA_KV, t, nsp), F32)],
        compiler_params=_cp(("parallel", "parallel")),
        name="nsa_cmp_select",
    )(slopes, q, cmp, overlap)


def _stack_heads(q, tq):
    del tq
    return jnp.concatenate([q[:, j * HD:(j + 1) * HD] for j in range(NSA_GROUP)], axis=0)


def _online_step(carry, s, dist, mask_add, slopes, v):
    m, l, acc = carry
    lg = s + jnp.concatenate([mask_add - sl * dist for sl in slopes], axis=0)
    m_new = jnp.maximum(m, jnp.max(lg, axis=-1, keepdims=True))
    p = jnp.exp(lg - m_new)
    a = jnp.exp(m - m_new)
    l = a * l + jnp.sum(p, axis=-1, keepdims=True)
    acc = a * acc + _dot(p.astype(BF16), v)
    return m_new, l, acc


def _nsa_slc_kernel(slopes_ref, q_ref, sel_ref, k_ref, v_ref, o_ref, *, TQ, TK):
    g = pl.program_id(0)
    i = pl.program_id(1)
    nsp = sel_ref.shape[-1]
    rows = NSA_GROUP * TQ
    q4 = (_stack_heads(q_ref[...], TQ) * HD ** -0.5).astype(BF16)
    qposf = (i * TQ + lax.broadcasted_iota(jnp.int32, (TQ, 1), 0)).astype(F32)
    slopes = [slopes_ref[NSA_GROUP * g + j] for j in range(NSA_GROUP)]
    selb = sel_ref[0].astype(BF16)
    blk_per_tile = TK // L_SLC
    delta = (lax.broadcasted_iota(jnp.int32, (nsp, TK), 0)
             - lax.shift_right_arithmetic(lax.broadcasted_iota(jnp.int32, (nsp, TK), 1), int(np.log2(L_SLC))))
    kcol = lax.broadcasted_iota(jnp.int32, (1, TK), 1)

    def body(kt, carry):
        k0 = pl.multiple_of(kt * TK, TK)
        k = k_ref[pl.ds(k0, TK), :]
        v = v_ref[pl.ds(k0, TK), :]
        s = _dot_nt(q4, k)
        dist = qposf - (k0 + kcol).astype(F32)
        expand = jnp.where(delta == kt * blk_per_tile, 1.0, 0.0).astype(BF16)
        se = _dot(selb, expand)
        mask_add = jnp.where(jnp.logical_and(se > 0.5, dist >= 0.0), 0.0, NEG_INF)
        return _online_step(carry, s, dist, mask_add, slopes, v)

    ntile = (i * TQ + TQ + TK - 1) // TK
    init = (jnp.full((rows, 1), NEG_INF, F32), jnp.zeros((rows, 1), F32), jnp.zeros((rows, HD), F32))
    _, l, acc = lax.fori_loop(0, ntile, body, init)
    o = acc / l
    for j in range(NSA_GROUP):
        o_ref[:, j * HD:(j + 1) * HD] = o[j * TQ:(j + 1) * TQ]


def _nsa_slc_prompt(q, sel, kvb, slopes, *, TQ, TK):
    t = q.shape[0]
    nsp = sel.shape[-1]
    kb0 = A_KVW // HD
    return pl.pallas_call(
        functools.partial(_nsa_slc_kernel, TQ=TQ, TK=TK),
        grid=(NSA_KV, t // TQ),
        in_specs=[
            pl.BlockSpec(memory_space=pltpu.SMEM),
            pl.BlockSpec((TQ, NSA_GROUP * HD), lambda g, i: (i, g)),
            pl.BlockSpec((1, TQ, nsp), lambda g, i: (g, i, 0)),
            pl.BlockSpec((t, HD), lambda g, i: (0, kb0 + g)),
            pl.BlockSpec((t, HD), lambda g, i: (0, kb0 + NSA_KV + g)),
        ],
        out_specs=pl.BlockSpec((TQ, NSA_GROUP * HD), lambda g, i: (i, g)),
        out_shape=jax.ShapeDtypeStruct((t, A_Q), F32),
        compiler_params=_cp(("parallel", "parallel")),
        name="nsa_slc_prompt",
    )(slopes, q, sel, kvb, kvb)


def _nsa_win_kernel(slopes_ref, q_ref, k_ref, v_ref, ga_ref, ocmp_ref, oslc_ref, o_ref, *, TQ):
    g = pl.program_id(0)
    i = pl.program_id(1)
    rows = NSA_GROUP * TQ
    q4 = (_stack_heads(q_ref[...], TQ) * HD ** -0.5).astype(BF16)
    qposf = (i * TQ + lax.broadcasted_iota(jnp.int32, (TQ, 1), 0)).astype(F32)
    slopes = [slopes_ref[NSA_GROUP * g + j] for j in range(NSA_GROUP)]
    kcol = lax.broadcasted_iota(jnp.int32, (1, TQ), 1)

    def body(kt, carry):
        k0 = pl.multiple_of(kt * TQ, TQ)
        k = k_ref[pl.ds(k0, TQ), :]
        v = v_ref[pl.ds(k0, TQ), :]
        s = _dot_nt(q4, k)
        dist = qposf - (k0 + kcol).astype(F32)
        mask_add = jnp.where(jnp.logical_and(dist >= 0.0, dist <= float(WINDOW)), 0.0, NEG_INF)
        return _online_step(carry, s, dist, mask_add, slopes, v)

    lo = jnp.maximum(i - WINDOW // TQ, 0)
    init = (jnp.full((rows, 1), NEG_INF, F32), jnp.zeros((rows, 1), F32), jnp.zeros((rows, HD), F32))
    _, l, acc = lax.fori_loop(lo, i + 1, body, init)
    o_win = acc / l
    gates = jax.nn.sigmoid(ga_ref[...])
    for j in range(NSA_GROUP):
        ca = 3 * j
        cb = 3 * (NSA_GROUP + j)

        def gate(c):
            return jnp.where(g == 0, gates[:, ca + c:ca + c + 1], gates[:, cb + c:cb + c + 1])

        sl = slice(j * HD, (j + 1) * HD)
        o_ref[:, sl] = (gate(0) * ocmp_ref[:, sl] + gate(1) * oslc_ref[:, sl]
                        + gate(2) * o_win[j * TQ:(j + 1) * TQ])


def _nsa_win_prompt(q, kvb, h_in, o_cmp, o_slc, slopes, *, TQ):
    t = q.shape[0]
    kb0 = 2 * A_KVW // HD
    hspec = pl.BlockSpec((TQ, NSA_GROUP * HD), lambda g, i: (i, g))
    return pl.pallas_call(
        functools.partial(_nsa_win_kernel, TQ=TQ),
        grid=(NSA_KV, t // TQ),
        in_specs=[
            pl.BlockSpec(memory_space=pltpu.SMEM),
            hspec,
            pl.BlockSpec((t, HD), lambda g, i: (0, kb0 + g)),
            pl.BlockSpec((t, HD), lambda g, i: (0, kb0 + NSA_KV + g)),
            pl.BlockSpec((TQ, LANE), lambda g, i: (i, COL_SM // LANE)),
            hspec,
            hspec,
        ],
        out_specs=hspec,
        out_shape=jax.ShapeDtypeStruct((t, A_Q), F32),
        compiler_params=_cp(("parallel", "parallel")),
        name="nsa_win_prompt",
    )(slopes, q, kvb, kvb, h_in, o_cmp, o_slc)


def _rows8(row, width):
    return jnp.concatenate([row[:, h * width:(h + 1) * width] for h in range(NSA_HEADS)], axis=0)


def _kv_rows8(kn, off):
    return jnp.concatenate(
        [kn[:, off + (h // NSA_GROUP) * HD: off + (h // NSA_GROUP + 1) * HD] for h in range(NSA_HEADS)], axis=0)


def _slope8(slopes_ref):
    hrow = lax.broadcasted_iota(jnp.int32, (NSA_HEADS, 1), 0)
    out = jnp.zeros((NSA_HEADS, 1), F32)
    for h in range(NSA_HEADS):
        out = jnp.where(hrow == h, slopes_ref[h], out)
    return out


def _bf(x):
    return x.astype(BF16).astype(F32)


def _nsa_slc_dec_kernel(pt_ref, slopes_ref, *refs, G, past):
    del pt_ref
    pages = refs[:G]
    q_ref, sel_ref, knew_ref, o_ref, kv_s, m_s, l_s, acc_s = refs[G:]
    pg = pl.program_id(1)
    nsp = sel_ref.shape[-1]
    tk = G * PAGE_ROWS
    q8 = _rows8(q_ref[0], HD) * HD ** -0.5
    q8b = q8.astype(BF16)
    hrow = lax.broadcasted_iota(jnp.int32, (NSA_HEADS, 1), 0)
    first_group = hrow < NSA_GROUP

    @pl.when(pg == 0)
    def _():
        kn = knew_ref[0]
        m_s[...] = jnp.sum(_bf(q8) * _bf(_kv_rows8(kn, 0)), axis=-1, keepdims=True)
        l_s[...] = jnp.ones_like(l_s)
        acc_s[...] = _bf(_kv_rows8(kn, NSA_KV * HD))

    for j in range(G):
        for sg in range(KV_PARTS):
            kv_s[sg, j * PAGE_ROWS:(j + 1) * PAGE_ROWS, :] = pages[j][0, pl.ds(sg, PAGE_ROWS, stride=KV_PARTS), :]
    kpos = pg * tk + lax.broadcasted_iota(jnp.int32, (1, tk), 1)
    dist = (past - kpos).astype(F32)
    srow = lax.broadcasted_iota(jnp.int32, (nsp, tk), 0)
    scol = (pg * tk + lax.broadcasted_iota(jnp.int32, (nsp, tk), 1)) // L_SLC
    expand = jnp.where(srow == scol, 1.0, 0.0).astype(BF16)
    sel2 = sel_ref[0]
    sel8 = jnp.where(first_group, sel2[0:1, :], sel2[1:2, :]).astype(BF16)
    mask = _dot(sel8, expand) > 0.5
    s8 = jnp.where(first_group, _dot_nt(q8b, kv_s[0].astype(BF16)), _dot_nt(q8b, kv_s[1].astype(BF16)))
    lg = jnp.where(mask, s8 - _slope8(slopes_ref) * dist, NEG_INF)
    m = m_s[...]
    m_new = jnp.maximum(m, jnp.max(lg, axis=-1, keepdims=True))
    p = jnp.where(mask, jnp.exp(lg - m_new), 0.0)
    a = jnp.exp(m - m_new)
    pb = p.astype(BF16)
    pv = jnp.where(first_group, _dot(pb, kv_s[2].astype(BF16)), _dot(pb, kv_s[3].astype(BF16)))
    m_s[...] = m_new
    l_s[...] = a * l_s[...] + jnp.sum(p, axis=-1, keepdims=True)
    acc_s[...] = a * acc_s[...] + pv

    @pl.when(pg == pl.num_programs(1) - 1)
    def _():
        o = acc_s[...] / l_s[...]
        for h in range(NSA_HEADS):
            o_ref[0, :, h * HD:(h + 1) * HD] = o[h:h + 1, :]


def _nsa_slc_sample(q, sel, knew, pool, page_table, slopes):
    b, n_pages = page_table.shape
    G = _tile(n_pages, (8, 4, 2, 1))
    nsp = sel.shape[-1]
    past = n_pages * PAGE_ROWS

    def page_map(j):
        return lambda bi, p, pt: (pt[bi, p * G + j], 0, 0)

    in_specs = [pl.BlockSpec(memory_space=pltpu.SMEM)]
    in_specs += [pl.BlockSpec((1, PAGE_SUBROWS, HD), page_map(j)) for j in range(G)]
    in_specs += [
        pl.BlockSpec((1, 1, A_Q), lambda bi, p, pt: (bi, 0, 0)),
        pl.BlockSpec((1, NSA_KV, nsp), lambda bi, p, pt: (bi, 0, 0)),
        pl.BlockSpec((1, 1, A_KVW), lambda bi, p, pt: (bi, 0, 0)),
    ]
    grid_spec = pltpu.PrefetchScalarGridSpec(
        num_scalar_prefetch=1,
        grid=(b, n_pages // G),
        in_specs=in_specs,
        out_specs=pl.BlockSpec((1, 1, A_Q), lambda bi, p, pt: (bi, 0, 0)),
        scratch_shapes=[pltpu.VMEM((KV_PARTS, G * PAGE_ROWS, HD), F32), pltpu.VMEM((NSA_HEADS, 1), F32),
                        pltpu.VMEM((NSA_HEADS, 1), F32), pltpu.VMEM((NSA_HEADS, HD), F32)],
    )
    return pl.pallas_call(
        functools.partial(_nsa_slc_dec_kernel, G=G, past=past),
        grid_spec=grid_spec,
        out_shape=jax.ShapeDtypeStruct((b, 1, A_Q), F32),
        compiler_params=_cp(("parallel", "arbitrary")),
        name="nsa_slc_sample",
    )(page_table, slopes, *([pool] * G), q, sel, knew)


def _nsa_win_dec_kernel(slopes_ref, q_ref, wb_ref, knew_ref, ga_ref, ocmp_ref, oslc_ref, o_ref, *, past):
    nb = wb_ref.shape[1] // KV_PARTS
    q8 = _rows8(q_ref[0], HD) * HD ** -0.5
    q8b = q8.astype(BF16)
    hrow = lax.broadcasted_iota(jnp.int32, (NSA_HEADS, 1), 0)
    first_group = hrow < NSA_GROUP
    kn = knew_ref[0]
    wb = [wb_ref[0, pl.ds(sg, nb, stride=KV_PARTS), :].astype(BF16) for sg in range(KV_PARTS)]
    kwpos = past - nb + lax.broadcasted_iota(jnp.int32, (1, nb), 1)
    wd = past - kwpos
    mask = jnp.logical_and(jnp.logical_and(wd >= 0, wd <= WINDOW), kwpos >= 0)
    s8 = jnp.where(first_group, _dot_nt(q8b, wb[0]), _dot_nt(q8b, wb[1]))
    lg = jnp.where(mask, s8 - _slope8(slopes_ref) * wd.astype(F32), NEG_INF)
    s_self = jnp.sum(_bf(q8) * _bf(_kv_rows8(kn, 0)), axis=-1, keepdims=True)
    m = jnp.maximum(jnp.max(lg, axis=-1, keepdims=True), s_self)
    p = jnp.where(mask, jnp.exp(lg - m), 0.0)
    p_self = jnp.exp(s_self - m)
    pb = p.astype(BF16)
    pv = jnp.where(first_group, _dot(pb, wb[2]), _dot(pb, wb[3]))
    pv = pv + _bf(p_self) * _bf(_kv_rows8(kn, NSA_KV * HD))
    o_win = pv / (jnp.sum(p, axis=-1, keepdims=True) + p_self)
    gates = jax.nn.sigmoid(ga_ref[0])
    ocmp = ocmp_ref[0]
    oslc = oslc_ref[0]
    for h in range(NSA_HEADS):
        sl = slice(h * HD, (h + 1) * HD)
        o_ref[0, :, sl] = (gates[:, 3 * h:3 * h + 1] * ocmp[:, sl] + gates[:, 3 * h + 1:3 * h + 2] * oslc[:, sl]
                           + gates[:, 3 * h + 2:3 * h + 3] * o_win[h:h + 1, :])


def _nsa_win_sample(q, win_buf, knew, ga, o_cmp, o_slc, slopes, *, past):
    b = q.shape[0]
    nsub = win_buf.shape[1]
    row = lambda w: pl.BlockSpec((1, 1, w), lambda bi: (bi, 0, 0))
    return pl.pallas_call(
        functools.partial(_nsa_win_dec_kernel, past=past),
        grid=(b,),
        in_specs=[pl.BlockSpec(memory_space=pltpu.SMEM), row(A_Q),
                  pl.BlockSpec((1, nsub, HD), lambda bi: (bi, 0, 0)), row(A_KVW), row(LANE), row(A_Q), row(A_Q)],
        out_specs=row(A_Q),
        out_shape=jax.ShapeDtypeStruct((b, 1, A_Q), F32),
        compiler_params=_cp(("parallel",)),
        name="nsa_win_sample",
    )(slopes, q, win_buf, knew, ga, o_cmp, o_slc)


def _dn_conv_kernel(x_ref, prev_ref, buf_ref, w_ref, o_ref, hist_ref):
    c = pl.program_id(0)
    t = pl.program_id(1)
    tt = x_ref.shape[0]
    hist_ref[0:8, :] = jnp.where(t == 0, buf_ref[...], prev_ref[...])
    hist_ref[8:8 + tt, :] = x_ref[...]
    w = w_ref[...]
    y = jnp.zeros((tt, LANE), F32)
    for i in range(CONV_W):
        y = y + w[i:i + 1, :] * hist_ref[8 - (CONV_W - 1) + i: 8 - (CONV_W - 1) + i + tt, :]
    y = _silu(y)
    nrm = y * lax.rsqrt(jnp.sum(y * y, axis=-1, keepdims=True) + 1e-6)
    o_ref[...] = jnp.where(c < DN_HEADS, nrm * DK ** -0.5, jnp.where(c < 2 * DN_HEADS, nrm, y))


def _dn_conv_prompt(h_in, buf8, conv_wt):
    t = h_in.shape[0]
    tt = _tile(t, (1024, 512, 256, 128, 64, 32, 16, 8))
    c0 = COL_DQKV // LANE
    return pl.pallas_call(
        _dn_conv_kernel,
        grid=(DN_QKV // LANE, t // tt),
        in_specs=[
            pl.BlockSpec((tt, LANE), lambda c, i: (i, c0 + c)),
            pl.BlockSpec((8, LANE), lambda c, i: (jnp.maximum(i * (tt // 8) - 1, 0), c0 + c)),
            pl.BlockSpec((8, LANE), lambda c, i: (0, c)),
            pl.BlockSpec((CONV_W, LANE), lambda c, i: (0, c)),
        ],
        out_specs=pl.BlockSpec((tt, LANE), lambda c, i: (i, c)),
        out_shape=jax.ShapeDtypeStruct((t, DN_QKV), F32),
        scratch_shapes=[pltpu.VMEM((tt + 8, LANE), F32)],
        compiler_params=_cp(("parallel", "parallel")),
        name="dn_conv",
    )(h_in, h_in, buf8, conv_wt)


def _dot1(x, y):
    return _dot(x.astype(BF16), y.astype(BF16))


def _unit_lower_inverse_many(mats, n, mm):
    ri = lax.broadcasted_iota(jnp.int32, (n, n), 0)
    ci = lax.broadcasted_iota(jnp.int32, (n, n), 1)
    eye = jnp.where(ri == ci, 1.0, 0.0).astype(F32)
    base = min(16, n)

    def same_block(b):
        return (ri // b) == (ci // b)

    ps = [jnp.where(same_block(base), -a, 0.0) for a in mats]
    rs = [eye + p for p in ps]
    for _ in range(int(np.log2(base)) - 1):
        ps = [mm(p, p) for p in ps]
        rs = [r + mm(r, p) for r, p in zip(rs, ps)]
    b = base
    while b < n:
        offm = jnp.logical_and(same_block(2 * b), jnp.logical_not(same_block(b)))
        ts = [mm(jnp.where(offm, a, 0.0), r) for a, r in zip(mats, rs)]
        rs = [r - mm(r, t) for r, t in zip(rs, ts)]
        b *= 2
    return rs


def _dn_chunk_kernel(q_ref, k_ref, v_ref, sm_ref, smt_ref, z_ref, prow_ref, pcol_ref, nw_ref,
                     o_ref, s_out_ref, s_ref):
    c = pl.program_id(0)
    C = q_ref.shape[0]

    @pl.when(c == 0)
    def _():
        s_ref[...] = jnp.zeros_like(s_ref)

    ri = lax.broadcasted_iota(jnp.int32, (C, C), 0)
    ci = lax.broadcasted_iota(jnp.int32, (C, C), 1)
    lower = ri >= ci
    tril = jnp.where(lower, 1.0, 0.0).astype(F32)
    triu = jnp.where(ri <= ci, 1.0, 0.0).astype(F32)
    ri2 = lax.broadcasted_iota(jnp.int32, (2 * C, C), 0)
    ci2 = lax.broadcasted_iota(jnp.int32, (2 * C, C), 1)
    mask2 = jnp.where(ri2 < C, ri2, ri2 - (C - 1)) > ci2
    sm = sm_ref[...]
    smt = smt_ref[...]
    g_cols = -jnp.exp(prow_ref[0:1, :]) * _softplus(sm + prow_ref[1:2, :])
    g_rows = -jnp.exp(pcol_ref[:, 0:1]) * _softplus(smt + pcol_ref[:, 1:2])
    gcum_cols = _dot(tril, g_cols, HI)
    gcum_rows = _dot(g_rows, triu, HI)
    beta_cols = jax.nn.sigmoid(sm)
    nw = nw_ref[...]
    heads = []
    for h in range(DN_HEADS):
        sl = slice(h * DK, (h + 1) * DK)
        q = q_ref[:, sl]
        k = k_ref[:, sl]
        gc = gcum_cols[:, SM_A + h:SM_A + h + 1]
        gr = gcum_rows[SM_A + h:SM_A + h + 1, :]
        beta = beta_cols[:, SM_BETA + h:SM_BETA + h + 1]
        decay = jnp.where(lower, jnp.exp(jnp.where(lower, gc - gr, 0.0)), 0.0)
        kb = k * beta
        egc = jnp.exp(gc)
        gl = gc[C - 1:C, :]
        heads.append(dict(
            h=h, sl=sl, kb16=k.astype(BF16), decay2=jnp.concatenate([decay, decay], axis=0),
            kbq=jnp.concatenate([kb, q], axis=0).astype(BF16),
            rhs=jnp.concatenate([v_ref[:, sl] * beta, kb * egc], axis=1).astype(BF16),
            qg=q * egc, kdec=(k * jnp.exp(gl - gc)).astype(BF16), egl=jnp.exp(gl)))
    for d in heads:
        d["aa"] = jnp.where(mask2, _dot_nt(d["kbq"], d["kb16"]) * d["decay2"], 0.0)
    tinv = _unit_lower_inverse_many([d["aa"][:C] for d in heads], C, _dot1)
    for d, ti in zip(heads, tinv):
        d["sol"] = _dot(ti.astype(BF16), d["rhs"])
    for d in heads:
        d["s"] = s_ref[d["h"]]
        d["ks2"] = _dot1(jnp.concatenate([d["sol"][:, DV:], d["qg"]], axis=0), d["s"])
    for d in heads:
        d["v_new"] = (d["sol"][:, :DV] - d["ks2"][:C]).astype(BF16)
    for d in heads:
        s_ref[d["h"]] = d["s"] * d["egl"] + _dot_tn(d["kdec"], d["v_new"])
    for d in heads:
        o = d["ks2"][C:] + _dot(d["aa"][C:].astype(BF16), d["v_new"])
        o = o * lax.rsqrt(jnp.mean(o * o, axis=-1, keepdims=True) + RMS_EPS) * nw
        o_ref[:, d["sl"]] = o * _silu(z_ref[:, d["sl"]])

    @pl.when(c == pl.num_programs(0) - 1)
    def _():
        s_out_ref[...] = s_ref[...]


def _dn_chunk_prompt(qkvn, h_in, smt, prow, pcol, norm_w, *, C):
    t = qkvn.shape[0]
    hw = DN_HEADS * DK
    return pl.pallas_call(
        _dn_chunk_kernel,
        grid=(t // C,),
        in_specs=[
            pl.BlockSpec((C, hw), lambda c: (c, 0)),
            pl.BlockSpec((C, hw), lambda c: (c, 1)),
            pl.BlockSpec((C, hw), lambda c: (c, 2)),
            pl.BlockSpec((C, LANE), lambda c: (c, COL_SM // LANE)),
            pl.BlockSpec((LANE, C), lambda c: (0, c)),
            pl.BlockSpec((C, hw), lambda c: (c, COL_Z // hw)),
            pl.BlockSpec((2, LANE), lambda c: (0, 0)),
            pl.BlockSpec((LANE, 2), lambda c: (0, 0)),
            pl.BlockSpec((1, DV), lambda c: (0, 0)),
        ],
        out_specs=[pl.BlockSpec((C, hw), lambda c: (c, 0)),
                   pl.BlockSpec((DN_HEADS, DK, DV), lambda c: (0, 0, 0))],
        out_shape=[jax.ShapeDtypeStruct((t, hw), F32), jax.ShapeDtypeStruct((DN_HEADS, DK, DV), F32)],
        scratch_shapes=[pltpu.VMEM((DN_HEADS, DK, DV), F32)],
        compiler_params=_cp(("arbitrary",)),
        name="dn_chunk",
    )(qkvn, qkvn, qkvn, h_in, smt, h_in, prow, pcol, norm_w)


def _row_to_col(row, n):
    ri = lax.broadcasted_iota(jnp.int32, (n, n), 0)
    ci = lax.broadcasted_iota(jnp.int32, (n, n), 1)
    return jnp.sum(jnp.where(ri == ci, jnp.broadcast_to(row, (n, n)), 0.0), axis=1, keepdims=True)


def _col_to_row(col, n):
    ri = lax.broadcasted_iota(jnp.int32, (n, n), 0)
    ci = lax.broadcasted_iota(jnp.int32, (n, n), 1)
    return jnp.sum(jnp.where(ri == ci, jnp.broadcast_to(col, (n, n)), 0.0), axis=0, keepdims=True)


def _dn_dec_kernel(buf_ref, xq_ref, xk_ref, xv_ref, w_ref, sm_ref, z_ref, prow_ref, nw_ref, s0_ref,
                   o_ref, s_out_ref):
    hw = DN_HEADS * DK
    buf = buf_ref[0]
    w = w_ref[...]
    sm = sm_ref[0]
    g_row = -jnp.exp(prow_ref[0:1, :]) * _softplus(sm + prow_ref[1:2, :])
    beta_row = jax.nn.sigmoid(sm)
    nw = nw_ref[...]
    z = z_ref[0]
    parts = []
    for part, x_ref in enumerate((xq_ref, xk_ref, xv_ref)):
        sl = slice(part * hw, (part + 1) * hw)
        y = w[CONV_W - 1:CONV_W, sl] * x_ref[0]
        for i in range(CONV_W - 1):
            y = y + w[i:i + 1, sl] * buf[i:i + 1, sl]
        parts.append(_silu(y))
    for h in range(DN_HEADS):
        sl = slice(h * DK, (h + 1) * DK)
        q = parts[0][:, sl]
        k = parts[1][:, sl]
        v = parts[2][:, sl]
        q = q * lax.rsqrt(jnp.sum(q * q, axis=-1, keepdims=True) + 1e-6) * DK ** -0.5
        k = k * lax.rsqrt(jnp.sum(k * k, axis=-1, keepdims=True) + 1e-6)
        a = jnp.exp(g_row[:, SM_A + h:SM_A + h + 1])
        beta = beta_row[:, SM_BETA + h:SM_BETA + h + 1]
        k_col = _row_to_col(k, DK)
        q_col = _row_to_col(q, DK)
        s0 = s0_ref[0, h]
        u = beta * (v - a * jnp.sum(s0 * k_col, axis=0, keepdims=True))
        s_new = a * s0 + k_col * u
        s_out_ref[0, h] = s_new
        o = jnp.sum(s_new * q_col, axis=0, keepdims=True)
        o = o * lax.rsqrt(jnp.mean(o * o, axis=-1, keepdims=True) + RMS_EPS) * nw
        o_ref[0, :, sl] = o * _silu(z[:, sl])


def _dn_sample(conv_buf, h3, conv_wt, prow, norm_w, s0):
    b = h3.shape[0]
    hw = DN_HEADS * DK
    c0 = COL_DQKV // hw
    row = lambda w, j: pl.BlockSpec((1, 1, w), lambda bi: (bi, 0, j))
    return pl.pallas_call(
        _dn_dec_kernel,
        grid=(b,),
        in_specs=[
            pl.BlockSpec((1, CONV_W - 1, DN_QKV), lambda bi: (bi, 0, 0)),
            row(hw, c0), row(hw, c0 + 1), row(hw, c0 + 2),
            pl.BlockSpec((CONV_W, DN_QKV), lambda bi: (0, 0)),
            row(LANE, COL_SM // LANE),
            row(hw, COL_Z // hw),
            pl.BlockSpec((2, LANE), lambda bi: (0, 0)),
            pl.BlockSpec((1, DV), lambda bi: (0, 0)),
            pl.BlockSpec((1, DN_HEADS, DK, DV), lambda bi: (bi, 0, 0, 0)),
        ],
        out_specs=[row(hw, 0), pl.BlockSpec((1, DN_HEADS, DK, DV), lambda bi: (bi, 0, 0, 0))],
        out_shape=[jax.ShapeDtypeStruct((b, 1, hw), F32), jax.ShapeDtypeStruct(s0.shape, F32)],
        compiler_params=_cp(("parallel",)),
        name="dn_sample",
    )(conv_buf, h3, h3, h3, conv_wt, h3, h3, prow, norm_w, s0)


def _rwkv_prep(r, k, wl, al, w0, a0, k_k, k_a):
    w_log = -_softplus(-(w0 + wl)) - 0.5
    log_decay = -jnp.exp(w_log)
    a = jax.nn.sigmoid(a0 + al)
    kk_raw = k * k_k
    k_h = k * (1.0 + (a - 1.0) * k_a)
    del r
    return log_decay, a, kk_raw, k_h


def _rwkv_chunk_kernel(r_ref, k_ref, v_ref, wl_ref, al_ref, gate_ref, prm_ref, o_ref, s_out_ref, s_ref):
    c = pl.program_id(1)
    PP = r_ref.shape[0]
    C = r_ref.shape[1]
    N = RWKV_HS
    hpp = LANE // N

    @pl.when(c == 0)
    def _():
        s_ref[...] = jnp.zeros_like(s_ref)

    ri = lax.broadcasted_iota(jnp.int32, (2 * C, C), 0)
    ci = lax.broadcasted_iota(jnp.int32, (2 * C, C), 1)
    mask2 = jnp.where(ri < C, ri, ri - (C - 1)) > ci
    trilb = jnp.where(lax.broadcasted_iota(jnp.int32, (C, C), 0) >= lax.broadcasted_iota(jnp.int32, (C, C), 1),
                      1.0, 0.0).astype(BF16)
    heads = []
    for pp in range(PP):
        prm = prm_ref[pp]
        r2 = r_ref[pp]
        v2 = v_ref[pp]
        log_decay2, a2, kk_raw2, kh2 = _rwkv_prep(r2, k_ref[pp], wl_ref[pp], al_ref[pp],
                                                  prm[0:1], prm[1:2], prm[2:3], prm[3:4])
        ld_h = log_decay2.astype(BF16)
        rem = log_decay2 - ld_h.astype(F32)
        ld_m = rem.astype(BF16)
        ld_l = (rem - ld_m.astype(F32)).astype(BF16)
        gcum2 = _dot(trilb, ld_h) + (_dot(trilb, ld_m) + _dot(trilb, ld_l))
        for hh in range(hpp):
            sl = slice(hh * N, (hh + 1) * N)
            r = r2[:, sl]
            v = v2[:, sl]
            k_h = kh2[:, sl]
            kk = kk_raw2[:, sl]
            kk = kk / jnp.maximum(jnp.sqrt(jnp.sum(kk * kk, axis=-1, keepdims=True)), 1e-12)
            gc = gcum2[:, sl]
            p_incl = jnp.exp(gc)
            p_inv = jnp.exp(-gc)
            at = -kk * jnp.exp(gc - log_decay2[:, sl])
            bt = kk * a2[:, sl] * p_inv
            kt = k_h * p_inv
            rt = r * p_incl
            heads.append(dict(
                idx=pp * hpp + hh, v=v, p_last=p_incl[C - 1:C, :],
                lhs=jnp.concatenate([at, rt], axis=0).astype(BF16),
                bk=jnp.concatenate([bt, kt], axis=0).astype(BF16),
                bonus=jnp.sum(r * k_h * prm[4:5, sl], axis=-1, keepdims=True) * v,
                ln_w=prm[5:6, sl], ln_b=prm[6:7, sl]))
    for h in heads:
        h["g_b"] = jnp.where(mask2, _dot_nt(h["lhs"], h["bk"][:C]), 0.0)
    tinv = _unit_lower_inverse_many([-h["g_b"][:C] for h in heads], C, _dot1)
    for h in heads:
        h["g_k"] = jnp.where(mask2, _dot_nt(h["lhs"], h["bk"][C:]), 0.0)
    for h in heads:
        h["w_kv"] = _dot1(h["g_k"], h["v"])
    for h in heads:
        h["s0"] = s_ref[h["idx"]]
        h["g_s"] = _dot_nt(h["lhs"], h["s0"].astype(BF16))
    for h, ti in zip(heads, tinv):
        h["u"] = _dot1(ti, h["g_s"][:C] + h["w_kv"][:C])
    for h in heads:
        uv = jnp.concatenate([h["u"], h["v"]], axis=0).astype(BF16)
        s_ref[h["idx"]] = (h["s0"] + _dot_tn(uv, h["bk"])) * h["p_last"]
    outs = []
    for h in heads:
        out = h["g_s"][C:] + h["w_kv"][C:] + _dot1(h["g_b"][C:], h["u"])
        mu = jnp.mean(out, axis=-1, keepdims=True)
        d = out - mu
        var = jnp.mean(d * d, axis=-1, keepdims=True)
        outs.append(d * lax.rsqrt(var + GN_EPS) * h["ln_w"] + h["ln_b"] + h["bonus"])
    for pp in range(PP):
        o_ref[pp] = jnp.concatenate(outs[pp * hpp:(pp + 1) * hpp], axis=1) * gate_ref[pp]

    @pl.when(c == pl.num_programs(1) - 1)
    def _():
        s_out_ref[...] = s_ref[...]


def _rwkv_chunk_prompt(r, k, v, wl, al, gate, prm, *, C, PP):
    npair, t, _ = r.shape
    hpp = LANE // RWKV_HS
    seq = pl.BlockSpec((PP, C, LANE), lambda p, c: (p, c, 0))
    return pl.pallas_call(
        _rwkv_chunk_kernel,
        grid=(npair // PP, t // C),
        in_specs=[seq] * 6 + [pl.BlockSpec((PP, 8, LANE), lambda p, c: (p, 0, 0))],
        out_specs=[seq, pl.BlockSpec((PP * hpp, RWKV_HS, RWKV_HS), lambda p, c: (p, 0, 0))],
        out_shape=[jax.ShapeDtypeStruct((npair, t, LANE), F32),
                   jax.ShapeDtypeStruct((npair * hpp, RWKV_HS, RWKV_HS), F32)],
        scratch_shapes=[pltpu.VMEM((PP * hpp, RWKV_HS, RWKV_HS), F32)],
        compiler_params=_cp(("parallel", "arbitrary")),
        name="rwkv_chunk",
    )(r, k, v, wl, al, gate, prm)


def _rwkv_dec_kernel(r_ref, k_ref, v_ref, wl_ref, al_ref, gate_ref, prm_ref, s0_ref, o_ref, s_out_ref):
    N = RWKV_HS
    prm = prm_ref[...]
    r2 = r_ref[0]
    k2 = k_ref[0]
    v2 = v_ref[0]
    log_decay2, a2, kk_raw2, kh2 = _rwkv_prep(r2, k2, wl_ref[0], al_ref[0], prm[0:1], prm[1:2], prm[2:3], prm[3:4])
    w2 = jnp.exp(log_decay2)
    gate = gate_ref[0]
    nh = r2.shape[1] // N
    sls = [slice(h * N, (h + 1) * N) for h in range(nh)]
    lane_sum = lambda xs: [jnp.sum(x, axis=-1, keepdims=True) for x in xs]
    kks = [kk_raw2[:, sl] for sl in sls]
    kks = [kk / jnp.maximum(jnp.sqrt(n2), 1e-12) for kk, n2 in zip(kks, lane_sum([kk * kk for kk in kks]))]
    s0s = [s0_ref[0, h] for h in range(nh)]
    sas = lane_sum([s0 * (-kk) for s0, kk in zip(s0s, kks)])
    vcols = [_row_to_col(v2[:, sl], N) for sl in sls]
    s_news = [s0 * w2[:, sl] + sa * (kk * a2[:, sl]) + vc * kh2[:, sl]
              for s0, sl, sa, kk, vc in zip(s0s, sls, sas, kks, vcols)]
    for h in range(nh):
        s_out_ref[0, h] = s_news[h]
    outs = [_col_to_row(oc, N) for oc in lane_sum([sn * r2[:, sl] for sn, sl in zip(s_news, sls)])]
    mus = lane_sum(outs)
    ds = [o - mu * (1.0 / N) for o, mu in zip(outs, mus)]
    vars_ = lane_sum([d * d for d in ds])
    bon = lane_sum([r2[:, sl] * kh2[:, sl] * prm[4:5, sl] for sl in sls])
    for sl, d, var, bo in zip(sls, ds, vars_, bon):
        gn = d * lax.rsqrt(var * (1.0 / N) + GN_EPS) * prm[5:6, sl] + prm[6:7, sl]
        o_ref[0, :, sl] = (gn + bo * v2[:, sl]) * gate[:, sl]


def _rwkv_sample(r, k, v, wl, al, gate, prm, s0):
    b, _, d = r.shape
    row = pl.BlockSpec((1, 1, d), lambda bi: (bi, 0, 0))
    st = pl.BlockSpec((1,) + s0.shape[1:], lambda bi: (bi, 0, 0, 0))
    return pl.pallas_call(
        _rwkv_dec_kernel,
        grid=(b,),
        in_specs=[row] * 6 + [pl.BlockSpec((8, d), lambda bi: (0, 0)), st],
        out_specs=[row, st],
        out_shape=[jax.ShapeDtypeStruct((b, 1, d), F32), jax.ShapeDtypeStruct(s0.shape, F32)],
        compiler_params=_cp(("parallel",)),
        name="rwkv_sample",
    )(r, k, v, wl, al, gate, prm, s0)


def _alibi_slopes():
    return jnp.asarray(2.0 ** (-8.0 * np.arange(1, NSA_HEADS + 1) / NSA_HEADS), dtype=F32)


def _overlap_matrix(nch, nsp):
    cstart = np.arange(nch)[:, None] * D_CMP
    sstart = np.arange(nsp)[None, :] * L_SLC
    return jnp.asarray(((cstart < sstart + L_SLC) & (cstart + L_CMP > sstart)).astype(np.float32))


def _pack_w_in(w):
    offs = np.concatenate([[0], np.cumsum(EVEN_SPLIT)])
    qa, kvc, kvs, kvw, ga, qkv, z, b, a = [w[:, offs[i]:offs[i + 1]] for i in range(len(EVEN_SPLIT))]
    used = COL_SM + A_GATES + 2 * DN_HEADS
    pad = jnp.zeros((w.shape[0], E_IN_PAD - used), w.dtype)
    return jnp.concatenate([qa, qkv, z, kvc, kvs, kvw, ga, b, a, pad], axis=1).astype(BF16)


def _pack_cmp_w1(w1):
    hid = w1.shape[-1]
    w = w1.reshape(2, 2, D_CMP, HD, hid).transpose(0, 2, 3, 1, 4)
    return w.reshape(2, D_CMP, HD, 2 * hid).astype(BF16)


def _pack_cmp_pe(pe):
    p = pe.reshape(2, 2, D_CMP, HD).transpose(0, 2, 1, 3)
    return jnp.concatenate([p, jnp.zeros((2, D_CMP, 6, HD), pe.dtype)], axis=2)


def _dn_gate_params(a_log, dt_bias):
    row = jnp.zeros((2, LANE), F32).at[0, SM_A:SM_A + DN_HEADS].set(a_log).at[1, SM_A:SM_A + DN_HEADS].set(dt_bias)
    return row, row.T


def _even_layer(xp, xs, w_in, w_out, pe, w1, b1, w2, conv_w, a_log, dt_bias, norm_w,
                cache_cmp, cache_slc, win_buf, conv_buf, dn_s0, page_table, g, b):
    t = xp.shape[0]
    bs = xs.shape[0]
    n_pages = page_table.shape[1]
    past = n_pages * PAGE_ROWS
    slopes = _alibi_slopes()
    w_in_p = _pack_w_in(w_in)
    w4 = _pack_cmp_w1(w1)
    pe8 = _pack_cmp_pe(pe)
    w2b = w2.astype(BF16)
    conv_wt = conv_w.T
    prow, pcol = _dn_gate_params(a_log, dt_bias)
    nw = norm_w.reshape(1, DV)

    hp = _mm(xp, w_in_p)
    hs = _mm(xs, w_in_p)
    kvc_p = hp[:, COL_KVC:COL_KVC + A_KVW]
    kvs_p = hp[:, COL_KVS:COL_KVS + A_KVW]
    kvw_p = hp[:, COL_KVW:COL_KVW + A_KVW]
    kvc_s = hs[:, COL_KVC:COL_KVC + A_KVW]
    kvs_s = hs[:, COL_KVS:COL_KVS + A_KVW]
    kvw_s = hs[:, COL_KVW:COL_KVW + A_KVW]

    TQ = _tile(t, (128, 64, 32, 16, 8))
    nch = t // D_CMP
    ns = t // L_SLC
    nsp = -(-ns // LANE) * LANE
    arange_pt = jnp.arange(t // PAGE_ROWS, dtype=jnp.int32)[None]
    h1 = _cmp_stage1(kvc_p.reshape(t // PAGE_ROWS, PAGE_SUBROWS, HD), arange_pt, w4)
    cmp_p = _cmp_stage2(h1, pe8, w4, b1, w2b)
    q_p = hp
    o_cmp, sel = _nsa_select(q_p[None], cmp_p, _overlap_matrix(nch, nsp), slopes,
                             TQ=TQ, NC=(t - L_CMP) // D_CMP + 1, NS=ns, q_off=0, BB=1)
    kvb = hp[:, COL_KVC:COL_KVC + 3 * A_KVW].astype(BF16)
    o_slc = _nsa_slc_prompt(q_p, sel[0], kvb, slopes, TQ=TQ, TK=_tile(t, (512, 256, 128, 64)))
    o_a_p = _nsa_win_prompt(q_p, kvb, hp, o_cmp[0], o_slc, slopes, TQ=TQ)

    nch_s = n_pages * CH_PER_PAGE
    nc_s = (past + 1 - L_CMP) // D_CMP + 1
    ns_s = -(-(past + 1) // L_SLC)
    nsp_s = -(-ns_s // LANE) * LANE
    h1s = _cmp_stage1(cache_cmp.reshape(cache_cmp.shape[0], PAGE_SUBROWS, HD), page_table, w4)
    cmp_s = _cmp_stage2(h1s, pe8, w4, b1, w2b)
    hs3 = hs[:, None, :]
    q_s = hs3[:, :, COL_QA:COL_QA + A_Q]
    o_cmp_s, sel_s = _nsa_select(q_s, cmp_s, _overlap_matrix(nch_s, nsp_s), slopes,
                                 TQ=1, NC=nc_s, NS=ns_s, q_off=past, BB=_tile(bs, (4, 2, 1)))
    o_slc_s = _nsa_slc_sample(q_s, sel_s[:, :, 0, :], kvs_s[:, None, :],
                              cache_slc.reshape(cache_slc.shape[0], PAGE_SUBROWS, HD), page_table, slopes)
    wb = win_buf.reshape(bs, win_buf.shape[1] * KV_PARTS, HD)
    o_a_s = _nsa_win_sample(q_s, wb, kvw_s[:, None, :], hs3[:, :, COL_SM:COL_SM + LANE], o_cmp_s, o_slc_s,
                            slopes, past=past)

    qkvn = _dn_conv_prompt(hp, jnp.zeros((8, DN_QKV), F32), conv_wt)
    smt = hp[:, COL_SM:COL_SM + LANE].T
    o_b_p, dn_s_p = _dn_chunk_prompt(qkvn, hp, smt, prow, pcol, nw, C=_tile(t, (128,)))
    o_b_s, dn_s_s = _dn_sample(conv_buf, hs3, conv_wt, prow, nw, dn_s0)

    w_out_b = w_out.astype(BF16)
    yp = _mm_ln([o_a_p, o_b_p], w_out_b, xp, g, b)
    ys = _mm_ln([o_a_s[:, 0], o_b_s[:, 0]], w_out_b, xs, g, b)

    kv6 = lambda a: a.reshape(a.shape[:-1] + (2, NSA_KV, HD))
    raw_p = hp[:, COL_DQKV:COL_DQKV + DN_QKV]
    raw_s = hs[:, COL_DQKV:COL_DQKV + DN_QKV]
    wlen = min(WINDOW, t)
    outs = dict(
        cmp_p=kv6(kvc_p)[None], cmp_s=kv6(kvc_s)[:, None],
        slc_p=kv6(kvs_p)[None], slc_s=kv6(kvs_s)[:, None],
        win_p=kv6(kvw_p[t - wlen:])[None],
        win_s=jnp.concatenate([win_buf, kv6(kvw_s)[:, None]], axis=1)[:, 1:],
        conv_p=jnp.concatenate([jnp.zeros((CONV_W - 1, DN_QKV), F32), raw_p], axis=0)[t:][None],
        conv_s=jnp.concatenate([conv_buf, raw_s[:, None]], axis=1)[:, 1:],
        dns_p=dn_s_p[None], dns_s=dn_s_s,
    )
    return yp, ys, outs


def _odd_layer(xp, xs, shift_s, s0_s, mix, wr, wk, wv, wo, w0, w1, w2, a0, a1, a2, g1, g2, k_k, k_a, r_k,
               ln_w, ln_b, g, b):
    t, d = xp.shape
    bs = xs.shape[0]
    npair = d // LANE
    first_p = jnp.zeros((8, d), F32)

    def padk(wa, wb_):
        r = wa.shape[1]
        rp = -(-r // LANE) * LANE
        return (jnp.pad(wa, ((0, 0), (0, rp - r))).astype(BF16), jnp.pad(wb_, ((0, rp - r), (0, 0))).astype(BF16))

    wrb, wkb, wvb, wob = (w.astype(BF16) for w in (wr, wk, wv, wo))
    w1b, w2b = padk(w1, w2)
    a1b, a2b = padk(a1, a2)
    g1b, g2b = padk(g1, g2)
    prm = jnp.stack([w0, a0, k_k, k_a, r_k.reshape(d), ln_w, ln_b, jnp.zeros((d,), F32)])

    def proj(x, prev_kw, pair_out):
        mm = lambda w_, i, **kw: _mm(x, w_, mixrow=mix[i:i + 1], **prev_kw, **kw)
        r = mm(wrb, 0, pair_out=pair_out)
        wl = _mm(mm(w1b, 1, act="tanh"), w2b, pair_out=pair_out)
        k = mm(wkb, 2, pair_out=pair_out)
        v = mm(wvb, 3, pair_out=pair_out)
        al = _mm(mm(a1b, 4), a2b, pair_out=pair_out)
        gate = _mm(mm(g1b, 5, act="sigmoid"), g2b, pair_out=pair_out)
        return r, k, v, wl, al, gate

    pp = proj(xp, dict(first8=first_p), True)
    prm_pair = prm.reshape(8, npair, LANE).transpose(1, 0, 2)
    y_p, s_p = _rwkv_chunk_prompt(*pp, prm_pair, C=_tile(t, (64, 32, 16, 8)), PP=16)
    yp = _mm_ln([y_p], wob, xp, g, b, pair_in=True)

    ps = [a[:, None, :] for a in proj(xs, dict(xprev=shift_s), False)]
    y_s, s_s = _rwkv_sample(*ps, prm, s0_s)
    ys = _mm_ln([y_s[:, 0]], wob, xs, g, b)
    return yp, ys, dict(shift_p=xp[t - 1:t], shift_s=xs, rs_p=s_p[None], rs_s=s_s)


def kernel(x_prompt, x_sample, cache_nsa_cmp, cache_nsa_slc, cache_nsa_win, state_dn_conv, state_dn_S, state_rwkv_shift, state_rwkv_S, page_table, ln_g, ln_b, ffn_wi, ffn_wo, mix_w_in, mix_w_out, nsa_cmp_pe, nsa_cmp_w1, nsa_cmp_b1, nsa_cmp_w2, dn_conv_w, dn_a_log, dn_dt_bias, dn_norm_w, rwkv_mix, rwkv_wr, rwkv_wk, rwkv_wv, rwkv_wo, rwkv_w0, rwkv_w1, rwkv_w2, rwkv_a0, rwkv_a1, rwkv_a2, rwkv_g1, rwkv_g2, rwkv_k_k, rwkv_k_a, rwkv_r_k, rwkv_ln_w, rwkv_ln_b):
    bp, t, d = x_prompt.shape
    assert bp == 1 and x_sample.shape[1] == 1
    depth = ffn_wi.shape[0]
    xp = x_prompt[0]
    xs = x_sample[:, 0]
    even, odd = [], []
    wi, wo = ffn_wi, ffn_wo
    for l in range(depth):
        gl = lambda i: (ln_g[l, i][None], ln_b[l, i][None])
        xp = _ffn_ln(xp, wi, wo, l, 0, *gl(0))
        xs = _ffn_ln(xs, wi, wo, l, 0, *gl(0))
        if l % 2 == 0:
            e = l // 2
            xp, xs, o = _even_layer(
                xp, xs, mix_w_in[e], mix_w_out[e], nsa_cmp_pe[e], nsa_cmp_w1[e], nsa_cmp_b1[e], nsa_cmp_w2[e],
                dn_conv_w[e], dn_a_log[e], dn_dt_bias[e], dn_norm_w[e], cache_nsa_cmp[e], cache_nsa_slc[e],
                cache_nsa_win[e], state_dn_conv[e], state_dn_S[e], page_table, *gl(1))
            even.append(o)
        else:
            c = l // 2
            xp, xs, o = _odd_layer(
                xp, xs, state_rwkv_shift[c], state_rwkv_S[c], rwkv_mix[c], rwkv_wr[c], rwkv_wk[c], rwkv_wv[c],
                rwkv_wo[c], rwkv_w0[c], rwkv_w1[c], rwkv_w2[c], rwkv_a0[c], rwkv_a1[c], rwkv_a2[c], rwkv_g1[c],
                rwkv_g2[c], rwkv_k_k[c], rwkv_k_a[c], rwkv_r_k[c], rwkv_ln_w[c], rwkv_ln_b[c], *gl(1))
            odd.append(o)
        xp = _ffn_ln(xp, wi, wo, l, 1, *gl(2))
        xs = _ffn_ln(xs, wi, wo, l, 1, *gl(2))
    st = lambda lst, key: jnp.stack([o[key] for o in lst])
    return (xp[None], xs[:, None],
            st(even, "cmp_p"), st(even, "cmp_s"), st(even, "slc_p"), st(even, "slc_s"),
            st(even, "win_p"), st(even, "win_s"), st(even, "conv_p"), st(even, "conv_s"),
            st(even, "dns_p"), st(even, "dns_s"),
            st(odd, "shift_p"), st(odd, "shift_s"), st(odd, "rs_p"), st(odd, "rs_s"))
```

```python
import functools

import numpy as np
import jax
import jax.numpy as jnp
from jax import lax
from jax.experimental import pallas as pl
from jax.experimental.pallas import tpu as pltpu

F32 = jnp.float32
BF16 = jnp.bfloat16
HI = lax.Precision.HIGHEST

DEPTH = 2
ALPHA = (2 * DEPTH) ** 0.25
LN_EPS = 1e-5
RMS_EPS = 1e-6
NSA_HEADS = 8
NSA_KV = 2
NSA_GROUP = NSA_HEADS // NSA_KV
HD = 128
L_CMP = 32
D_CMP = 16
L_SLC = 64
N_SEL = 16
WINDOW = 512
NEG_INF = -1e30
FORCE_SCORE = 1e6
DN_HEADS = 8
DK = 128
DV = 128
CONV_W = 4
RWKV_HS = 64
GN_EPS = 64e-5

A_Q = NSA_HEADS * HD
A_KVW = 2 * NSA_KV * HD
A_GATES = 3 * NSA_HEADS
DN_QKV = DN_HEADS * (2 * DK + DV)
EVEN_SPLIT = (A_Q, A_KVW, A_KVW, A_KVW, A_GATES, DN_QKV, DN_HEADS * DV, DN_HEADS, DN_HEADS)
COL_QA = 0
COL_DQKV = 1024
COL_Z = 4096
COL_KVC = 5120
COL_KVS = 5632
COL_KVW = 6144
COL_SM = 6656
SM_BETA = A_GATES
SM_A = A_GATES + DN_HEADS
E_IN_PAD = 7168
LANE = 128
PAGE_ROWS = 128

VMEM_LIMIT = 56 * 1024 * 1024


def _cp(sem):
    return pltpu.CompilerParams(dimension_semantics=sem, vmem_limit_bytes=VMEM_LIMIT)


def _tile(n, prefs):
    for t in prefs:
        if n % t == 0:
            return t
    return n


def _dot(a, b, precision=None):
    return jnp.dot(a, b, preferred_element_type=F32, precision=precision)


def _dot_nt(a, b, precision=None):
    return lax.dot_general(a, b, (((1,), (1,)), ((), ())), preferred_element_type=F32, precision=precision)


def _dot_tn(a, b, precision=None):
    return lax.dot_general(a, b, (((0,), (0,)), ((), ())), preferred_element_type=F32, precision=precision)


def _layernorm_rows(y, g, b):
    mu = jnp.mean(y, axis=-1, keepdims=True)
    d = y - mu
    var = jnp.mean(d * d, axis=-1, keepdims=True)
    return d * lax.rsqrt(var + LN_EPS) * g + b


def _softplus(x):
    return jnp.maximum(x, 0.0) + jnp.log1p(jnp.exp(-jnp.abs(x)))


def _silu(x):
    return x * jax.nn.sigmoid(x)


def _ffn_kernel(x_ref, wg_ref, wu_ref, wo_ref, g_ref, b_ref, o_ref, xb_ref, *, nf):
    f = pl.program_id(1)

    @pl.when(f == 0)
    def _():
        xb_ref[...] = x_ref[...].astype(BF16)
        o_ref[...] = jnp.zeros_like(o_ref)

    xb = xb_ref[...]
    gate = _dot(xb, wg_ref[...].astype(BF16))
    up = _dot(xb, wu_ref[...].astype(BF16))
    act = (_silu(gate) * up).astype(BF16)
    o_ref[...] += _dot(act, wo_ref[...].astype(BF16))

    @pl.when(f == nf - 1)
    def _():
        y = ALPHA * x_ref[...] + 0.5 * o_ref[...]
        o_ref[...] = _layernorm_rows(y, g_ref[...], b_ref[...])


def _ffn_ln(x, wi, wo, layer, which, g, b):
    m, d = x.shape
    f = wo.shape[2]
    tm = _tile(m, (1024, 512, 256, 128, 64, 32, 16, 8))
    tf = _tile(f, (256, 128))
    nf = f // tf
    return pl.pallas_call(
        functools.partial(_ffn_kernel, nf=nf),
        grid=(m // tm, nf),
        in_specs=[
            pl.BlockSpec((tm, d), lambda i, j: (i, 0), pipeline_mode=pl.Buffered(1)),
            pl.BlockSpec((None, None, d, tf), lambda i, j: (layer, which, 0, j)),
            pl.BlockSpec((None, None, d, tf), lambda i, j: (layer, which, 0, j + nf)),
            pl.BlockSpec((None, None, tf, d), lambda i, j: (layer, which, j, 0)),
            pl.BlockSpec((1, d), lambda i, j: (0, 0)),
            pl.BlockSpec((1, d), lambda i, j: (0, 0)),
        ],
        out_specs=pl.BlockSpec((tm, d), lambda i, j: (i, 0)),
        out_shape=jax.ShapeDtypeStruct((m, d), F32),
        scratch_shapes=[pltpu.VMEM((tm, d), BF16)],
        compiler_params=_cp(("parallel", "arbitrary")),
        name="ffn_ln",
    )(x, wi, wi, wo, g, b)


def _mm_kernel(*refs, mix, act, pair_out):
    if mix == "array":
        x_ref, xp_ref, m_ref, w_ref, o_ref, xb_ref = refs
    elif mix == "shift":
        x_ref, prev_ref, first_ref, m_ref, w_ref, o_ref, xb_ref, hist_ref = refs
    else:
        x_ref, w_ref, o_ref, xb_ref = refs

    @pl.when(pl.program_id(1) == 0)
    def _():
        x = x_ref[...]
        if mix == "array":
            x = x + (xp_ref[...] - x) * m_ref[...]
        elif mix == "shift":
            tm = x_ref.shape[0]
            hist_ref[0:8, :] = jnp.where(pl.program_id(0) == 0, first_ref[...], prev_ref[...])
            hist_ref[8:8 + tm, :] = x
            x = x + (hist_ref[7:7 + tm, :] - x) * m_ref[...]
        xb_ref[...] = x.astype(BF16)

    y = _dot(xb_ref[...], w_ref[...])
    if act == "tanh":
        y = jnp.tanh(y)
    elif act == "sigmoid":
        y = jax.nn.sigmoid(y)
    if pair_out:
        for p in range(o_ref.shape[0]):
            o_ref[p] = y[:, p * LANE:(p + 1) * LANE]
    else:
        o_ref[...] = y


def _mm(x, w, *, xprev=None, first8=None, mixrow=None, act=None, pair_out=False, tn_prefs=(512, 256, 128)):
    m, k = x.shape
    n = w.shape[1]
    tm = _tile(m, (1024, 512, 256, 128, 64, 32, 16, 8))
    tn = _tile(n, tn_prefs)
    mix = "array" if xprev is not None else ("shift" if first8 is not None else None)
    in_specs = [pl.BlockSpec((tm, k), lambda i, j: (i, 0))]
    args = [x]
    scratch = [pltpu.VMEM((tm, k), BF16)]
    if mix == "array":
        in_specs += [pl.BlockSpec((tm, k), lambda i, j: (i, 0)), pl.BlockSpec((1, k), lambda i, j: (0, 0))]
        args += [xprev, mixrow]
    elif mix == "shift":
        in_specs += [pl.BlockSpec((8, k), lambda i, j: (jnp.maximum(i * (tm // 8) - 1, 0), 0)),
                     pl.BlockSpec((8, k), lambda i, j: (0, 0)), pl.BlockSpec((1, k), lambda i, j: (0, 0))]
        args += [x, first8, mixrow]
        scratch.append(pltpu.VMEM((tm + 8, k), F32))
    in_specs.append(pl.BlockSpec((k, tn), lambda i, j: (0, j)))
    args.append(w)
    if pair_out:
        npb = tn // LANE
        out_spec = pl.BlockSpec((npb, tm, LANE), lambda i, j: (j, i, 0))
        out_shape = jax.ShapeDtypeStruct((n // LANE, m, LANE), F32)
    else:
        out_spec = pl.BlockSpec((tm, tn), lambda i, j: (i, j))
        out_shape = jax.ShapeDtypeStruct((m, n), F32)
    return pl.pallas_call(
        functools.partial(_mm_kernel, mix=mix, act=act, pair_out=pair_out),
        grid=(m // tm, n // tn),
        in_specs=in_specs,
        out_specs=out_spec,
        out_shape=out_shape,
        scratch_shapes=scratch,
        compiler_params=_cp(("parallel", "arbitrary")),
        name="matmul",
    )(*args)


def _mmln_kernel(*refs, pair_in, n_parts):
    a_refs = refs[:n_parts]
    w_ref, x_ref, g_ref, b_ref, o_ref = refs[n_parts:]
    if pair_in:
        a_ref = a_refs[0]
        y = _dot(jnp.concatenate([a_ref[p].astype(BF16) for p in range(a_ref.shape[0])], axis=1), w_ref[...])
    else:
        y = None
        k0 = 0
        for a_ref in a_refs:
            kw = a_ref.shape[1]
            part = _dot(a_ref[...].astype(BF16), w_ref[k0:k0 + kw, :])
            y = part if y is None else y + part
            k0 += kw
    o_ref[...] = _layernorm_rows(ALPHA * x_ref[...] + y, g_ref[...], b_ref[...])


def _mm_ln(a_parts, w, x, g, b, *, pair_in=False):
    m, d = x.shape
    k = w.shape[0]
    tm = _tile(m, (512, 256, 128, 64, 32, 16, 8))
    if pair_in:
        a_specs = [pl.BlockSpec((k // LANE, tm, LANE), lambda i: (0, i, 0))]
    else:
        a_specs = [pl.BlockSpec((tm, a.shape[1]), lambda i: (i, 0)) for a in a_parts]
    return pl.pallas_call(
        functools.partial(_mmln_kernel, pair_in=pair_in, n_parts=len(a_parts)),
        grid=(m // tm,),
        in_specs=a_specs + [
            pl.BlockSpec((k, d), lambda i: (0, 0)),
            pl.BlockSpec((tm, d), lambda i: (i, 0)),
            pl.BlockSpec((1, d), lambda i: (0, 0)),
            pl.BlockSpec((1, d), lambda i: (0, 0)),
        ],
        out_specs=pl.BlockSpec((tm, d), lambda i: (i, 0)),
        out_shape=jax.ShapeDtypeStruct((m, d), F32),
        compiler_params=_cp(("parallel",)),
        name="matmul_ln",
    )(*a_parts, w, x, g, b)


CH_PER_PAGE = PAGE_ROWS // D_CMP
KV_PARTS = 2 * NSA_KV
PAGE_SUBROWS = PAGE_ROWS * KV_PARTS


def _cmp1_kernel(pt_ref, *refs, G):
    del pt_ref
    page_refs = refs[:G]
    w_ref, o_ref = refs[G:]
    rows = G * CH_PER_PAGE
    for sg in range(KV_PARTS):
        s = sg // NSA_KV
        acc = jnp.zeros((rows, 2 * HD), F32)
        for p in range(D_CMP):
            x = jnp.concatenate(
                [page_refs[j][0, pl.ds(p * KV_PARTS + sg, CH_PER_PAGE, stride=D_CMP * KV_PARTS), :]
                 for j in range(G)], axis=0)
            acc = acc + _dot(x.astype(BF16), w_ref[s, p])
        o_ref[0, :, sg * 2 * HD:(sg + 1) * 2 * HD] = acc


def _cmp_stage1(pool, page_table, w4):
    b, n_pages = page_table.shape
    G = _tile(n_pages, (16, 8, 4, 2, 1))

    def page_map(j):
        return lambda bi, p, pt: (pt[bi, p * G + j], 0, 0)

    in_specs = [pl.BlockSpec((1, PAGE_SUBROWS, HD), page_map(j)) for j in range(G)]
    in_specs.append(pl.BlockSpec(w4.shape, lambda bi, p, pt: (0, 0, 0, 0)))
    grid_spec = pltpu.PrefetchScalarGridSpec(
        num_scalar_prefetch=1,
        grid=(b, n_pages // G),
        in_specs=in_specs,
        out_specs=pl.BlockSpec((1, G * CH_PER_PAGE, 8 * HD), lambda bi, p, pt: (bi, p, 0)),
    )
    return pl.pallas_call(
        functools.partial(_cmp1_kernel, G=G),
        grid_spec=grid_spec,
        out_shape=jax.ShapeDtypeStruct((b, n_pages * CH_PER_PAGE, 8 * HD), F32),
        compiler_params=_cp(("parallel", "arbitrary")),
        name="nsa_cmp_stage1",
    )(page_table, *([pool] * G), w4)


def _gelu_tanh(x):
    return 0.5 * x * (1.0 + jnp.tanh(np.sqrt(2.0 / np.pi).astype(np.float32) * (x + 0.044715 * (x * x * x))))


def _cmp2_kernel(h_ref, pe_ref, w4_ref, b1_ref, w2_ref, o_ref):
    nch = h_ref.shape[1]
    for s in range(2):
        pacc = jnp.zeros((8, 2 * HD), F32)
        for p in range(D_CMP):
            pacc = pacc + _dot(pe_ref[s, p].astype(BF16), w4_ref[s, p])
        const = pacc[0:1, 0:HD] + pacc[1:2, HD:2 * HD] + b1_ref[s:s + 1, :]
        for gi in range(NSA_KV):
            sg = s * NSA_KV + gi
            h0 = h_ref[0, :, sg * 2 * HD:sg * 2 * HD + HD]
            h1 = h_ref[0, :, sg * 2 * HD + HD:(sg + 1) * 2 * HD]
            hid = h0 + pltpu.roll(h1, nch - 1, 0) + const
            o_ref[0, :, sg * HD:(sg + 1) * HD] = _dot(_gelu_tanh(hid).astype(BF16), w2_ref[s])


def _cmp_stage2(h, pe8, w4, b1, w2):
    b, nch, _ = h.shape
    return pl.pallas_call(
        _cmp2_kernel,
        grid=(b,),
        in_specs=[
            pl.BlockSpec((1, nch, 8 * HD), lambda i: (i, 0, 0)),
            pl.BlockSpec(pe8.shape, lambda i: (0, 0, 0, 0)),
            pl.BlockSpec(w4.shape, lambda i: (0, 0, 0, 0)),
            pl.BlockSpec(b1.shape, lambda i: (0, 0)),
            pl.BlockSpec(w2.shape, lambda i: (0, 0, 0)),
        ],
        out_specs=pl.BlockSpec((1, nch, A_KVW), lambda i: (i, 0, 0)),
        out_shape=jax.ShapeDtypeStruct((b, nch, A_KVW), F32),
        compiler_params=_cp(("parallel",)),
        name="nsa_cmp_stage2",
    )(h, pe8, w4, b1, w2)


def _nsa_sel_kernel(slopes_ref, q_ref, cmp_ref, ov_ref, ocmp_ref, sel_ref, *, TQ, NC, NS, q_off, n_pick, RS):
    i = pl.program_id(1)
    bb = q_ref.shape[0]
    nch = cmp_ref.shape[1]
    nsp = ov_ref.shape[1]
    qpos = q_off + i * TQ + lax.broadcasted_iota(jnp.int32, (TQ, 1), 0)
    cidx = lax.broadcasted_iota(jnp.int32, (1, nch), 1)
    cstart = cidx * D_CMP
    cmask = jnp.logical_and(cstart + (L_CMP - 1) <= qpos, cidx < NC)
    mask_add = jnp.where(cmask, 0.0, NEG_INF)
    mask_mul = jnp.where(cmask, 1.0, 0.0)
    cdist = (qpos - cstart).astype(F32) - 0.5 * (L_CMP - 1)
    sid = lax.broadcasted_iota(jnp.int32, (1, nsp), 1)
    cur = lax.shift_right_arithmetic(qpos, int(np.log2(L_SLC)))
    svalid = jnp.logical_and(sid * L_SLC <= qpos, sid < NS)
    forced = jnp.logical_or(sid == 0, jnp.logical_or(sid == cur, sid == cur - 1))
    ov = ov_ref[...]
    scores = []
    for bi in range(bb):
        q = q_ref[bi]
        cm = cmp_ref[bi]
        for g in range(NSA_KV):
            kb = cm[:, g * HD:(g + 1) * HD].astype(BF16)
            vb = cm[:, (NSA_KV + g) * HD:(NSA_KV + g + 1) * HD].astype(BF16)
            pcs = jnp.zeros((TQ, nch), F32)
            for j in range(NSA_GROUP):
                h = NSA_GROUP * g + j
                qj = (q[:, h * HD:(h + 1) * HD] * HD ** -0.5).astype(BF16)
                lg = _dot_nt(qj, kb) - slopes_ref[h] * cdist + mask_add
                e = jnp.exp(lg - jnp.max(lg, axis=-1, keepdims=True))
                p = e * (mask_mul * (1.0 / jnp.sum(e, axis=-1, keepdims=True)))
                pcs = pcs + p
                ocmp_ref[bi, :, h * HD:(h + 1) * HD] = _dot(p.astype(BF16), vb)
            imp = _dot(pcs, ov, precision=HI)
            scores.append(jnp.where(svalid, jnp.where(forced, FORCE_SCORE, imp), NEG_INF))
    sidb = jnp.broadcast_to(sid, (RS, nsp))
    validb = [jnp.logical_and(sid * L_SLC <= qpos[r0:r0 + RS], sid < NS) for _ in scores for r0 in range(0, TQ, RS)]
    chains = [sc[r0:r0 + RS] for sc in scores for r0 in range(0, TQ, RS)]
    sels = [jnp.zeros((RS, nsp), F32) for _ in chains]
    for _ in range(n_pick):
        firsts = [jnp.argmax(sc, axis=-1, keepdims=True).astype(jnp.int32) for sc in chains]
        picks = [sidb == f for f in firsts]
        sels = [jnp.where(jnp.logical_and(pk, vb), 1.0, sl) for pk, vb, sl in zip(picks, validb, sels)]
        chains = [jnp.where(pk, -jnp.inf, sc) for pk, sc in zip(picks, chains)]
    nsub = TQ // RS
    for bi in range(bb):
        for g in range(NSA_KV):
            for r in range(nsub):
                sel_ref[bi, g, r * RS:(r + 1) * RS, :] = sels[(bi * NSA_KV + g) * nsub + r]


def _nsa_select(q, cmp, overlap, slopes, *, TQ, NC, NS, q_off, BB):
    b, t, _ = q.shape
    nch = cmp.shape[1]
    nsp = overlap.shape[1]
    kern = functools.partial(_nsa_sel_kernel, TQ=TQ, NC=NC, NS=NS, q_off=q_off, n_pick=min(N_SEL, NS),
                             RS=_tile(TQ, (32, 16, 8)))
    return pl.pallas_call(
        kern,
        grid=(b // BB, t // TQ),
        in_specs=[
            pl.BlockSpec(memory_space=pltpu.SMEM),
            pl.BlockSpec((BB, TQ, A_Q), lambda bi, i: (bi, i, 0)),
            pl.BlockSpec((BB, nch, A_KVW), lambda bi, i: (bi, 0, 0)),
            pl.BlockSpec((nch, nsp), lambda bi, i: (0, 0)),
        ],
        out_specs=[
            pl.BlockSpec((BB, TQ, A_Q), lambda bi, i: (bi, i, 0)),
            pl.BlockSpec((BB, NSA_KV, TQ, nsp), lambda bi, i: (bi, 0, i, 0)),
        ],
        out_shape=[jax.ShapeDtypeStruct((b, t, A_Q), F32), jax.ShapeDtypeStruct((b, NSA_KV, t, nsp), F32)],
        compiler_params=_cp(("parallel", "parallel")),
        name="nsa_cmp_select",
    )(slopes, q, cmp, overlap)


def _stack_heads(q, tq):
    del tq
    return jnp.concatenate([q[:, j * HD:(j + 1) * HD] for j in range(NSA_GROUP)], axis=0)


def _online_step(carry, s, dist, mask_add, slopes, v):
    m, l, acc = carry
    lg = s + jnp.concatenate([mask_add - sl * dist for sl in slopes], axis=0)
    m_new = jnp.maximum(m, jnp.max(lg, axis=-1, keepdims=True))
    p = jnp.exp(lg - m_new)
    a = jnp.exp(m - m_new)
    l = a * l + jnp.sum(p, axis=-1, keepdims=True)
    acc = a * acc + _dot(p.astype(BF16), v)
    return m_new, l, acc


def _nsa_slc_kernel(slopes_ref, q_ref, sel_ref, k_ref, v_ref, o_ref, *, TQ, TK):
    g = pl.program_id(0)
    i = pl.program_id(1)
    nsp = sel_ref.shape[-1]
    rows = NSA_GROUP * TQ
    q4 = (_stack_heads(q_ref[...], TQ) * HD ** -0.5).astype(BF16)
    qposf = (i * TQ + lax.broadcasted_iota(jnp.int32, (TQ, 1), 0)).astype(F32)
    slopes = [slopes_ref[NSA_GROUP * g + j] for j in range(NSA_GROUP)]
    selb = sel_ref[0].astype(BF16)
    blk_per_tile = TK // L_SLC
    delta = (lax.broadcasted_iota(jnp.int32, (nsp, TK), 0)
             - lax.shift_right_arithmetic(lax.broadcasted_iota(jnp.int32, (nsp, TK), 1), int(np.log2(L_SLC))))
    kcol = lax.broadcasted_iota(jnp.int32, (1, TK), 1)

    def body(kt, carry):
        k0 = pl.multiple_of(kt * TK, TK)
        k = k_ref[pl.ds(k0, TK), :]
        v = v_ref[pl.ds(k0, TK), :]
        s = _dot_nt(q4, k)
        dist = qposf - (k0 + kcol).astype(F32)
        expand = jnp.where(delta == kt * blk_per_tile, 1.0, 0.0).astype(BF16)
        se = _dot(selb, expand)
        mask_add = jnp.where(jnp.logical_and(se > 0.5, dist >= 0.0), 0.0, NEG_INF)
        return _online_step(carry, s, dist, mask_add, slopes, v)

    ntile = (i * TQ + TQ + TK - 1) // TK
    init = (jnp.full((rows, 1), NEG_INF, F32), jnp.zeros((rows, 1), F32), jnp.zeros((rows, HD), F32))
    _, l, acc = lax.fori_loop(0, ntile, body, init)
    o = acc / l
    for j in range(NSA_GROUP):
        o_ref[:, j * HD:(j + 1) * HD] = o[j * TQ:(j + 1) * TQ]


def _nsa_slc_prompt(q, sel, kvb, slopes, *, TQ, TK):
    t = q.shape[0]
    nsp = sel.shape[-1]
    kb0 = A_KVW // HD
    return pl.pallas_call(
        functools.partial(_nsa_slc_kernel, TQ=TQ, TK=TK),
        grid=(NSA_KV, t // TQ),
        in_specs=[
            pl.BlockSpec(memory_space=pltpu.SMEM),
            pl.BlockSpec((TQ, NSA_GROUP * HD), lambda g, i: (i, g)),
            pl.BlockSpec((1, TQ, nsp), lambda g, i: (g, i, 0)),
            pl.BlockSpec((t, HD), lambda g, i: (0, kb0 + g)),
            pl.BlockSpec((t, HD), lambda g, i: (0, kb0 + NSA_KV + g)),
        ],
        out_specs=pl.BlockSpec((TQ, NSA_GROUP * HD), lambda g, i: (i, g)),
        out_shape=jax.ShapeDtypeStruct((t, A_Q), F32),
        compiler_params=_cp(("parallel", "parallel")),
        name="nsa_slc_prompt",
    )(slopes, q, sel, kvb, kvb)


def _nsa_win_kernel(slopes_ref, q_ref, k_ref, v_ref, ga_ref, ocmp_ref, oslc_ref, o_ref, *, TQ):
    g = pl.program_id(0)
    i = pl.program_id(1)
    rows = NSA_GROUP * TQ
    q4 = (_stack_heads(q_ref[...], TQ) * HD ** -0.5).astype(BF16)
    qposf = (i * TQ + lax.broadcasted_iota(jnp.int32, (TQ, 1), 0)).astype(F32)
    slopes = [slopes_ref[NSA_GROUP * g + j] for j in range(NSA_GROUP)]
    span = min(WINDOW + TQ, k_ref.shape[0])
    k0 = pl.multiple_of(jnp.maximum((i + 1) * TQ - span, 0), TQ)
    kcol = lax.broadcasted_iota(jnp.int32, (1, span), 1)
    k = k_ref[pl.ds(k0, span), :]
    v = v_ref[pl.ds(k0, span), :]
    dist = qposf - (k0 + kcol).astype(F32)
    mask_add = jnp.where(jnp.logical_and(dist >= 0.0, dist <= float(WINDOW)), 0.0, NEG_INF)
    init = (jnp.full((rows, 1), NEG_INF, F32), jnp.zeros((rows, 1), F32), jnp.zeros((rows, HD), F32))
    _, l, acc = _online_step(init, _dot_nt(q4, k), dist, mask_add, slopes, v)
    o_win = acc / l
    gates = jax.nn.sigmoid(ga_ref[...])
    for j in range(NSA_GROUP):
        ca = 3 * j
        cb = 3 * (NSA_GROUP + j)

        def gate(c):
            return jnp.where(g == 0, gates[:, ca + c:ca + c + 1], gates[:, cb + c:cb + c + 1])

        sl = slice(j * HD, (j + 1) * HD)
        o_ref[:, sl] = (gate(0) * ocmp_ref[:, sl] + gate(1) * oslc_ref[:, sl]
                        + gate(2) * o_win[j * TQ:(j + 1) * TQ])


def _nsa_win_prompt(q, kvb, h_in, o_cmp, o_slc, slopes, *, TQ):
    t = q.shape[0]
    kb0 = 2 * A_KVW // HD
    hspec = pl.BlockSpec((TQ, NSA_GROUP * HD), lambda g, i: (i, g))
    return pl.pallas_call(
        functools.partial(_nsa_win_kernel, TQ=TQ),
        grid=(NSA_KV, t // TQ),
        in_specs=[
            pl.BlockSpec(memory_space=pltpu.SMEM),
            hspec,
            pl.BlockSpec((t, HD), lambda g, i: (0, kb0 + g)),
            pl.BlockSpec((t, HD), lambda g, i: (0, kb0 + NSA_KV + g)),
            pl.BlockSpec((TQ, LANE), lambda g, i: (i, COL_SM // LANE)),
            hspec,
            hspec,
        ],
        out_specs=hspec,
        out_shape=jax.ShapeDtypeStruct((t, A_Q), F32),
        compiler_params=_cp(("parallel", "parallel")),
        name="nsa_win_prompt",
    )(slopes, q, kvb, kvb, h_in, o_cmp, o_slc)


def _rows8(row, width):
    return jnp.concatenate([row[:, h * width:(h + 1) * width] for h in range(NSA_HEADS)], axis=0)


def _kv_rows8(kn, off):
    return jnp.concatenate(
        [kn[:, off + (h // NSA_GROUP) * HD: off + (h // NSA_GROUP + 1) * HD] for h in range(NSA_HEADS)], axis=0)


def _slope8(slopes_ref):
    hrow = lax.broadcasted_iota(jnp.int32, (NSA_HEADS, 1), 0)
    out = jnp.zeros((NSA_HEADS, 1), F32)
    for h in range(NSA_HEADS):
        out = jnp.where(hrow == h, slopes_ref[h], out)
    return out


def _bf(x):
    return x.astype(BF16).astype(F32)


def _nsa_slc_dec_kernel(pt_ref, slopes_ref, *refs, G, past):
    del pt_ref
    pages = refs[:G]
    q_ref, sel_ref, knew_ref, o_ref, kv_s, m_s, l_s, acc_s = refs[G:]
    pg = pl.program_id(1)
    nsp = sel_ref.shape[-1]
    tk = G * PAGE_ROWS
    q8 = _rows8(q_ref[0], HD) * HD ** -0.5
    q8b = q8.astype(BF16)
    hrow = lax.broadcasted_iota(jnp.int32, (NSA_HEADS, 1), 0)
    first_group = hrow < NSA_GROUP

    @pl.when(pg == 0)
    def _():
        kn = knew_ref[0]
        m_s[...] = jnp.sum(_bf(q8) * _bf(_kv_rows8(kn, 0)), axis=-1, keepdims=True)
        l_s[...] = jnp.ones_like(l_s)
        acc_s[...] = _bf(_kv_rows8(kn, NSA_KV * HD))

    for j in range(G):
        for sg in range(KV_PARTS):
            kv_s[sg, j * PAGE_ROWS:(j + 1) * PAGE_ROWS, :] = pages[j][0, pl.ds(sg, PAGE_ROWS, stride=KV_PARTS), :]
    kpos = pg * tk + lax.broadcasted_iota(jnp.int32, (1, tk), 1)
    dist = (past - kpos).astype(F32)
    srow = lax.broadcasted_iota(jnp.int32, (nsp, tk), 0)
    scol = (pg * tk + lax.broadcasted_iota(jnp.int32, (nsp, tk), 1)) // L_SLC
    expand = jnp.where(srow == scol, 1.0, 0.0).astype(BF16)
    sel2 = sel_ref[0]
    sel8 = jnp.where(first_group, sel2[0:1, :], sel2[1:2, :]).astype(BF16)
    mask = _dot(sel8, expand) > 0.5
    s8 = jnp.where(first_group, _dot_nt(q8b, kv_s[0].astype(BF16)), _dot_nt(q8b, kv_s[1].astype(BF16)))
    lg = jnp.where(mask, s8 - _slope8(slopes_ref) * dist, NEG_INF)
    m = m_s[...]
    m_new = jnp.maximum(m, jnp.max(lg, axis=-1, keepdims=True))
    p = jnp.where(mask, jnp.exp(lg - m_new), 0.0)
    a = jnp.exp(m - m_new)
    pb = p.astype(BF16)
    pv = jnp.where(first_group, _dot(pb, kv_s[2].astype(BF16)), _dot(pb, kv_s[3].astype(BF16)))
    m_s[...] = m_new
    l_s[...] = a * l_s[...] + jnp.sum(p, axis=-1, keepdims=True)
    acc_s[...] = a * acc_s[...] + pv

    @pl.when(pg == pl.num_programs(1) - 1)
    def _():
        o = acc_s[...] / l_s[...]
        for h in range(NSA_HEADS):
            o_ref[0, :, h * HD:(h + 1) * HD] = o[h:h + 1, :]


def _nsa_slc_sample(q, sel, knew, pool, page_table, slopes):
    b, n_pages = page_table.shape
    G = _tile(n_pages, (8, 4, 2, 1))
    nsp = sel.shape[-1]
    past = n_pages * PAGE_ROWS

    def page_map(j):
        return lambda bi, p, pt: (pt[bi, p * G + j], 0, 0)

    in_specs = [pl.BlockSpec(memory_space=pltpu.SMEM)]
    in_specs += [pl.BlockSpec((1, PAGE_SUBROWS, HD), page_map(j)) for j in range(G)]
    in_specs += [
        pl.BlockSpec((1, 1, A_Q), lambda bi, p, pt: (bi, 0, 0)),
        pl.BlockSpec((1, NSA_KV, nsp), lambda bi, p, pt: (bi, 0, 0)),
        pl.BlockSpec((1, 1, A_KVW), lambda bi, p, pt: (bi, 0, 0)),
    ]
    grid_spec = pltpu.PrefetchScalarGridSpec(
        num_scalar_prefetch=1,
        grid=(b, n_pages // G),
        in_specs=in_specs,
        out_specs=pl.BlockSpec((1, 1, A_Q), lambda bi, p, pt: (bi, 0, 0)),
        scratch_shapes=[pltpu.VMEM((KV_PARTS, G * PAGE_ROWS, HD), F32), pltpu.VMEM((NSA_HEADS, 1), F32),
                        pltpu.VMEM((NSA_HEADS, 1), F32), pltpu.VMEM((NSA_HEADS, HD), F32)],
    )
    return pl.pallas_call(
        functools.partial(_nsa_slc_dec_kernel, G=G, past=past),
        grid_spec=grid_spec,
        out_shape=jax.ShapeDtypeStruct((b, 1, A_Q), F32),
        compiler_params=_cp(("parallel", "arbitrary")),
        name="nsa_slc_sample",
    )(page_table, slopes, *([pool] * G), q, sel, knew)


def _nsa_win_dec_kernel(slopes_ref, q_ref, wb_ref, knew_ref, ga_ref, ocmp_ref, oslc_ref, o_ref, *, past):
    nb = wb_ref.shape[1] // KV_PARTS
    q8 = _rows8(q_ref[0], HD) * HD ** -0.5
    q8b = q8.astype(BF16)
    hrow = lax.broadcasted_iota(jnp.int32, (NSA_HEADS, 1), 0)
    first_group = hrow < NSA_GROUP
    kn = knew_ref[0]
    wb = [wb_ref[0, pl.ds(sg, nb, stride=KV_PARTS), :].astype(BF16) for sg in range(KV_PARTS)]
    kwpos = past - nb + lax.broadcasted_iota(jnp.int32, (1, nb), 1)
    wd = past - kwpos
    mask = jnp.logical_and(jnp.logical_and(wd >= 0, wd <= WINDOW), kwpos >= 0)
    s8 = jnp.where(first_group, _dot_nt(q8b, wb[0]), _dot_nt(q8b, wb[1]))
    lg = jnp.where(mask, s8 - _slope8(slopes_ref) * wd.astype(F32), NEG_INF)
    s_self = jnp.sum(_bf(q8) * _bf(_kv_rows8(kn, 0)), axis=-1, keepdims=True)
    m = jnp.maximum(jnp.max(lg, axis=-1, keepdims=True), s_self)
    p = jnp.where(mask, jnp.exp(lg - m), 0.0)
    p_self = jnp.exp(s_self - m)
    pb = p.astype(BF16)
    pv = jnp.where(first_group, _dot(pb, wb[2]), _dot(pb, wb[3]))
    pv = pv + _bf(p_self) * _bf(_kv_rows8(kn, NSA_KV * HD))
    o_win = pv / (jnp.sum(p, axis=-1, keepdims=True) + p_self)
    gates = jax.nn.sigmoid(ga_ref[0])
    ocmp = ocmp_ref[0]
    oslc = oslc_ref[0]
    for h in range(NSA_HEADS):
        sl = slice(h * HD, (h + 1) * HD)
        o_ref[0, :, sl] = (gates[:, 3 * h:3 * h + 1] * ocmp[:, sl] + gates[:, 3 * h + 1:3 * h + 2] * oslc[:, sl]
                           + gates[:, 3 * h + 2:3 * h + 3] * o_win[h:h + 1, :])


def _nsa_win_sample(q, win_buf, knew, ga, o_cmp, o_slc, slopes, *, past):
    b = q.shape[0]
    nsub = win_buf.shape[1]
    row = lambda w: pl.BlockSpec((1, 1, w), lambda bi: (bi, 0, 0))
    return pl.pallas_call(
        functools.partial(_nsa_win_dec_kernel, past=past),
        grid=(b,),
        in_specs=[pl.BlockSpec(memory_space=pltpu.SMEM), row(A_Q),
                  pl.BlockSpec((1, nsub, HD), lambda bi: (bi, 0, 0)), row(A_KVW), row(LANE), row(A_Q), row(A_Q)],
        out_specs=row(A_Q),
        out_shape=jax.ShapeDtypeStruct((b, 1, A_Q), F32),
        compiler_params=_cp(("parallel",)),
        name="nsa_win_sample",
    )(slopes, q, win_buf, knew, ga, o_cmp, o_slc)


def _dn_conv_kernel(x_ref, prev_ref, buf_ref, w_ref, o_ref, hist_ref):
    c = pl.program_id(0)
    t = pl.program_id(1)
    tt = x_ref.shape[0]
    hist_ref[0:8, :] = jnp.where(t == 0, buf_ref[...], prev_ref[...])
    hist_ref[8:8 + tt, :] = x_ref[...]
    w = w_ref[...]
    y = jnp.zeros((tt, LANE), F32)
    for i in range(CONV_W):
        y = y + w[i:i + 1, :] * hist_ref[8 - (CONV_W - 1) + i: 8 - (CONV_W - 1) + i + tt, :]
    y = _silu(y)
    nrm = y * lax.rsqrt(jnp.sum(y * y, axis=-1, keepdims=True) + 1e-6)
    o_ref[...] = jnp.where(c < DN_HEADS, nrm * DK ** -0.5, jnp.where(c < 2 * DN_HEADS, nrm, y))


def _dn_conv_prompt(h_in, buf8, conv_wt):
    t = h_in.shape[0]
    tt = _tile(t, (1024, 512, 256, 128, 64, 32, 16, 8))
    c0 = COL_DQKV // LANE
    return pl.pallas_call(
        _dn_conv_kernel,
        grid=(DN_QKV // LANE, t // tt),
        in_specs=[
            pl.BlockSpec((tt, LANE), lambda c, i: (i, c0 + c)),
            pl.BlockSpec((8, LANE), lambda c, i: (jnp.maximum(i * (tt // 8) - 1, 0), c0 + c)),
            pl.BlockSpec((8, LANE), lambda c, i: (0, c)),
            pl.BlockSpec((CONV_W, LANE), lambda c, i: (0, c)),
        ],
        out_specs=pl.BlockSpec((tt, LANE), lambda c, i: (i, c)),
        out_shape=jax.ShapeDtypeStruct((t, DN_QKV), F32),
        scratch_shapes=[pltpu.VMEM((tt + 8, LANE), F32)],
        compiler_params=_cp(("parallel", "parallel")),
        name="dn_conv",
    )(h_in, h_in, buf8, conv_wt)


def _dot1(x, y):
    return _dot(x.astype(BF16), y.astype(BF16))


def _unit_lower_inverse_many(mats, n, mm):
    ri = lax.broadcasted_iota(jnp.int32, (n, n), 0)
    ci = lax.broadcasted_iota(jnp.int32, (n, n), 1)
    eye = jnp.where(ri == ci, 1.0, 0.0).astype(F32)
    base = min(16, n)

    def same_block(b):
        return (ri // b) == (ci // b)

    ps = [jnp.where(same_block(base), -a, 0.0) for a in mats]
    rs = [eye + p for p in ps]
    for _ in range(int(np.log2(base)) - 1):
        ps = [mm(p, p) for p in ps]
        rs = [r + mm(r, p) for r, p in zip(rs, ps)]
    b = base
    while b < n:
        offm = jnp.logical_and(same_block(2 * b), jnp.logical_not(same_block(b)))
        ts = [mm(jnp.where(offm, a, 0.0), r) for a, r in zip(mats, rs)]
        rs = [r - mm(r, t) for r, t in zip(rs, ts)]
        b *= 2
    return rs


def _dn_chunk_kernel(q_ref, k_ref, v_ref, sm_ref, smt_ref, z_ref, prow_ref, pcol_ref, nw_ref,
                     o_ref, s_out_ref, s_ref):
    c = pl.program_id(0)
    C = q_ref.shape[0]

    @pl.when(c == 0)
    def _():
        s_ref[...] = jnp.zeros_like(s_ref)

    ri = lax.broadcasted_iota(jnp.int32, (C, C), 0)
    ci = lax.broadcasted_iota(jnp.int32, (C, C), 1)
    lower = ri >= ci
    tril = jnp.where(lower, 1.0, 0.0).astype(F32)
    triu = jnp.where(ri <= ci, 1.0, 0.0).astype(F32)
    ri2 = lax.broadcasted_iota(jnp.int32, (2 * C, C), 0)
    ci2 = lax.broadcasted_iota(jnp.int32, (2 * C, C), 1)
    mask2 = jnp.where(ri2 < C, ri2, ri2 - (C - 1)) > ci2
    sm = sm_ref[...]
    smt = smt_ref[...]
    g_cols = -jnp.exp(prow_ref[0:1, :]) * _softplus(sm + prow_ref[1:2, :])
    g_rows = -jnp.exp(pcol_ref[:, 0:1]) * _softplus(smt + pcol_ref[:, 1:2])
    gcum_cols = _dot(tril, g_cols, HI)
    gcum_rows = _dot(g_rows, triu, HI)
    beta_cols = jax.nn.sigmoid(sm)
    nw = nw_ref[...]
    heads = []
    for h in range(DN_HEADS):
        sl = slice(h * DK, (h + 1) * DK)
        q = q_ref[:, sl]
        k = k_ref[:, sl]
        gc = gcum_cols[:, SM_A + h:SM_A + h + 1]
        gr = gcum_rows[SM_A + h:SM_A + h + 1, :]
        beta = beta_cols[:, SM_BETA + h:SM_BETA + h + 1]
        decay = jnp.where(lower, jnp.exp(jnp.where(lower, gc - gr, 0.0)), 0.0)
        kb = k * beta
        egc = jnp.exp(gc)
        gl = gc[C - 1:C, :]
        heads.append(dict(
            h=h, sl=sl, kb16=k.astype(BF16), decay2=jnp.concatenate([decay, decay], axis=0),
            kbq=jnp.concatenate([kb, q], axis=0).astype(BF16),
            rhs=jnp.concatenate([v_ref[:, sl] * beta, kb * egc], axis=1).astype(BF16),
            qg=q * egc, kdec=(k * jnp.exp(gl - gc)).astype(BF16), egl=jnp.exp(gl)))
    for d in heads:
        d["aa"] = jnp.where(mask2, _dot_nt(d["kbq"], d["kb16"]) * d["decay2"], 0.0)
    tinv = _unit_lower_inverse_many([d["aa"][:C] for d in heads], C, _dot1)
    for d, ti in zip(heads, tinv):
        d["sol"] = _dot(ti.astype(BF16), d["rhs"])
    for d in heads:
        d["s"] = s_ref[d["h"]]
        d["ks2"] = _dot1(jnp.concatenate([d["sol"][:, DV:], d["qg"]], axis=0), d["s"])
    for d in heads:
        d["v_new"] = (d["sol"][:, :DV] - d["ks2"][:C]).astype(BF16)
    for d in heads:
        s_ref[d["h"]] = d["s"] * d["egl"] + _dot_tn(d["kdec"], d["v_new"])
    for d in heads:
        o = d["ks2"][C:] + _dot(d["aa"][C:].astype(BF16), d["v_new"])
        o = o * lax.rsqrt(jnp.mean(o * o, axis=-1, keepdims=True) + RMS_EPS) * nw
        o_ref[:, d["sl"]] = o * _silu(z_ref[:, d["sl"]])

    @pl.when(c == pl.num_programs(0) - 1)
    def _():
        s_out_ref[...] = s_ref[...]


def _dn_chunk_prompt(qkvn, h_in, smt, prow, pcol, norm_w, *, C):
    t = qkvn.shape[0]
    hw = DN_HEADS * DK
    return pl.pallas_call(
        _dn_chunk_kernel,
        grid=(t // C,),
        in_specs=[
            pl.BlockSpec((C, hw), lambda c: (c, 0)),
            pl.BlockSpec((C, hw), lambda c: (c, 1)),
            pl.BlockSpec((C, hw), lambda c: (c, 2)),
            pl.BlockSpec((C, LANE), lambda c: (c, COL_SM // LANE)),
            pl.BlockSpec((LANE, C), lambda c: (0, c)),
            pl.BlockSpec((C, hw), lambda c: (c, COL_Z // hw)),
            pl.BlockSpec((2, LANE), lambda c: (0, 0)),
            pl.BlockSpec((LANE, 2), lambda c: (0, 0)),
            pl.BlockSpec((1, DV), lambda c: (0, 0)),
        ],
        out_specs=[pl.BlockSpec((C, hw), lambda c: (c, 0)),
                   pl.BlockSpec((DN_HEADS, DK, DV), lambda c: (0, 0, 0))],
        out_shape=[jax.ShapeDtypeStruct((t, hw), F32), jax.ShapeDtypeStruct((DN_HEADS, DK, DV), F32)],
        scratch_shapes=[pltpu.VMEM((DN_HEADS, DK, DV), F32)],
        compiler_params=_cp(("arbitrary",)),
        name="dn_chunk",
    )(qkvn, qkvn, qkvn, h_in, smt, h_in, prow, pcol, norm_w)


def _row_to_col(row, n):
    ri = lax.broadcasted_iota(jnp.int32, (n, n), 0)
    ci = lax.broadcasted_iota(jnp.int32, (n, n), 1)
    return jnp.sum(jnp.where(ri == ci, jnp.broadcast_to(row, (n, n)), 0.0), axis=1, keepdims=True)


def _col_to_row(col, n):
    ri = lax.broadcasted_iota(jnp.int32, (n, n), 0)
    ci = lax.broadcasted_iota(jnp.int32, (n, n), 1)
    return jnp.sum(jnp.where(ri == ci, jnp.broadcast_to(col, (n, n)), 0.0), axis=0, keepdims=True)


def _dn_dec_kernel(buf_ref, xq_ref, xk_ref, xv_ref, w_ref, sm_ref, z_ref, prow_ref, nw_ref, s0_ref,
                   o_ref, s_out_ref):
    hw = DN_HEADS * DK
    buf = buf_ref[0]
    w = w_ref[...]
    sm = sm_ref[0]
    g_row = -jnp.exp(prow_ref[0:1, :]) * _softplus(sm + prow_ref[1:2, :])
    beta_row = jax.nn.sigmoid(sm)
    nw = nw_ref[...]
    z = z_ref[0]
    parts = []
    for part, x_ref in enumerate((xq_ref, xk_ref, xv_ref)):
        sl = slice(part * hw, (part + 1) * hw)
        y = w[CONV_W - 1:CONV_W, sl] * x_ref[0]
        for i in range(CONV_W - 1):
            y = y + w[i:i + 1, sl] * buf[i:i + 1, sl]
        parts.append(_silu(y))
    for h in range(DN_HEADS):
        sl = slice(h * DK, (h + 1) * DK)
        q = parts[0][:, sl]
        k = parts[1][:, sl]
        v = parts[2][:, sl]
        q = q * lax.rsqrt(jnp.sum(q * q, axis=-1, keepdims=True) + 1e-6) * DK ** -0.5
        k = k * lax.rsqrt(jnp.sum(k * k, axis=-1, keepdims=True) + 1e-6)
        a = jnp.exp(g_row[:, SM_A + h:SM_A + h + 1])
        beta = beta_row[:, SM_BETA + h:SM_BETA + h + 1]
        k_col = _row_to_col(k, DK)
        q_col = _row_to_col(q, DK)
        s0 = s0_ref[0, h]
        u = beta * (v - a * jnp.sum(s0 * k_col, axis=0, keepdims=True))
        s_new = a * s0 + k_col * u
        s_out_ref[0, h] = s_new
        o = jnp.sum(s_new * q_col, axis=0, keepdims=True)
        o = o * lax.rsqrt(jnp.mean(o * o, axis=-1, keepdims=True) + RMS_EPS) * nw
        o_ref[0, :, sl] = o * _silu(z[:, sl])


def _dn_sample(conv_buf, h3, conv_wt, prow, norm_w, s0):
    b = h3.shape[0]
    hw = DN_HEADS * DK
    c0 = COL_DQKV // hw
    row = lambda w, j: pl.BlockSpec((1, 1, w), lambda bi: (bi, 0, j))
    return pl.pallas_call(
        _dn_dec_kernel,
        grid=(b,),
        in_specs=[
            pl.BlockSpec((1, CONV_W - 1, DN_QKV), lambda bi: (bi, 0, 0)),
            row(hw, c0), row(hw, c0 + 1), row(hw, c0 + 2),
            pl.BlockSpec((CONV_W, DN_QKV), lambda bi: (0, 0)),
            row(LANE, COL_SM // LANE),
            row(hw, COL_Z // hw),
            pl.BlockSpec((2, LANE), lambda bi: (0, 0)),
            pl.BlockSpec((1, DV), lambda bi: (0, 0)),
            pl.BlockSpec((1, DN_HEADS, DK, DV), lambda bi: (bi, 0, 0, 0)),
        ],
        out_specs=[row(hw, 0), pl.BlockSpec((1, DN_HEADS, DK, DV), lambda bi: (bi, 0, 0, 0))],
        out_shape=[jax.ShapeDtypeStruct((b, 1, hw), F32), jax.ShapeDtypeStruct(s0.shape, F32)],
        compiler_params=_cp(("parallel",)),
        name="dn_sample",
    )(conv_buf, h3, h3, h3, conv_wt, h3, h3, prow, norm_w, s0)


def _rwkv_prep(r, k, wl, al, w0, a0, k_k, k_a):
    w_log = -_softplus(-(w0 + wl)) - 0.5
    log_decay = -jnp.exp(w_log)
    a = jax.nn.sigmoid(a0 + al)
    kk_raw = k * k_k
    k_h = k * (1.0 + (a - 1.0) * k_a)
    del r
    return log_decay, a, kk_raw, k_h


def _rwkv_chunk_kernel(r_ref, k_ref, v_ref, wl_ref, al_ref, gate_ref, prm_ref, o_ref, s_out_ref, s_ref):
    c = pl.program_id(1)
    PP = r_ref.shape[0]
    C = r_ref.shape[1]
    N = RWKV_HS
    hpp = LANE // N

    @pl.when(c == 0)
    def _():
        s_ref[...] = jnp.zeros_like(s_ref)

    ri = lax.broadcasted_iota(jnp.int32, (2 * C, C), 0)
    ci = lax.broadcasted_iota(jnp.int32, (2 * C, C), 1)
    mask2 = jnp.where(ri < C, ri, ri - (C - 1)) > ci
    trilb = jnp.where(lax.broadcasted_iota(jnp.int32, (C, C), 0) >= lax.broadcasted_iota(jnp.int32, (C, C), 1),
                      1.0, 0.0).astype(BF16)
    heads = []
    for pp in range(PP):
        prm = prm_ref[pp]
        r2 = r_ref[pp]
        v2 = v_ref[pp]
        log_decay2, a2, kk_raw2, kh2 = _rwkv_prep(r2, k_ref[pp], wl_ref[pp], al_ref[pp],
                                                  prm[0:1], prm[1:2], prm[2:3], prm[3:4])
        ld_h = log_decay2.astype(BF16)
        rem = log_decay2 - ld_h.astype(F32)
        ld_m = rem.astype(BF16)
        ld_l = (rem - ld_m.astype(F32)).astype(BF16)
        gcum2 = _dot(trilb, ld_h) + (_dot(trilb, ld_m) + _dot(trilb, ld_l))
        for hh in range(hpp):
            sl = slice(hh * N, (hh + 1) * N)
            r = r2[:, sl]
            v = v2[:, sl]
            k_h = kh2[:, sl]
            kk = kk_raw2[:, sl]
            kk = kk / jnp.maximum(jnp.sqrt(jnp.sum(kk * kk, axis=-1, keepdims=True)), 1e-12)
            gc = gcum2[:, sl]
            p_incl = jnp.exp(gc)
            p_inv = jnp.exp(-gc)
            at = -kk * jnp.exp(gc - log_decay2[:, sl])
            bt = kk * a2[:, sl] * p_inv
            kt = k_h * p_inv
            rt = r * p_incl
            heads.append(dict(
                idx=pp * hpp + hh, v=v, p_last=p_incl[C - 1:C, :],
                lhs=jnp.concatenate([at, rt], axis=0).astype(BF16),
                bk=jnp.concatenate([bt, kt], axis=0).astype(BF16),
                bonus=jnp.sum(r * k_h * prm[4:5, sl], axis=-1, keepdims=True) * v,
                ln_w=prm[5:6, sl], ln_b=prm[6:7, sl]))
    for h in heads:
        h["g_b"] = jnp.where(mask2, _dot_nt(h["lhs"], h["bk"][:C]), 0.0)
    tinv = _unit_lower_inverse_many([-h["g_b"][:C] for h in heads], C, _dot1)
    for h in heads:
        h["g_k"] = jnp.where(mask2, _dot_nt(h["lhs"], h["bk"][C:]), 0.0)
    for h in heads:
        h["w_kv"] = _dot1(h["g_k"], h["v"])
    for h in heads:
        h["s0"] = s_ref[h["idx"]]
        h["g_s"] = _dot_nt(h["lhs"], h["s0"].astype(BF16))
    for h, ti in zip(heads, tinv):
        h["u"] = _dot1(ti, h["g_s"][:C] + h["w_kv"][:C])
    for h in heads:
        uv = jnp.concatenate([h["u"], h["v"]], axis=0).astype(BF16)
        s_ref[h["idx"]] = (h["s0"] + _dot_tn(uv, h["bk"])) * h["p_last"]
    outs = []
    for h in heads:
        out = h["g_s"][C:] + h["w_kv"][C:] + _dot1(h["g_b"][C:], h["u"])
        mu = jnp.mean(out, axis=-1, keepdims=True)
        d = out - mu
        var = jnp.mean(d * d, axis=-1, keepdims=True)
        outs.append(d * lax.rsqrt(var + GN_EPS) * h["ln_w"] + h["ln_b"] + h["bonus"])
    for pp in range(PP):
        o_ref[pp] = jnp.concatenate(outs[pp * hpp:(pp + 1) * hpp], axis=1) * gate_ref[pp]

    @pl.when(c == pl.num_programs(1) - 1)
    def _():
        s_out_ref[...] = s_ref[...]


def _rwkv_chunk_prompt(r, k, v, wl, al, gate, prm, *, C, PP):
    npair, t, _ = r.shape
    hpp = LANE // RWKV_HS
    seq = pl.BlockSpec((PP, C, LANE), lambda p, c: (p, c, 0))
    return pl.pallas_call(
        _rwkv_chunk_kernel,
        grid=(npair // PP, t // C),
        in_specs=[seq] * 6 + [pl.BlockSpec((PP, 8, LANE), lambda p, c: (p, 0, 0))],
        out_specs=[seq, pl.BlockSpec((PP * hpp, RWKV_HS, RWKV_HS), lambda p, c: (p, 0, 0))],
        out_shape=[jax.ShapeDtypeStruct((npair, t, LANE), F32),
                   jax.ShapeDtypeStruct((npair * hpp, RWKV_HS, RWKV_HS), F32)],
        scratch_shapes=[pltpu.VMEM((PP * hpp, RWKV_HS, RWKV_HS), F32)],
        compiler_params=_cp(("parallel", "arbitrary")),
        name="rwkv_chunk",
    )(r, k, v, wl, al, gate, prm)


def _rwkv_dec_kernel(r_ref, k_ref, v_ref, wl_ref, al_ref, gate_ref, prm_ref, s0_ref, o_ref, s_out_ref):
    N = RWKV_HS
    prm = prm_ref[...]
    r2 = r_ref[0]
    k2 = k_ref[0]
    v2 = v_ref[0]
    log_decay2, a2, kk_raw2, kh2 = _rwkv_prep(r2, k2, wl_ref[0], al_ref[0], prm[0:1], prm[1:2], prm[2:3], prm[3:4])
    w2 = jnp.exp(log_decay2)
    gate = gate_ref[0]
    nh = r2.shape[1] // N
    sls = [slice(h * N, (h + 1) * N) for h in range(nh)]
    lane_sum = lambda xs: [jnp.sum(x, axis=-1, keepdims=True) for x in xs]
    kks = [kk_raw2[:, sl] for sl in sls]
    kks = [kk / jnp.maximum(jnp.sqrt(n2), 1e-12) for kk, n2 in zip(kks, lane_sum([kk * kk for kk in kks]))]
    s0s = [s0_ref[0, h] for h in range(nh)]
    sas = lane_sum([s0 * (-kk) for s0, kk in zip(s0s, kks)])
    vcols = [_row_to_col(v2[:, sl], N) for sl in sls]
    s_news = [s0 * w2[:, sl] + sa * (kk * a2[:, sl]) + vc * kh2[:, sl]
              for s0, sl, sa, kk, vc in zip(s0s, sls, sas, kks, vcols)]
    for h in range(nh):
        s_out_ref[0, h] = s_news[h]
    outs = [_col_to_row(oc, N) for oc in lane_sum([sn * r2[:, sl] for sn, sl in zip(s_news, sls)])]
    mus = lane_sum(outs)
    ds = [o - mu * (1.0 / N) for o, mu in zip(outs, mus)]
    vars_ = lane_sum([d * d for d in ds])
    bon = lane_sum([r2[:, sl] * kh2[:, sl] * prm[4:5, sl] for sl in sls])
    for sl, d, var, bo in zip(sls, ds, vars_, bon):
        gn = d * lax.rsqrt(var * (1.0 / N) + GN_EPS) * prm[5:6, sl] + prm[6:7, sl]
        o_ref[0, :, sl] = (gn + bo * v2[:, sl]) * gate[:, sl]


def _rwkv_sample(r, k, v, wl, al, gate, prm, s0):
    b, _, d = r.shape
    row = pl.BlockSpec((1, 1, d), lambda bi: (bi, 0, 0))
    st = pl.BlockSpec((1,) + s0.shape[1:], lambda bi: (bi, 0, 0, 0))
    return pl.pallas_call(
        _rwkv_dec_kernel,
        grid=(b,),
        in_specs=[row] * 6 + [pl.BlockSpec((8, d), lambda bi: (0, 0)), st],
        out_specs=[row, st],
        out_shape=[jax.ShapeDtypeStruct((b, 1, d), F32), jax.ShapeDtypeStruct(s0.shape, F32)],
        compiler_params=_cp(("parallel",)),
        name="rwkv_sample",
    )(r, k, v, wl, al, gate, prm, s0)


def _alibi_slopes():
    return jnp.asarray(2.0 ** (-8.0 * np.arange(1, NSA_HEADS + 1) / NSA_HEADS), dtype=F32)


def _overlap_matrix(nch, nsp):
    cstart = np.arange(nch)[:, None] * D_CMP
    sstart = np.arange(nsp)[None, :] * L_SLC
    return jnp.asarray(((cstart < sstart + L_SLC) & (cstart + L_CMP > sstart)).astype(np.float32))


def _pack_w_in(w):
    offs = np.concatenate([[0], np.cumsum(EVEN_SPLIT)])
    qa, kvc, kvs, kvw, ga, qkv, z, b, a = [w[:, offs[i]:offs[i + 1]] for i in range(len(EVEN_SPLIT))]
    used = COL_SM + A_GATES + 2 * DN_HEADS
    pad = jnp.zeros((w.shape[0], E_IN_PAD - used), w.dtype)
    return jnp.concatenate([qa, qkv, z, kvc, kvs, kvw, ga, b, a, pad], axis=1).astype(BF16)


def _pack_cmp_w1(w1):
    hid = w1.shape[-1]
    w = w1.reshape(2, 2, D_CMP, HD, hid).transpose(0, 2, 3, 1, 4)
    return w.reshape(2, D_CMP, HD, 2 * hid).astype(BF16)


def _pack_cmp_pe(pe):
    p = pe.reshape(2, 2, D_CMP, HD).transpose(0, 2, 1, 3)
    return jnp.concatenate([p, jnp.zeros((2, D_CMP, 6, HD), pe.dtype)], axis=2)


def _dn_gate_params(a_log, dt_bias):
    row = jnp.zeros((2, LANE), F32).at[0, SM_A:SM_A + DN_HEADS].set(a_log).at[1, SM_A:SM_A + DN_HEADS].set(dt_bias)
    return row, row.T


def _even_layer(xp, xs, w_in, w_out, pe, w1, b1, w2, conv_w, a_log, dt_bias, norm_w,
                cache_cmp, cache_slc, win_buf, conv_buf, dn_s0, page_table, g, b):
    t = xp.shape[0]
    bs = xs.shape[0]
    n_pages = page_table.shape[1]
    past = n_pages * PAGE_ROWS
    slopes = _alibi_slopes()
    w_in_p = _pack_w_in(w_in)
    w4 = _pack_cmp_w1(w1)
    pe8 = _pack_cmp_pe(pe)
    w2b = w2.astype(BF16)
    conv_wt = conv_w.T
    prow, pcol = _dn_gate_params(a_log, dt_bias)
    nw = norm_w.reshape(1, DV)

    hp = _mm(xp, w_in_p)
    hs = _mm(xs, w_in_p)
    kvc_p = hp[:, COL_KVC:COL_KVC + A_KVW]
    kvs_p = hp[:, COL_KVS:COL_KVS + A_KVW]
    kvw_p = hp[:, COL_KVW:COL_KVW + A_KVW]
    kvc_s = hs[:, COL_KVC:COL_KVC + A_KVW]
    kvs_s = hs[:, COL_KVS:COL_KVS + A_KVW]
    kvw_s = hs[:, COL_KVW:COL_KVW + A_KVW]

    TQ = _tile(t, (128, 64, 32, 16, 8))
    nch = t // D_CMP
    ns = t // L_SLC
    nsp = -(-ns // LANE) * LANE
    arange_pt = jnp.arange(t // PAGE_ROWS, dtype=jnp.int32)[None]
    h1 = _cmp_stage1(kvc_p.reshape(t // PAGE_ROWS, PAGE_SUBROWS, HD), arange_pt, w4)
    cmp_p = _cmp_stage2(h1, pe8, w4, b1, w2b)
    q_p = hp
    o_cmp, sel = _nsa_select(q_p[None], cmp_p, _overlap_matrix(nch, nsp), slopes,
                             TQ=_tile(t, (256, 128, 64, 32, 16, 8)), NC=(t - L_CMP) // D_CMP + 1, NS=ns, q_off=0,
                             BB=1)
    kvb = hp[:, COL_KVC:COL_KVC + 3 * A_KVW].astype(BF16)
    o_slc = _nsa_slc_prompt(q_p, sel[0], kvb, slopes, TQ=TQ, TK=_tile(t, (512, 256, 128, 64)))
    o_a_p = _nsa_win_prompt(q_p, kvb, hp, o_cmp[0], o_slc, slopes, TQ=TQ)

    nch_s = n_pages * CH_PER_PAGE
    nc_s = (past + 1 - L_CMP) // D_CMP + 1
    ns_s = -(-(past + 1) // L_SLC)
    nsp_s = -(-ns_s // LANE) * LANE
    h1s = _cmp_stage1(cache_cmp.reshape(cache_cmp.shape[0], PAGE_SUBROWS, HD), page_table, w4)
    cmp_s = _cmp_stage2(h1s, pe8, w4, b1, w2b)
    hs3 = hs[:, None, :]
    q_s = hs3[:, :, COL_QA:COL_QA + A_Q]
    o_cmp_s, sel_s = _nsa_select(q_s, cmp_s, _overlap_matrix(nch_s, nsp_s), slopes,
                                 TQ=1, NC=nc_s, NS=ns_s, q_off=past, BB=_tile(bs, (4, 2, 1)))
    o_slc_s = _nsa_slc_sample(q_s, sel_s[:, :, 0, :], kvs_s[:, None, :],
                              cache_slc.reshape(cache_slc.shape[0], PAGE_SUBROWS, HD), page_table, slopes)
    wb = win_buf.reshape(bs, win_buf.shape[1] * KV_PARTS, HD)
    o_a_s = _nsa_win_sample(q_s, wb, kvw_s[:, None, :], hs3[:, :, COL_SM:COL_SM + LANE], o_cmp_s, o_slc_s,
                            slopes, past=past)

    qkvn = _dn_conv_prompt(hp, jnp.zeros((8, DN_QKV), F32), conv_wt)
    smt = hp[:, COL_SM:COL_SM + LANE].T
    o_b_p, dn_s_p = _dn_chunk_prompt(qkvn, hp, smt, prow, pcol, nw, C=_tile(t, (128,)))
    o_b_s, dn_s_s = _dn_sample(conv_buf, hs3, conv_wt, prow, nw, dn_s0)

    w_out_b = w_out.astype(BF16)
    yp = _mm_ln([o_a_p, o_b_p], w_out_b, xp, g, b)
    ys = _mm_ln([o_a_s[:, 0], o_b_s[:, 0]], w_out_b, xs, g, b)

    kv6 = lambda a: a.reshape(a.shape[:-1] + (2, NSA_KV, HD))
    raw_p = hp[:, COL_DQKV:COL_DQKV + DN_QKV]
    raw_s = hs[:, COL_DQKV:COL_DQKV + DN_QKV]
    wlen = min(WINDOW, t)
    outs = dict(
        cmp_p=kv6(kvc_p)[None], cmp_s=kv6(kvc_s)[:, None],
        slc_p=kv6(kvs_p)[None], slc_s=kv6(kvs_s)[:, None],
        win_p=kv6(kvw_p[t - wlen:])[None],
        win_s=jnp.concatenate([win_buf, kv6(kvw_s)[:, None]], axis=1)[:, 1:],
        conv_p=jnp.concatenate([jnp.zeros((CONV_W - 1, DN_QKV), F32), raw_p], axis=0)[t:][None],
        conv_s=jnp.concatenate([conv_buf, raw_s[:, None]], axis=1)[:, 1:],
        dns_p=dn_s_p[None], dns_s=dn_s_s,
    )
    return yp, ys, outs


def _odd_layer(xp, xs, shift_s, s0_s, mix, wr, wk, wv, wo, w0, w1, w2, a0, a1, a2, g1, g2, k_k, k_a, r_k,
               ln_w, ln_b, g, b):
    t, d = xp.shape
    bs = xs.shape[0]
    npair = d // LANE
    first_p = jnp.zeros((8, d), F32)

    def padk(wa, wb_):
        r = wa.shape[1]
        rp = -(-r // LANE) * LANE
        return (jnp.pad(wa, ((0, 0), (0, rp - r))).astype(BF16), jnp.pad(wb_, ((0, rp - r), (0, 0))).astype(BF16))

    wrb, wkb, wvb, wob = (w.astype(BF16) for w in (wr, wk, wv, wo))
    w1b, w2b = padk(w1, w2)
    a1b, a2b = padk(a1, a2)
    g1b, g2b = padk(g1, g2)
    prm = jnp.stack([w0, a0, k_k, k_a, r_k.reshape(d), ln_w, ln_b, jnp.zeros((d,), F32)])

    def proj(x, prev_kw, pair_out):
        mm = lambda w_, i, **kw: _mm(x, w_, mixrow=mix[i:i + 1], **prev_kw, **kw)
        r = mm(wrb, 0, pair_out=pair_out)
        wl = _mm(mm(w1b, 1, act="tanh"), w2b, pair_out=pair_out)
        k = mm(wkb, 2, pair_out=pair_out)
        v = mm(wvb, 3, pair_out=pair_out)
        al = _mm(mm(a1b, 4), a2b, pair_out=pair_out)
        gate = _mm(mm(g1b, 5, act="sigmoid"), g2b, pair_out=pair_out)
        return r, k, v, wl, al, gate

    pp = proj(xp, dict(first8=first_p), True)
    prm_pair = prm.reshape(8, npair, LANE).transpose(1, 0, 2)
    y_p, s_p = _rwkv_chunk_prompt(*pp, prm_pair, C=_tile(t, (64, 32, 16, 8)), PP=16)
    yp = _mm_ln([y_p], wob, xp, g, b, pair_in=True)

    ps = [a[:, None, :] for a in proj(xs, dict(xprev=shift_s), False)]
    y_s, s_s = _rwkv_sample(*ps, prm, s0_s)
    ys = _mm_ln([y_s[:, 0]], wob, xs, g, b)
    return yp, ys, dict(shift_p=xp[t - 1:t], shift_s=xs, rs_p=s_p[None], rs_s=s_s)


def kernel(x_prompt, x_sample, cache_nsa_cmp, cache_nsa_slc, cache_nsa_win, state_dn_conv, state_dn_S, state_rwkv_shift, state_rwkv_S, page_table, ln_g, ln_b, ffn_wi, ffn_wo, mix_w_in, mix_w_out, nsa_cmp_pe, nsa_cmp_w1, nsa_cmp_b1, nsa_cmp_w2, dn_conv_w, dn_a_log, dn_dt_bias, dn_norm_w, rwkv_mix, rwkv_wr, rwkv_wk, rwkv_wv, rwkv_wo, rwkv_w0, rwkv_w1, rwkv_w2, rwkv_a0, rwkv_a1, rwkv_a2, rwkv_g1, rwkv_g2, rwkv_k_k, rwkv_k_a, rwkv_r_k, rwkv_ln_w, rwkv_ln_b):
    bp, t, d = x_prompt.shape
    assert bp == 1 and x_sample.shape[1] == 1
    depth = ffn_wi.shape[0]
    xp = x_prompt[0]
    xs = x_sample[:, 0]
    even, odd = [], []
    wi, wo = ffn_wi, ffn_wo
    for l in range(depth):
        gl = lambda i: (ln_g[l, i][None], ln_b[l, i][None])
        xp = _ffn_ln(xp, wi, wo, l, 0, *gl(0))
        xs = _ffn_ln(xs, wi, wo, l, 0, *gl(0))
        if l % 2 == 0:
            e = l // 2
            xp, xs, o = _even_layer(
                xp, xs, mix_w_in[e], mix_w_out[e], nsa_cmp_pe[e], nsa_cmp_w1[e], nsa_cmp_b1[e], nsa_cmp_w2[e],
                dn_conv_w[e], dn_a_log[e], dn_dt_bias[e], dn_norm_w[e], cache_nsa_cmp[e], cache_nsa_slc[e],
                cache_nsa_win[e], state_dn_conv[e], state_dn_S[e], page_table, *gl(1))
            even.append(o)
        else:
            c = l // 2
            xp, xs, o = _odd_layer(
                xp, xs, state_rwkv_shift[c], state_rwkv_S[c], rwkv_mix[c], rwkv_wr[c], rwkv_wk[c], rwkv_wv[c],
                rwkv_wo[c], rwkv_w0[c], rwkv_w1[c], rwkv_w2[c], rwkv_a0[c], rwkv_a1[c], rwkv_a2[c], rwkv_g1[c],
                rwkv_g2[c], rwkv_k_k[c], rwkv_k_a[c], rwkv_r_k[c], rwkv_ln_w[c], rwkv_ln_b[c], *gl(1))
            odd.append(o)
        xp = _ffn_ln(xp, wi, wo, l, 1, *gl(2))
        xs = _ffn_ln(xs, wi, wo, l, 1, *gl(2))
    st = lambda lst, key: jnp.stack([o[key] for o in lst])
    return (xp[None], xs[:, None],
            st(even, "cmp_p"), st(even, "cmp_s"), st(even, "slc_p"), st(even, "slc_s"),
            st(even, "win_p"), st(even, "win_s"), st(even, "conv_p"), st(even, "conv_s"),
            st(even, "dns_p"), st(even, "dns_s"),
            st(odd, "shift_p"), st(odd, "shift_s"), st(odd, "rs_p"), st(odd, "rs_s"))
```

```python
import functools

import numpy as np
import jax
import jax.numpy as jnp
from jax import lax
from jax.experimental import pallas as pl
from jax.experimental.pallas import tpu as pltpu

F32 = jnp.float32
BF16 = jnp.bfloat16
HI = lax.Precision.HIGHEST

DEPTH = 2
ALPHA = (2 * DEPTH) ** 0.25
LN_EPS = 1e-5
RMS_EPS = 1e-6
NSA_HEADS = 8
NSA_KV = 2
NSA_GROUP = NSA_HEADS // NSA_KV
HD = 128
L_CMP = 32
D_CMP = 16
L_SLC = 64
N_SEL = 16
WINDOW = 512
NEG_INF = -1e30
FORCE_SCORE = 1e6
DN_HEADS = 8
DK = 128
DV = 128
CONV_W = 4
RWKV_HS = 64
GN_EPS = 64e-5

A_Q = NSA_HEADS * HD
A_KVW = 2 * NSA_KV * HD
A_GATES = 3 * NSA_HEADS
DN_QKV = DN_HEADS * (2 * DK + DV)
EVEN_SPLIT = (A_Q, A_KVW, A_KVW, A_KVW, A_GATES, DN_QKV, DN_HEADS * DV, DN_HEADS, DN_HEADS)
COL_QA = 0
COL_DQKV = 1024
COL_Z = 4096
COL_KVC = 5120
COL_KVS = 5632
COL_KVW = 6144
COL_SM = 6656
SM_BETA = A_GATES
SM_A = A_GATES + DN_HEADS
E_IN_PAD = 7168
LANE = 128
PAGE_ROWS = 128

VMEM_LIMIT = 56 * 1024 * 1024


def _cp(sem):
    return pltpu.CompilerParams(dimension_semantics=sem, vmem_limit_bytes=VMEM_LIMIT)


def _tile(n, prefs):
    for t in prefs:
        if n % t == 0:
            return t
    return n


def _dot(a, b, precision=None):
    return jnp.dot(a, b, preferred_element_type=F32, precision=precision)


def _dot_nt(a, b, precision=None):
    return lax.dot_general(a, b, (((1,), (1,)), ((), ())), preferred_element_type=F32, precision=precision)


def _dot_tn(a, b, precision=None):
    return lax.dot_general(a, b, (((0,), (0,)), ((), ())), preferred_element_type=F32, precision=precision)


def _layernorm_rows(y, g, b):
    mu = jnp.mean(y, axis=-1, keepdims=True)
    d = y - mu
    var = jnp.mean(d * d, axis=-1, keepdims=True)
    return d * lax.rsqrt(var + LN_EPS) * g + b


def _softplus(x):
    return jnp.maximum(x, 0.0) + jnp.log1p(jnp.exp(-jnp.abs(x)))


def _silu(x):
    return x * jax.nn.sigmoid(x)


def _ffn_kernel(x_ref, wg_ref, wu_ref, wo_ref, g_ref, b_ref, o_ref, xb_ref, *, nf):
    f = pl.program_id(1)

    @pl.when(f == 0)
    def _():
        xb_ref[...] = x_ref[...].astype(BF16)
        o_ref[...] = jnp.zeros_like(o_ref)

    xb = xb_ref[...]
    gate = _dot(xb, wg_ref[...].astype(BF16))
    up = _dot(xb, wu_ref[...].astype(BF16))
    act = (_silu(gate) * up).astype(BF16)
    o_ref[...] += _dot(act, wo_ref[...].astype(BF16))

    @pl.when(f == nf - 1)
    def _():
        y = ALPHA * x_ref[...] + 0.5 * o_ref[...]
        o_ref[...] = _layernorm_rows(y, g_ref[...], b_ref[...])


def _ffn_ln(x, wi, wo, layer, which, g, b):
    m, d = x.shape
    f = wo.shape[2]
    tm = _tile(m, (1024, 512, 256, 128, 64, 32, 16, 8))
    tf = _tile(f, (256, 128))
    nf = f // tf
    return pl.pallas_call(
        functools.partial(_ffn_kernel, nf=nf),
        grid=(m // tm, nf),
        in_specs=[
            pl.BlockSpec((tm, d), lambda i, j: (i, 0), pipeline_mode=pl.Buffered(1)),
            pl.BlockSpec((None, None, d, tf), lambda i, j: (layer, which, 0, j)),
            pl.BlockSpec((None, None, d, tf), lambda i, j: (layer, which, 0, j + nf)),
            pl.BlockSpec((None, None, tf, d), lambda i, j: (layer, which, j, 0)),
            pl.BlockSpec((1, d), lambda i, j: (0, 0)),
            pl.BlockSpec((1, d), lambda i, j: (0, 0)),
        ],
        out_specs=pl.BlockSpec((tm, d), lambda i, j: (i, 0)),
        out_shape=jax.ShapeDtypeStruct((m, d), F32),
        scratch_shapes=[pltpu.VMEM((tm, d), BF16)],
        compiler_params=_cp(("parallel", "arbitrary")),
        name="ffn_ln",
    )(x, wi, wi, wo, g, b)


def _mm_kernel(*refs, mix, act, pair_out):
    if mix == "array":
        x_ref, xp_ref, m_ref, w_ref, o_ref, xb_ref = refs
    elif mix == "shift":
        x_ref, prev_ref, first_ref, m_ref, w_ref, o_ref, xb_ref, hist_ref = refs
    else:
        x_ref, w_ref, o_ref, xb_ref = refs

    @pl.when(pl.program_id(1) == 0)
    def _():
        x = x_ref[...]
        if mix == "array":
            x = x + (xp_ref[...] - x) * m_ref[...]
        elif mix == "shift":
            tm = x_ref.shape[0]
            hist_ref[0:8, :] = jnp.where(pl.program_id(0) == 0, first_ref[...], prev_ref[...])
            hist_ref[8:8 + tm, :] = x
            x = x + (hist_ref[7:7 + tm, :] - x) * m_ref[...]
        xb_ref[...] = x.astype(BF16)

    y = _dot(xb_ref[...], w_ref[...])
    if act == "tanh":
        y = jnp.tanh(y)
    elif act == "sigmoid":
        y = jax.nn.sigmoid(y)
    if pair_out:
        for p in range(o_ref.shape[0]):
            o_ref[p] = y[:, p * LANE:(p + 1) * LANE]
    else:
        o_ref[...] = y


def _mm(x, w, *, xprev=None, first8=None, mixrow=None, act=None, pair_out=False, tn_prefs=(512, 256, 128)):
    m, k = x.shape
    n = w.shape[1]
    tm = _tile(m, (1024, 512, 256, 128, 64, 32, 16, 8))
    tn = _tile(n, tn_prefs)
    mix = "array" if xprev is not None else ("shift" if first8 is not None else None)
    in_specs = [pl.BlockSpec((tm, k), lambda i, j: (i, 0))]
    args = [x]
    scratch = [pltpu.VMEM((tm, k), BF16)]
    if mix == "array":
        in_specs += [pl.BlockSpec((tm, k), lambda i, j: (i, 0)), pl.BlockSpec((1, k), lambda i, j: (0, 0))]
        args += [xprev, mixrow]
    elif mix == "shift":
        in_specs += [pl.BlockSpec((8, k), lambda i, j: (jnp.maximum(i * (tm // 8) - 1, 0), 0)),
                     pl.BlockSpec((8, k), lambda i, j: (0, 0)), pl.BlockSpec((1, k), lambda i, j: (0, 0))]
        args += [x, first8, mixrow]
        scratch.append(pltpu.VMEM((tm + 8, k), F32))
    in_specs.append(pl.BlockSpec((k, tn), lambda i, j: (0, j)))
    args.append(w)
    if pair_out:
        npb = tn // LANE
        out_spec = pl.BlockSpec((npb, tm, LANE), lambda i, j: (j, i, 0))
        out_shape = jax.ShapeDtypeStruct((n // LANE, m, LANE), F32)
    else:
        out_spec = pl.BlockSpec((tm, tn), lambda i, j: (i, j))
        out_shape = jax.ShapeDtypeStruct((m, n), F32)
    return pl.pallas_call(
        functools.partial(_mm_kernel, mix=mix, act=act, pair_out=pair_out),
        grid=(m // tm, n // tn),
        in_specs=in_specs,
        out_specs=out_spec,
        out_shape=out_shape,
        scratch_shapes=scratch,
        compiler_params=_cp(("parallel", "arbitrary")),
        name="matmul",
    )(*args)


def _mmln_kernel(*refs, pair_in, n_parts):
    a_refs = refs[:n_parts]
    w_ref, x_ref, g_ref, b_ref, o_ref = refs[n_parts:]
    if pair_in:
        a_ref = a_refs[0]
        y = _dot(jnp.concatenate([a_ref[p].astype(BF16) for p in range(a_ref.shape[0])], axis=1), w_ref[...])
    else:
        y = None
        k0 = 0
        for a_ref in a_refs:
            kw = a_ref.shape[1]
            part = _dot(a_ref[...].astype(BF16), w_ref[k0:k0 + kw, :])
            y = part if y is None else y + part
            k0 += kw
    o_ref[...] = _layernorm_rows(ALPHA * x_ref[...] + y, g_ref[...], b_ref[...])


def _mm_ln(a_parts, w, x, g, b, *, pair_in=False):
    m, d = x.shape
    k = w.shape[0]
    tm = _tile(m, (512, 256, 128, 64, 32, 16, 8))
    if pair_in:
        a_specs = [pl.BlockSpec((k // LANE, tm, LANE), lambda i: (0, i, 0))]
    else:
        a_specs = [pl.BlockSpec((tm, a.shape[1]), lambda i: (i, 0)) for a in a_parts]
    return pl.pallas_call(
        functools.partial(_mmln_kernel, pair_in=pair_in, n_parts=len(a_parts)),
        grid=(m // tm,),
        in_specs=a_specs + [
            pl.BlockSpec((k, d), lambda i: (0, 0)),
            pl.BlockSpec((tm, d), lambda i: (i, 0)),
            pl.BlockSpec((1, d), lambda i: (0, 0)),
            pl.BlockSpec((1, d), lambda i: (0, 0)),
        ],
        out_specs=pl.BlockSpec((tm, d), lambda i: (i, 0)),
        out_shape=jax.ShapeDtypeStruct((m, d), F32),
        compiler_params=_cp(("parallel",)),
        name="matmul_ln",
    )(*a_parts, w, x, g, b)


CH_PER_PAGE = PAGE_ROWS // D_CMP
KV_PARTS = 2 * NSA_KV
PAGE_SUBROWS = PAGE_ROWS * KV_PARTS


def _cmp1_kernel(pt_ref, *refs, G):
    del pt_ref
    page_refs = refs[:G]
    w_ref, o_ref = refs[G:]
    rows = G * CH_PER_PAGE
    for sg in range(KV_PARTS):
        s = sg // NSA_KV
        acc = jnp.zeros((rows, 2 * HD), F32)
        for p in range(D_CMP):
            x = jnp.concatenate(
                [page_refs[j][0, pl.ds(p * KV_PARTS + sg, CH_PER_PAGE, stride=D_CMP * KV_PARTS), :]
                 for j in range(G)], axis=0)
            acc = acc + _dot(x.astype(BF16), w_ref[s, p])
        o_ref[0, :, sg * 2 * HD:(sg + 1) * 2 * HD] = acc


def _cmp_stage1(pool, page_table, w4):
    b, n_pages = page_table.shape
    G = _tile(n_pages, (16, 8, 4, 2, 1))

    def page_map(j):
        return lambda bi, p, pt: (pt[bi, p * G + j], 0, 0)

    in_specs = [pl.BlockSpec((1, PAGE_SUBROWS, HD), page_map(j)) for j in range(G)]
    in_specs.append(pl.BlockSpec(w4.shape, lambda bi, p, pt: (0, 0, 0, 0)))
    grid_spec = pltpu.PrefetchScalarGridSpec(
        num_scalar_prefetch=1,
        grid=(b, n_pages // G),
        in_specs=in_specs,
        out_specs=pl.BlockSpec((1, G * CH_PER_PAGE, 8 * HD), lambda bi, p, pt: (bi, p, 0)),
    )
    return pl.pallas_call(
        functools.partial(_cmp1_kernel, G=G),
        grid_spec=grid_spec,
        out_shape=jax.ShapeDtypeStruct((b, n_pages * CH_PER_PAGE, 8 * HD), F32),
        compiler_params=_cp(("parallel", "arbitrary")),
        name="nsa_cmp_stage1",
    )(page_table, *([pool] * G), w4)


def _gelu_tanh(x):
    return 0.5 * x * (1.0 + jnp.tanh(np.sqrt(2.0 / np.pi).astype(np.float32) * (x + 0.044715 * (x * x * x))))


def _cmp2_kernel(h_ref, pe_ref, w4_ref, b1_ref, w2_ref, o_ref):
    nch = h_ref.shape[1]
    for s in range(2):
        pacc = jnp.zeros((8, 2 * HD), F32)
        for p in range(D_CMP):
            pacc = pacc + _dot(pe_ref[s, p].astype(BF16), w4_ref[s, p])
        const = pacc[0:1, 0:HD] + pacc[1:2, HD:2 * HD] + b1_ref[s:s + 1, :]
        for gi in range(NSA_KV):
            sg = s * NSA_KV + gi
            h0 = h_ref[0, :, sg * 2 * HD:sg * 2 * HD + HD]
            h1 = h_ref[0, :, sg * 2 * HD + HD:(sg + 1) * 2 * HD]
            hid = h0 + pltpu.roll(h1, nch - 1, 0) + const
            o_ref[0, :, sg * HD:(sg + 1) * HD] = _dot(_gelu_tanh(hid).astype(BF16), w2_ref[s])


def _cmp_stage2(h, pe8, w4, b1, w2):
    b, nch, _ = h.shape
    return pl.pallas_call(
        _cmp2_kernel,
        grid=(b,),
        in_specs=[
            pl.BlockSpec((1, nch, 8 * HD), lambda i: (i, 0, 0)),
            pl.BlockSpec(pe8.shape, lambda i: (0, 0, 0, 0)),
            pl.BlockSpec(w4.shape, lambda i: (0, 0, 0, 0)),
            pl.BlockSpec(b1.shape, lambda i: (0, 0)),
            pl.BlockSpec(w2.shape, lambda i: (0, 0, 0)),
        ],
        out_specs=pl.BlockSpec((1, nch, A_KVW), lambda i: (i, 0, 0)),
        out_shape=jax.ShapeDtypeStruct((b, nch, A_KVW), F32),
        compiler_params=_cp(("parallel",)),
        name="nsa_cmp_stage2",
    )(h, pe8, w4, b1, w2)


def _nsa_sel_kernel(slopes_ref, q_ref, cmp_ref, ov_ref, ocmp_ref, sel_ref, *, TQ, NC, NS, q_off, n_pick, RS):
    i = pl.program_id(1)
    bb = q_ref.shape[0]
    nch = cmp_ref.shape[1]
    nsp = ov_ref.shape[1]
    qpos = q_off + i * TQ + lax.broadcasted_iota(jnp.int32, (TQ, 1), 0)
    cidx = lax.broadcasted_iota(jnp.int32, (1, nch), 1)
    cstart = cidx * D_CMP
    cmask = jnp.logical_and(cstart + (L_CMP - 1) <= qpos, cidx < NC)
    mask_add = jnp.where(cmask, 0.0, NEG_INF)
    mask_mul = jnp.where(cmask, 1.0, 0.0)
    cdist = (qpos - cstart).astype(F32) - 0.5 * (L_CMP - 1)
    sid = lax.broadcasted_iota(jnp.int32, (1, nsp), 1)
    cur = lax.shift_right_arithmetic(qpos, int(np.log2(L_SLC)))
    svalid = jnp.logical_and(sid * L_SLC <= qpos, sid < NS)
    forced = jnp.logical_or(sid == 0, jnp.logical_or(sid == cur, sid == cur - 1))
    ov = ov_ref[...]
    scores = []
    for bi in range(bb):
        q = q_ref[bi]
        cm = cmp_ref[bi]
        for g in range(NSA_KV):
            kb = cm[:, g * HD:(g + 1) * HD].astype(BF16)
            vb = cm[:, (NSA_KV + g) * HD:(NSA_KV + g + 1) * HD].astype(BF16)
            pcs = jnp.zeros((TQ, nch), F32)
            for j in range(NSA_GROUP):
                h = NSA_GROUP * g + j
                qj = (q[:, h * HD:(h + 1) * HD] * HD ** -0.5).astype(BF16)
                lg = _dot_nt(qj, kb) - slopes_ref[h] * cdist + mask_add
                e = jnp.exp(lg - jnp.max(lg, axis=-1, keepdims=True))
                p = e * (mask_mul * (1.0 / jnp.sum(e, axis=-1, keepdims=True)))
                pcs = pcs + p
                ocmp_ref[bi, :, h * HD:(h + 1) * HD] = _dot(p.astype(BF16), vb)
            imp = _dot(pcs, ov, precision=HI)
            scores.append(jnp.where(svalid, jnp.where(forced, FORCE_SCORE, imp), NEG_INF))
    sidb = jnp.broadcast_to(sid, (RS, nsp))
    validb = [jnp.logical_and(sid * L_SLC <= qpos[r0:r0 + RS], sid < NS) for _ in scores for r0 in range(0, TQ, RS)]
    chains = [sc[r0:r0 + RS] for sc in scores for r0 in range(0, TQ, RS)]
    sels = [jnp.zeros((RS, nsp), F32) for _ in chains]
    for _ in range(n_pick):
        firsts = [jnp.argmax(sc, axis=-1, keepdims=True).astype(jnp.int32) for sc in chains]
        picks = [sidb == f for f in firsts]
        sels = [jnp.where(jnp.logical_and(pk, vb), 1.0, sl) for pk, vb, sl in zip(picks, validb, sels)]
        chains = [jnp.where(pk, -jnp.inf, sc) for pk, sc in zip(picks, chains)]
    nsub = TQ // RS
    for bi in range(bb):
        for g in range(NSA_KV):
            for r in range(nsub):
                sel_ref[bi, g, r * RS:(r + 1) * RS, :] = sels[(bi * NSA_KV + g) * nsub + r]


def _nsa_select(q, cmp, overlap, slopes, *, TQ, NC, NS, q_off, BB):
    b, t, _ = q.shape
    nch = cmp.shape[1]
    nsp = overlap.shape[1]
    kern = functools.partial(_nsa_sel_kernel, TQ=TQ, NC=NC, NS=NS, q_off=q_off, n_pick=min(N_SEL, NS),
                             RS=_tile(TQ, (32, 16, 8)))
    return pl.pallas_call(
        kern,
        grid=(b // BB, t // TQ),
        in_specs=[
            pl.BlockSpec(memory_space=pltpu.SMEM),
            pl.BlockSpec((BB, TQ, A_Q), lambda bi, i: (bi, i, 0)),
            pl.BlockSpec((BB, nch, A_KVW), lambda bi, i: (bi, 0, 0)),
            pl.BlockSpec((nch, nsp), lambda bi, i: (0, 0)),
        ],
        out_specs=[
            pl.BlockSpec((BB, TQ, A_Q), lambda bi, i: (bi, i, 0)),
            pl.BlockSpec((BB, NSA_KV, TQ, nsp), lambda bi, i: (bi, 0, i, 0)),
        ],
        out_shape=[jax.ShapeDtypeStruct((b, t, A_Q), F32), jax.ShapeDtypeStruct((b, NSA_KV, t, nsp), F32)],
        compiler_params=_cp(("parallel", "parallel")),
        name="nsa_cmp_select",
    )(slopes, q, cmp, overlap)


def _stack_heads(q, tq):
    del tq
    return jnp.concatenate([q[:, j * HD:(j + 1) * HD] for j in range(NSA_GROUP)], axis=0)


def _online_step(carry, s, dist, mask_add, slopes, v):
    m, l, acc = carry
    lg = s + jnp.concatenate([mask_add - sl * dist for sl in slopes], axis=0)
    m_new = jnp.maximum(m, jnp.max(lg, axis=-1, keepdims=True))
    p = jnp.exp(lg - m_new)
    a = jnp.exp(m - m_new)
    l = a * l + jnp.sum(p, axis=-1, keepdims=True)
    acc = a * acc + _dot(p.astype(BF16), v)
    return m_new, l, acc


def _nsa_slc_kernel(slopes_ref, q_ref, sel_ref, k_ref, v_ref, o_ref, *, TQ, TK):
    g = pl.program_id(0)
    i = pl.program_id(1)
    nsp = sel_ref.shape[-1]
    rows = NSA_GROUP * TQ
    q4 = (_stack_heads(q_ref[...], TQ) * HD ** -0.5).astype(BF16)
    qposf = (i * TQ + lax.broadcasted_iota(jnp.int32, (TQ, 1), 0)).astype(F32)
    slopes = [slopes_ref[NSA_GROUP * g + j] for j in range(NSA_GROUP)]
    selb = sel_ref[0].astype(BF16)
    blk_per_tile = TK // L_SLC
    delta = (lax.broadcasted_iota(jnp.int32, (nsp, TK), 0)
             - lax.shift_right_arithmetic(lax.broadcasted_iota(jnp.int32, (nsp, TK), 1), int(np.log2(L_SLC))))
    kcol = lax.broadcasted_iota(jnp.int32, (1, TK), 1)

    def body(kt, carry):
        k0 = pl.multiple_of(kt * TK, TK)
        k = k_ref[pl.ds(k0, TK), :]
        v = v_ref[pl.ds(k0, TK), :]
        s = _dot_nt(q4, k)
        dist = qposf - (k0 + kcol).astype(F32)
        expand = jnp.where(delta == kt * blk_per_tile, 1.0, 0.0).astype(BF16)
        se = _dot(selb, expand)
        mask_add = jnp.where(jnp.logical_and(se > 0.5, dist >= 0.0), 0.0, NEG_INF)
        return _online_step(carry, s, dist, mask_add, slopes, v)

    ntile = (i * TQ + TQ + TK - 1) // TK
    init = (jnp.full((rows, 1), NEG_INF, F32), jnp.zeros((rows, 1), F32), jnp.zeros((rows, HD), F32))
    _, l, acc = lax.fori_loop(0, ntile, body, init)
    o = acc / l
    for j in range(NSA_GROUP):
        o_ref[:, j * HD:(j + 1) * HD] = o[j * TQ:(j + 1) * TQ]


def _nsa_slc_prompt(q, sel, kvb, slopes, *, TQ, TK):
    t = q.shape[0]
    nsp = sel.shape[-1]
    kb0 = A_KVW // HD
    return pl.pallas_call(
        functools.partial(_nsa_slc_kernel, TQ=TQ, TK=TK),
        grid=(NSA_KV, t // TQ),
        in_specs=[
            pl.BlockSpec(memory_space=pltpu.SMEM),
            pl.BlockSpec((TQ, NSA_GROUP * HD), lambda g, i: (i, g)),
            pl.BlockSpec((1, TQ, nsp), lambda g, i: (g, i, 0)),
            pl.BlockSpec((t, HD), lambda g, i: (0, kb0 + g)),
            pl.BlockSpec((t, HD), lambda g, i: (0, kb0 + NSA_KV + g)),
        ],
        out_specs=pl.BlockSpec((TQ, NSA_GROUP * HD), lambda g, i: (i, g)),
        out_shape=jax.ShapeDtypeStruct((t, A_Q), F32),
        compiler_params=_cp(("parallel", "parallel")),
        name="nsa_slc_prompt",
    )(slopes, q, sel, kvb, kvb)


def _nsa_win_kernel(slopes_ref, q_ref, k_ref, v_ref, ga_ref, ocmp_ref, oslc_ref, o_ref, *, TQ):
    g = pl.program_id(0)
    i = pl.program_id(1)
    rows = NSA_GROUP * TQ
    q4 = (_stack_heads(q_ref[...], TQ) * HD ** -0.5).astype(BF16)
    qposf = (i * TQ + lax.broadcasted_iota(jnp.int32, (TQ, 1), 0)).astype(F32)
    slopes = [slopes_ref[NSA_GROUP * g + j] for j in range(NSA_GROUP)]
    span = min(WINDOW + TQ, k_ref.shape[0])
    k0 = pl.multiple_of(jnp.maximum((i + 1) * TQ - span, 0), TQ)
    kcol = lax.broadcasted_iota(jnp.int32, (1, span), 1)
    k = k_ref[pl.ds(k0, span), :]
    v = v_ref[pl.ds(k0, span), :]
    dist = qposf - (k0 + kcol).astype(F32)
    mask_add = jnp.where(jnp.logical_and(dist >= 0.0, dist <= float(WINDOW)), 0.0, NEG_INF)
    init = (jnp.full((rows, 1), NEG_INF, F32), jnp.zeros((rows, 1), F32), jnp.zeros((rows, HD), F32))
    _, l, acc = _online_step(init, _dot_nt(q4, k), dist, mask_add, slopes, v)
    o_win = acc / l
    gates = jax.nn.sigmoid(ga_ref[...])
    for j in range(NSA_GROUP):
        ca = 3 * j
        cb = 3 * (NSA_GROUP + j)

        def gate(c):
            return jnp.where(g == 0, gates[:, ca + c:ca + c + 1], gates[:, cb + c:cb + c + 1])

        sl = slice(j * HD, (j + 1) * HD)
        o_ref[:, sl] = (gate(0) * ocmp_ref[:, sl] + gate(1) * oslc_ref[:, sl]
                        + gate(2) * o_win[j * TQ:(j + 1) * TQ])


def _nsa_win_prompt(q, kvb, h_in, o_cmp, o_slc, slopes, *, TQ):
    t = q.shape[0]
    kb0 = 2 * A_KVW // HD
    hspec = pl.BlockSpec((TQ, NSA_GROUP * HD), lambda g, i: (i, g))
    return pl.pallas_call(
        functools.partial(_nsa_win_kernel, TQ=TQ),
        grid=(NSA_KV, t // TQ),
        in_specs=[
            pl.BlockSpec(memory_space=pltpu.SMEM),
            hspec,
            pl.BlockSpec((t, HD), lambda g, i: (0, kb0 + g)),
            pl.BlockSpec((t, HD), lambda g, i: (0, kb0 + NSA_KV + g)),
            pl.BlockSpec((TQ, LANE), lambda g, i: (i, COL_SM // LANE)),
            hspec,
            hspec,
        ],
        out_specs=hspec,
        out_shape=jax.ShapeDtypeStruct((t, A_Q), F32),
        compiler_params=_cp(("parallel", "parallel")),
        name="nsa_win_prompt",
    )(slopes, q, kvb, kvb, h_in, o_cmp, o_slc)


def _rows8(row, width):
    return jnp.concatenate([row[:, h * width:(h + 1) * width] for h in range(NSA_HEADS)], axis=0)


def _kv_rows8(kn, off):
    return jnp.concatenate(
        [kn[:, off + (h // NSA_GROUP) * HD: off + (h // NSA_GROUP + 1) * HD] for h in range(NSA_HEADS)], axis=0)


def _slope8(slopes_ref):
    hrow = lax.broadcasted_iota(jnp.int32, (NSA_HEADS, 1), 0)
    out = jnp.zeros((NSA_HEADS, 1), F32)
    for h in range(NSA_HEADS):
        out = jnp.where(hrow == h, slopes_ref[h], out)
    return out


def _bf(x):
    return x.astype(BF16).astype(F32)


def _nsa_slc_dec_kernel(pt_ref, slopes_ref, *refs, G, past):
    del pt_ref
    pages = refs[:G]
    q_ref, sel_ref, knew_ref, o_ref, kv_s, m_s, l_s, acc_s = refs[G:]
    pg = pl.program_id(1)
    nsp = sel_ref.shape[-1]
    tk = G * PAGE_ROWS
    q8 = _rows8(q_ref[0], HD) * HD ** -0.5
    q8b = q8.astype(BF16)
    hrow = lax.broadcasted_iota(jnp.int32, (NSA_HEADS, 1), 0)
    first_group = hrow < NSA_GROUP

    @pl.when(pg == 0)
    def _():
        kn = knew_ref[0]
        m_s[...] = jnp.sum(_bf(q8) * _bf(_kv_rows8(kn, 0)), axis=-1, keepdims=True)
        l_s[...] = jnp.ones_like(l_s)
        acc_s[...] = _bf(_kv_rows8(kn, NSA_KV * HD))

    for j in range(G):
        for sg in range(KV_PARTS):
            kv_s[sg, j * PAGE_ROWS:(j + 1) * PAGE_ROWS, :] = pages[j][0, pl.ds(sg, PAGE_ROWS, stride=KV_PARTS), :]
    kpos = pg * tk + lax.broadcasted_iota(jnp.int32, (1, tk), 1)
    dist = (past - kpos).astype(F32)
    srow = lax.broadcasted_iota(jnp.int32, (nsp, tk), 0)
    scol = (pg * tk + lax.broadcasted_iota(jnp.int32, (nsp, tk), 1)) // L_SLC
    expand = jnp.where(srow == scol, 1.0, 0.0).astype(BF16)
    sel2 = sel_ref[0]
    sel8 = jnp.where(first_group, sel2[0:1, :], sel2[1:2, :]).astype(BF16)
    mask = _dot(sel8, expand) > 0.5
    s8 = jnp.where(first_group, _dot_nt(q8b, kv_s[0].astype(BF16)), _dot_nt(q8b, kv_s[1].astype(BF16)))
    lg = jnp.where(mask, s8 - _slope8(slopes_ref) * dist, NEG_INF)
    m = m_s[...]
    m_new = jnp.maximum(m, jnp.max(lg, axis=-1, keepdims=True))
    p = jnp.where(mask, jnp.exp(lg - m_new), 0.0)
    a = jnp.exp(m - m_new)
    pb = p.astype(BF16)
    pv = jnp.where(first_group, _dot(pb, kv_s[2].astype(BF16)), _dot(pb, kv_s[3].astype(BF16)))
    m_s[...] = m_new
    l_s[...] = a * l_s[...] + jnp.sum(p, axis=-1, keepdims=True)
    acc_s[...] = a * acc_s[...] + pv

    @pl.when(pg == pl.num_programs(1) - 1)
    def _():
        o = acc_s[...] / l_s[...]
        for h in range(NSA_HEADS):
            o_ref[0, :, h * HD:(h + 1) * HD] = o[h:h + 1, :]


def _nsa_slc_sample(q, sel, knew, pool, page_table, slopes):
    b, n_pages = page_table.shape
    G = _tile(n_pages, (32, 16, 8, 4, 2, 1))
    nsp = sel.shape[-1]
    past = n_pages * PAGE_ROWS

    def page_map(j):
        return lambda bi, p, pt: (pt[bi, p * G + j], 0, 0)

    in_specs = [pl.BlockSpec(memory_space=pltpu.SMEM)]
    in_specs += [pl.BlockSpec((1, PAGE_SUBROWS, HD), page_map(j)) for j in range(G)]
    in_specs += [
        pl.BlockSpec((1, 1, A_Q), lambda bi, p, pt: (bi, 0, 0)),
        pl.BlockSpec((1, NSA_KV, nsp), lambda bi, p, pt: (bi, 0, 0)),
        pl.BlockSpec((1, 1, A_KVW), lambda bi, p, pt: (bi, 0, 0)),
    ]
    grid_spec = pltpu.PrefetchScalarGridSpec(
        num_scalar_prefetch=1,
        grid=(b, n_pages // G),
        in_specs=in_specs,
        out_specs=pl.BlockSpec((1, 1, A_Q), lambda bi, p, pt: (bi, 0, 0)),
        scratch_shapes=[pltpu.VMEM((KV_PARTS, G * PAGE_ROWS, HD), F32), pltpu.VMEM((NSA_HEADS, 1), F32),
                        pltpu.VMEM((NSA_HEADS, 1), F32), pltpu.VMEM((NSA_HEADS, HD), F32)],
    )
    return pl.pallas_call(
        functools.partial(_nsa_slc_dec_kernel, G=G, past=past),
        grid_spec=grid_spec,
        out_shape=jax.ShapeDtypeStruct((b, 1, A_Q), F32),
        compiler_params=_cp(("parallel", "arbitrary")),
        name="nsa_slc_sample",
    )(page_table, slopes, *([pool] * G), q, sel, knew)


def _nsa_win_dec_kernel(slopes_ref, q_ref, wb_ref, knew_ref, ga_ref, ocmp_ref, oslc_ref, o_ref, *, past):
    nb = wb_ref.shape[1] // KV_PARTS
    q8 = _rows8(q_ref[0], HD) * HD ** -0.5
    q8b = q8.astype(BF16)
    hrow = lax.broadcasted_iota(jnp.int32, (NSA_HEADS, 1), 0)
    first_group = hrow < NSA_GROUP
    kn = knew_ref[0]
    wb = [wb_ref[0, pl.ds(sg, nb, stride=KV_PARTS), :].astype(BF16) for sg in range(KV_PARTS)]
    kwpos = past - nb + lax.broadcasted_iota(jnp.int32, (1, nb), 1)
    wd = past - kwpos
    mask = jnp.logical_and(jnp.logical_and(wd >= 0, wd <= WINDOW), kwpos >= 0)
    s8 = jnp.where(first_group, _dot_nt(q8b, wb[0]), _dot_nt(q8b, wb[1]))
    lg = jnp.where(mask, s8 - _slope8(slopes_ref) * wd.astype(F32), NEG_INF)
    s_self = jnp.sum(_bf(q8) * _bf(_kv_rows8(kn, 0)), axis=-1, keepdims=True)
    m = jnp.maximum(jnp.max(lg, axis=-1, keepdims=True), s_self)
    p = jnp.where(mask, jnp.exp(lg - m), 0.0)
    p_self = jnp.exp(s_self - m)
    pb = p.astype(BF16)
    pv = jnp.where(first_group, _dot(pb, wb[2]), _dot(pb, wb[3]))
    pv = pv + _bf(p_self) * _bf(_kv_rows8(kn, NSA_KV * HD))
    o_win = pv / (jnp.sum(p, axis=-1, keepdims=True) + p_self)
    gates = jax.nn.sigmoid(ga_ref[0])
    ocmp = ocmp_ref[0]
    oslc = oslc_ref[0]
    for h in range(NSA_HEADS):
        sl = slice(h * HD, (h + 1) * HD)
        o_ref[0, :, sl] = (gates[:, 3 * h:3 * h + 1] * ocmp[:, sl] + gates[:, 3 * h + 1:3 * h + 2] * oslc[:, sl]
                           + gates[:, 3 * h + 2:3 * h + 3] * o_win[h:h + 1, :])


def _nsa_win_sample(q, win_buf, knew, ga, o_cmp, o_slc, slopes, *, past):
    b = q.shape[0]
    nsub = win_buf.shape[1]
    row = lambda w: pl.BlockSpec((1, 1, w), lambda bi: (bi, 0, 0))
    return pl.pallas_call(
        functools.partial(_nsa_win_dec_kernel, past=past),
        grid=(b,),
        in_specs=[pl.BlockSpec(memory_space=pltpu.SMEM), row(A_Q),
                  pl.BlockSpec((1, nsub, HD), lambda bi: (bi, 0, 0)), row(A_KVW), row(LANE), row(A_Q), row(A_Q)],
        out_specs=row(A_Q),
        out_shape=jax.ShapeDtypeStruct((b, 1, A_Q), F32),
        compiler_params=_cp(("parallel",)),
        name="nsa_win_sample",
    )(slopes, q, win_buf, knew, ga, o_cmp, o_slc)


def _dn_conv_kernel(x_ref, prev_ref, buf_ref, w_ref, o_ref, hist_ref):
    c = pl.program_id(0)
    t = pl.program_id(1)
    tt = x_ref.shape[0]
    hist_ref[0:8, :] = jnp.where(t == 0, buf_ref[...], prev_ref[...])
    hist_ref[8:8 + tt, :] = x_ref[...]
    w = w_ref[...]
    y = jnp.zeros((tt, LANE), F32)
    for i in range(CONV_W):
        y = y + w[i:i + 1, :] * hist_ref[8 - (CONV_W - 1) + i: 8 - (CONV_W - 1) + i + tt, :]
    y = _silu(y)
    nrm = y * lax.rsqrt(jnp.sum(y * y, axis=-1, keepdims=True) + 1e-6)
    o_ref[...] = jnp.where(c < DN_HEADS, nrm * DK ** -0.5, jnp.where(c < 2 * DN_HEADS, nrm, y))


def _dn_conv_prompt(h_in, buf8, conv_wt):
    t = h_in.shape[0]
    tt = _tile(t, (1024, 512, 256, 128, 64, 32, 16, 8))
    c0 = COL_DQKV // LANE
    return pl.pallas_call(
        _dn_conv_kernel,
        grid=(DN_QKV // LANE, t // tt),
        in_specs=[
            pl.BlockSpec((tt, LANE), lambda c, i: (i, c0 + c)),
            pl.BlockSpec((8, LANE), lambda c, i: (jnp.maximum(i * (tt // 8) - 1, 0), c0 + c)),
            pl.BlockSpec((8, LANE), lambda c, i: (0, c)),
            pl.BlockSpec((CONV_W, LANE), lambda c, i: (0, c)),
        ],
        out_specs=pl.BlockSpec((tt, LANE), lambda c, i: (i, c)),
        out_shape=jax.ShapeDtypeStruct((t, DN_QKV), F32),
        scratch_shapes=[pltpu.VMEM((tt + 8, LANE), F32)],
        compiler_params=_cp(("parallel", "parallel")),
        name="dn_conv",
    )(h_in, h_in, buf8, conv_wt)


def _dot1(x, y):
    return _dot(x.astype(BF16), y.astype(BF16))


def _unit_lower_inverse_many(mats, n, mm):
    ri = lax.broadcasted_iota(jnp.int32, (n, n), 0)
    ci = lax.broadcasted_iota(jnp.int32, (n, n), 1)
    eye = jnp.where(ri == ci, 1.0, 0.0).astype(F32)
    base = min(16, n)

    def same_block(b):
        return (ri // b) == (ci // b)

    ps = [jnp.where(same_block(base), -a, 0.0) for a in mats]
    rs = [eye + p for p in ps]
    for _ in range(int(np.log2(base)) - 1):
        ps = [mm(p, p) for p in ps]
        rs = [r + mm(r, p) for r, p in zip(rs, ps)]
    b = base
    while b < n:
        offm = jnp.logical_and(same_block(2 * b), jnp.logical_not(same_block(b)))
        ts = [mm(jnp.where(offm, a, 0.0), r) for a, r in zip(mats, rs)]
        rs = [r - mm(r, t) for r, t in zip(rs, ts)]
        b *= 2
    return rs


def _dn_chunk_kernel(q_ref, k_ref, v_ref, sm_ref, smt_ref, z_ref, prow_ref, pcol_ref, nw_ref,
                     o_ref, s_out_ref, s_ref):
    c = pl.program_id(0)
    C = q_ref.shape[0]

    @pl.when(c == 0)
    def _():
        s_ref[...] = jnp.zeros_like(s_ref)

    ri = lax.broadcasted_iota(jnp.int32, (C, C), 0)
    ci = lax.broadcasted_iota(jnp.int32, (C, C), 1)
    lower = ri >= ci
    tril = jnp.where(lower, 1.0, 0.0).astype(F32)
    triu = jnp.where(ri <= ci, 1.0, 0.0).astype(F32)
    ri2 = lax.broadcasted_iota(jnp.int32, (2 * C, C), 0)
    ci2 = lax.broadcasted_iota(jnp.int32, (2 * C, C), 1)
    mask2 = jnp.where(ri2 < C, ri2, ri2 - (C - 1)) > ci2
    sm = sm_ref[...]
    smt = smt_ref[...]
    g_cols = -jnp.exp(prow_ref[0:1, :]) * _softplus(sm + prow_ref[1:2, :])
    g_rows = -jnp.exp(pcol_ref[:, 0:1]) * _softplus(smt + pcol_ref[:, 1:2])
    gcum_cols = _dot(tril, g_cols, HI)
    gcum_rows = _dot(g_rows, triu, HI)
    beta_cols = jax.nn.sigmoid(sm)
    nw = nw_ref[...]
    heads = []
    for h in range(DN_HEADS):
        sl = slice(h * DK, (h + 1) * DK)
        q = q_ref[:, sl]
        k = k_ref[:, sl]
        gc = gcum_cols[:, SM_A + h:SM_A + h + 1]
        gr = gcum_rows[SM_A + h:SM_A + h + 1, :]
        beta = beta_cols[:, SM_BETA + h:SM_BETA + h + 1]
        decay = jnp.where(lower, jnp.exp(jnp.where(lower, gc - gr, 0.0)), 0.0)
        kb = k * beta
        egc = jnp.exp(gc)
        gl = gc[C - 1:C, :]
        heads.append(dict(
            h=h, sl=sl, kb16=k.astype(BF16), decay2=jnp.concatenate([decay, decay], axis=0),
            kbq=jnp.concatenate([kb, q], axis=0).astype(BF16),
            rhs=jnp.concatenate([v_ref[:, sl] * beta, kb * egc], axis=1).astype(BF16),
            qg=q * egc, kdec=(k * jnp.exp(gl - gc)).astype(BF16), egl=jnp.exp(gl)))
    for d in heads:
        d["aa"] = jnp.where(mask2, _dot_nt(d["kbq"], d["kb16"]) * d["decay2"], 0.0)
    tinv = _unit_lower_inverse_many([d["aa"][:C] for d in heads], C, _dot1)
    for d, ti in zip(heads, tinv):
        d["sol"] = _dot(ti.astype(BF16), d["rhs"])
    for d in heads:
        d["s"] = s_ref[d["h"]]
        d["ks2"] = _dot1(jnp.concatenate([d["sol"][:, DV:], d["qg"]], axis=0), d["s"])
    for d in heads:
        d["v_new"] = (d["sol"][:, :DV] - d["ks2"][:C]).astype(BF16)
    for d in heads:
        s_ref[d["h"]] = d["s"] * d["egl"] + _dot_tn(d["kdec"], d["v_new"])
    for d in heads:
        o = d["ks2"][C:] + _dot(d["aa"][C:].astype(BF16), d["v_new"])
        o = o * lax.rsqrt(jnp.mean(o * o, axis=-1, keepdims=True) + RMS_EPS) * nw
        o_ref[:, d["sl"]] = o * _silu(z_ref[:, d["sl"]])

    @pl.when(c == pl.num_programs(0) - 1)
    def _():
        s_out_ref[...] = s_ref[...]


def _dn_chunk_prompt(qkvn, h_in, smt, prow, pcol, norm_w, *, C):
    t = qkvn.shape[0]
    hw = DN_HEADS * DK
    return pl.pallas_call(
        _dn_chunk_kernel,
        grid=(t // C,),
        in_specs=[
            pl.BlockSpec((C, hw), lambda c: (c, 0)),
            pl.BlockSpec((C, hw), lambda c: (c, 1)),
            pl.BlockSpec((C, hw), lambda c: (c, 2)),
            pl.BlockSpec((C, LANE), lambda c: (c, COL_SM // LANE)),
            pl.BlockSpec((LANE, C), lambda c: (0, c)),
            pl.BlockSpec((C, hw), lambda c: (c, COL_Z // hw)),
            pl.BlockSpec((2, LANE), lambda c: (0, 0)),
            pl.BlockSpec((LANE, 2), lambda c: (0, 0)),
            pl.BlockSpec((1, DV), lambda c: (0, 0)),
        ],
        out_specs=[pl.BlockSpec((C, hw), lambda c: (c, 0)),
                   pl.BlockSpec((DN_HEADS, DK, DV), lambda c: (0, 0, 0))],
        out_shape=[jax.ShapeDtypeStruct((t, hw), F32), jax.ShapeDtypeStruct((DN_HEADS, DK, DV), F32)],
        scratch_shapes=[pltpu.VMEM((DN_HEADS, DK, DV), F32)],
        compiler_params=_cp(("arbitrary",)),
        name="dn_chunk",
    )(qkvn, qkvn, qkvn, h_in, smt, h_in, prow, pcol, norm_w)


def _row_to_col(row, n):
    ri = lax.broadcasted_iota(jnp.int32, (n, n), 0)
    ci = lax.broadcasted_iota(jnp.int32, (n, n), 1)
    return jnp.sum(jnp.where(ri == ci, jnp.broadcast_to(row, (n, n)), 0.0), axis=1, keepdims=True)


def _col_to_row(col, n):
    ri = lax.broadcasted_iota(jnp.int32, (n, n), 0)
    ci = lax.broadcasted_iota(jnp.int32, (n, n), 1)
    return jnp.sum(jnp.where(ri == ci, jnp.broadcast_to(col, (n, n)), 0.0), axis=0, keepdims=True)


def _dn_dec_kernel(buf_ref, xq_ref, xk_ref, xv_ref, w_ref, sm_ref, z_ref, prow_ref, nw_ref, s0_ref,
                   o_ref, s_out_ref):
    hw = DN_HEADS * DK
    buf = buf_ref[0]
    w = w_ref[...]
    sm = sm_ref[0]
    g_row = -jnp.exp(prow_ref[0:1, :]) * _softplus(sm + prow_ref[1:2, :])
    beta_row = jax.nn.sigmoid(sm)
    nw = nw_ref[...]
    z = z_ref[0]
    parts = []
    for part, x_ref in enumerate((xq_ref, xk_ref, xv_ref)):
        sl = slice(part * hw, (part + 1) * hw)
        y = w[CONV_W - 1:CONV_W, sl] * x_ref[0]
        for i in range(CONV_W - 1):
            y = y + w[i:i + 1, sl] * buf[i:i + 1, sl]
        parts.append(_silu(y))
    for h in range(DN_HEADS):
        sl = slice(h * DK, (h + 1) * DK)
        q = parts[0][:, sl]
        k = parts[1][:, sl]
        v = parts[2][:, sl]
        q = q * lax.rsqrt(jnp.sum(q * q, axis=-1, keepdims=True) + 1e-6) * DK ** -0.5
        k = k * lax.rsqrt(jnp.sum(k * k, axis=-1, keepdims=True) + 1e-6)
        a = jnp.exp(g_row[:, SM_A + h:SM_A + h + 1])
        beta = beta_row[:, SM_BETA + h:SM_BETA + h + 1]
        k_col = _row_to_col(k, DK)
        q_col = _row_to_col(q, DK)
        s0 = s0_ref[0, h]
        u = beta * (v - a * jnp.sum(s0 * k_col, axis=0, keepdims=True))
        s_new = a * s0 + k_col * u
        s_out_ref[0, h] = s_new
        o = jnp.sum(s_new * q_col, axis=0, keepdims=True)
        o = o * lax.rsqrt(jnp.mean(o * o, axis=-1, keepdims=True) + RMS_EPS) * nw
        o_ref[0, :, sl] = o * _silu(z[:, sl])


def _dn_sample(conv_buf, h3, conv_wt, prow, norm_w, s0):
    b = h3.shape[0]
    hw = DN_HEADS * DK
    c0 = COL_DQKV // hw
    row = lambda w, j: pl.BlockSpec((1, 1, w), lambda bi: (bi, 0, j))
    return pl.pallas_call(
        _dn_dec_kernel,
        grid=(b,),
        in_specs=[
            pl.BlockSpec((1, CONV_W - 1, DN_QKV), lambda bi: (bi, 0, 0)),
            row(hw, c0), row(hw, c0 + 1), row(hw, c0 + 2),
            pl.BlockSpec((CONV_W, DN_QKV), lambda bi: (0, 0)),
            row(LANE, COL_SM // LANE),
            row(hw, COL_Z // hw),
            pl.BlockSpec((2, LANE), lambda bi: (0, 0)),
            pl.BlockSpec((1, DV), lambda bi: (0, 0)),
            pl.BlockSpec((1, DN_HEADS, DK, DV), lambda bi: (bi, 0, 0, 0)),
        ],
        out_specs=[row(hw, 0), pl.BlockSpec((1, DN_HEADS, DK, DV), lambda bi: (bi, 0, 0, 0))],
        out_shape=[jax.ShapeDtypeStruct((b, 1, hw), F32), jax.ShapeDtypeStruct(s0.shape, F32)],
        compiler_params=_cp(("parallel",)),
        name="dn_sample",
    )(conv_buf, h3, h3, h3, conv_wt, h3, h3, prow, norm_w, s0)


def _rwkv_prep(r, k, wl, al, w0, a0, k_k, k_a):
    w_log = -_softplus(-(w0 + wl)) - 0.5
    log_decay = -jnp.exp(w_log)
    a = jax.nn.sigmoid(a0 + al)
    kk_raw = k * k_k
    k_h = k * (1.0 + (a - 1.0) * k_a)
    del r
    return log_decay, a, kk_raw, k_h


def _rwkv_chunk_kernel(r_ref, k_ref, v_ref, wl_ref, al_ref, gate_ref, prm_ref, o_ref, s_out_ref, s_ref):
    c = pl.program_id(1)
    PP = r_ref.shape[0]
    C = r_ref.shape[1]
    N = RWKV_HS
    hpp = LANE // N

    @pl.when(c == 0)
    def _():
        s_ref[...] = jnp.zeros_like(s_ref)

    ri = lax.broadcasted_iota(jnp.int32, (2 * C, C), 0)
    ci = lax.broadcasted_iota(jnp.int32, (2 * C, C), 1)
    mask2 = jnp.where(ri < C, ri, ri - (C - 1)) > ci
    trilb = jnp.where(lax.broadcasted_iota(jnp.int32, (C, C), 0) >= lax.broadcasted_iota(jnp.int32, (C, C), 1),
                      1.0, 0.0).astype(BF16)
    heads = []
    for pp in range(PP):
        prm = prm_ref[pp]
        r2 = r_ref[pp]
        v2 = v_ref[pp]
        log_decay2, a2, kk_raw2, kh2 = _rwkv_prep(r2, k_ref[pp], wl_ref[pp], al_ref[pp],
                                                  prm[0:1], prm[1:2], prm[2:3], prm[3:4])
        ld_h = log_decay2.astype(BF16)
        rem = log_decay2 - ld_h.astype(F32)
        ld_m = rem.astype(BF16)
        ld_l = (rem - ld_m.astype(F32)).astype(BF16)
        gcum2 = _dot(trilb, ld_h) + (_dot(trilb, ld_m) + _dot(trilb, ld_l))
        for hh in range(hpp):
            sl = slice(hh * N, (hh + 1) * N)
            r = r2[:, sl]
            v = v2[:, sl]
            k_h = kh2[:, sl]
            kk = kk_raw2[:, sl]
            kk = kk / jnp.maximum(jnp.sqrt(jnp.sum(kk * kk, axis=-1, keepdims=True)), 1e-12)
            gc = gcum2[:, sl]
            p_incl = jnp.exp(gc)
            p_inv = jnp.exp(-gc)
            at = -kk * jnp.exp(gc - log_decay2[:, sl])
            bt = kk * a2[:, sl] * p_inv
            kt = k_h * p_inv
            rt = r * p_incl
            heads.append(dict(
                idx=pp * hpp + hh, v=v, p_last=p_incl[C - 1:C, :],
                lhs=jnp.concatenate([at, rt], axis=0).astype(BF16),
                bk=jnp.concatenate([bt, kt], axis=0).astype(BF16),
                bonus=jnp.sum(r * k_h * prm[4:5, sl], axis=-1, keepdims=True) * v,
                ln_w=prm[5:6, sl], ln_b=prm[6:7, sl]))
    for h in heads:
        h["g_b"] = jnp.where(mask2, _dot_nt(h["lhs"], h["bk"][:C]), 0.0)
    tinv = _unit_lower_inverse_many([-h["g_b"][:C] for h in heads], C, _dot1)
    for h in heads:
        h["g_k"] = jnp.where(mask2, _dot_nt(h["lhs"], h["bk"][C:]), 0.0)
    for h in heads:
        h["w_kv"] = _dot1(h["g_k"], h["v"])
    for h in heads:
        h["s0"] = s_ref[h["idx"]]
        h["g_s"] = _dot_nt(h["lhs"], h["s0"].astype(BF16))
    for h, ti in zip(heads, tinv):
        h["u"] = _dot1(ti, h["g_s"][:C] + h["w_kv"][:C])
    for h in heads:
        uv = jnp.concatenate([h["u"], h["v"]], axis=0).astype(BF16)
        s_ref[h["idx"]] = (h["s0"] + _dot_tn(uv, h["bk"])) * h["p_last"]
    outs = []
    for h in heads:
        out = h["g_s"][C:] + h["w_kv"][C:] + _dot1(h["g_b"][C:], h["u"])
        mu = jnp.mean(out, axis=-1, keepdims=True)
        d = out - mu
        var = jnp.mean(d * d, axis=-1, keepdims=True)
        outs.append(d * lax.rsqrt(var + GN_EPS) * h["ln_w"] + h["ln_b"] + h["bonus"])
    for pp in range(PP):
        o_ref[pp] = jnp.concatenate(outs[pp * hpp:(pp + 1) * hpp], axis=1) * gate_ref[pp]

    @pl.when(c == pl.num_programs(1) - 1)
    def _():
        s_out_ref[...] = s_ref[...]


def _rwkv_chunk_prompt(r, k, v, wl, al, gate, prm, *, C, PP):
    npair, t, _ = r.shape
    hpp = LANE // RWKV_HS
    seq = pl.BlockSpec((PP, C, LANE), lambda p, c: (p, c, 0))
    return pl.pallas_call(
        _rwkv_chunk_kernel,
        grid=(npair // PP, t // C),
        in_specs=[seq] * 6 + [pl.BlockSpec((PP, 8, LANE), lambda p, c: (p, 0, 0))],
        out_specs=[seq, pl.BlockSpec((PP * hpp, RWKV_HS, RWKV_HS), lambda p, c: (p, 0, 0))],
        out_shape=[jax.ShapeDtypeStruct((npair, t, LANE), F32),
                   jax.ShapeDtypeStruct((npair * hpp, RWKV_HS, RWKV_HS), F32)],
        scratch_shapes=[pltpu.VMEM((PP * hpp, RWKV_HS, RWKV_HS), F32)],
        compiler_params=_cp(("parallel", "arbitrary")),
        name="rwkv_chunk",
    )(r, k, v, wl, al, gate, prm)


def _rwkv_dec_kernel(r_ref, k_ref, v_ref, wl_ref, al_ref, gate_ref, prm_ref, s0_ref, o_ref, s_out_ref):
    N = RWKV_HS
    prm = prm_ref[...]
    r2 = r_ref[0]
    k2 = k_ref[0]
    v2 = v_ref[0]
    log_decay2, a2, kk_raw2, kh2 = _rwkv_prep(r2, k2, wl_ref[0], al_ref[0], prm[0:1], prm[1:2], prm[2:3], prm[3:4])
    w2 = jnp.exp(log_decay2)
    gate = gate_ref[0]
    nh = r2.shape[1] // N
    sls = [slice(h * N, (h + 1) * N) for h in range(nh)]
    lane_sum = lambda xs: [jnp.sum(x, axis=-1, keepdims=True) for x in xs]
    kks = [kk_raw2[:, sl] for sl in sls]
    kks = [kk / jnp.maximum(jnp.sqrt(n2), 1e-12) for kk, n2 in zip(kks, lane_sum([kk * kk for kk in kks]))]
    s0s = [s0_ref[0, h] for h in range(nh)]
    sas = lane_sum([s0 * (-kk) for s0, kk in zip(s0s, kks)])
    vcols = [_row_to_col(v2[:, sl], N) for sl in sls]
    s_news = [s0 * w2[:, sl] + sa * (kk * a2[:, sl]) + vc * kh2[:, sl]
              for s0, sl, sa, kk, vc in zip(s0s, sls, sas, kks, vcols)]
    for h in range(nh):
        s_out_ref[0, h] = s_news[h]
    outs = [_col_to_row(oc, N) for oc in lane_sum([sn * r2[:, sl] for sn, sl in zip(s_news, sls)])]
    mus = lane_sum(outs)
    ds = [o - mu * (1.0 / N) for o, mu in zip(outs, mus)]
    vars_ = lane_sum([d * d for d in ds])
    bon = lane_sum([r2[:, sl] * kh2[:, sl] * prm[4:5, sl] for sl in sls])
    for sl, d, var, bo in zip(sls, ds, vars_, bon):
        gn = d * lax.rsqrt(var * (1.0 / N) + GN_EPS) * prm[5:6, sl] + prm[6:7, sl]
        o_ref[0, :, sl] = (gn + bo * v2[:, sl]) * gate[:, sl]


def _rwkv_sample(r, k, v, wl, al, gate, prm, s0):
    b, _, d = r.shape
    row = pl.BlockSpec((1, 1, d), lambda bi: (bi, 0, 0))
    st = pl.BlockSpec((1,) + s0.shape[1:], lambda bi: (bi, 0, 0, 0))
    return pl.pallas_call(
        _rwkv_dec_kernel,
        grid=(b,),
        in_specs=[row] * 6 + [pl.BlockSpec((8, d), lambda bi: (0, 0)), st],
        out_specs=[row, st],
        out_shape=[jax.ShapeDtypeStruct((b, 1, d), F32), jax.ShapeDtypeStruct(s0.shape, F32)],
        compiler_params=_cp(("parallel",)),
        name="rwkv_sample",
    )(r, k, v, wl, al, gate, prm, s0)


def _alibi_slopes():
    return jnp.asarray(2.0 ** (-8.0 * np.arange(1, NSA_HEADS + 1) / NSA_HEADS), dtype=F32)


def _overlap_matrix(nch, nsp):
    cstart = np.arange(nch)[:, None] * D_CMP
    sstart = np.arange(nsp)[None, :] * L_SLC
    return jnp.asarray(((cstart < sstart + L_SLC) & (cstart + L_CMP > sstart)).astype(np.float32))


def _pack_w_in(w):
    offs = np.concatenate([[0], np.cumsum(EVEN_SPLIT)])
    qa, kvc, kvs, kvw, ga, qkv, z, b, a = [w[:, offs[i]:offs[i + 1]] for i in range(len(EVEN_SPLIT))]
    used = COL_SM + A_GATES + 2 * DN_HEADS
    pad = jnp.zeros((w.shape[0], E_IN_PAD - used), w.dtype)
    return jnp.concatenate([qa, qkv, z, kvc, kvs, kvw, ga, b, a, pad], axis=1).astype(BF16)


def _pack_cmp_w1(w1):
    hid = w1.shape[-1]
    w = w1.reshape(2, 2, D_CMP, HD, hid).transpose(0, 2, 3, 1, 4)
    return w.reshape(2, D_CMP, HD, 2 * hid).astype(BF16)


def _pack_cmp_pe(pe):
    p = pe.reshape(2, 2, D_CMP, HD).transpose(0, 2, 1, 3)
    return jnp.concatenate([p, jnp.zeros((2, D_CMP, 6, HD), pe.dtype)], axis=2)


def _dn_gate_params(a_log, dt_bias):
    row = jnp.zeros((2, LANE), F32).at[0, SM_A:SM_A + DN_HEADS].set(a_log).at[1, SM_A:SM_A + DN_HEADS].set(dt_bias)
    return row, row.T


def _even_layer(xp, xs, w_in, w_out, pe, w1, b1, w2, conv_w, a_log, dt_bias, norm_w,
                cache_cmp, cache_slc, win_buf, conv_buf, dn_s0, page_table, g, b):
    t = xp.shape[0]
    bs = xs.shape[0]
    n_pages = page_table.shape[1]
    past = n_pages * PAGE_ROWS
    slopes = _alibi_slopes()
    w_in_p = _pack_w_in(w_in)
    w4 = _pack_cmp_w1(w1)
    pe8 = _pack_cmp_pe(pe)
    w2b = w2.astype(BF16)
    conv_wt = conv_w.T
    prow, pcol = _dn_gate_params(a_log, dt_bias)
    nw = norm_w.reshape(1, DV)

    hp = _mm(xp, w_in_p)
    hs = _mm(xs, w_in_p)
    kvc_p = hp[:, COL_KVC:COL_KVC + A_KVW]
    kvs_p = hp[:, COL_KVS:COL_KVS + A_KVW]
    kvw_p = hp[:, COL_KVW:COL_KVW + A_KVW]
    kvc_s = hs[:, COL_KVC:COL_KVC + A_KVW]
    kvs_s = hs[:, COL_KVS:COL_KVS + A_KVW]
    kvw_s = hs[:, COL_KVW:COL_KVW + A_KVW]

    TQ = _tile(t, (128, 64, 32, 16, 8))
    nch = t // D_CMP
    ns = t // L_SLC
    nsp = -(-ns // LANE) * LANE
    arange_pt = jnp.arange(t // PAGE_ROWS, dtype=jnp.int32)[None]
    h1 = _cmp_stage1(kvc_p.reshape(t // PAGE_ROWS, PAGE_SUBROWS, HD), arange_pt, w4)
    cmp_p = _cmp_stage2(h1, pe8, w4, b1, w2b)
    q_p = hp
    o_cmp, sel = _nsa_select(q_p[None], cmp_p, _overlap_matrix(nch, nsp), slopes,
                             TQ=_tile(t, (512, 256, 128, 64, 32, 16, 8)), NC=(t - L_CMP) // D_CMP + 1, NS=ns, q_off=0,
                             BB=1)
    kvb = hp[:, COL_KVC:COL_KVC + 3 * A_KVW].astype(BF16)
    o_slc = _nsa_slc_prompt(q_p, sel[0], kvb, slopes, TQ=TQ, TK=_tile(t, (1024, 512, 256, 128, 64)))
    o_a_p = _nsa_win_prompt(q_p, kvb, hp, o_cmp[0], o_slc, slopes, TQ=TQ)

    nch_s = n_pages * CH_PER_PAGE
    nc_s = (past + 1 - L_CMP) // D_CMP + 1
    ns_s = -(-(past + 1) // L_SLC)
    nsp_s = -(-ns_s // LANE) * LANE
    h1s = _cmp_stage1(cache_cmp.reshape(cache_cmp.shape[0], PAGE_SUBROWS, HD), page_table, w4)
    cmp_s = _cmp_stage2(h1s, pe8, w4, b1, w2b)
    hs3 = hs[:, None, :]
    q_s = hs3[:, :, COL_QA:COL_QA + A_Q]
    o_cmp_s, sel_s = _nsa_select(q_s, cmp_s, _overlap_matrix(nch_s, nsp_s), slopes,
                                 TQ=1, NC=nc_s, NS=ns_s, q_off=past, BB=_tile(bs, (8, 4, 2, 1)))
    o_slc_s = _nsa_slc_sample(q_s, sel_s[:, :, 0, :], kvs_s[:, None, :],
                              cache_slc.reshape(cache_slc.shape[0], PAGE_SUBROWS, HD), page_table, slopes)
    wb = win_buf.reshape(bs, win_buf.shape[1] * KV_PARTS, HD)
    o_a_s = _nsa_win_sample(q_s, wb, kvw_s[:, None, :], hs3[:, :, COL_SM:COL_SM + LANE], o_cmp_s, o_slc_s,
                            slopes, past=past)

    qkvn = _dn_conv_prompt(hp, jnp.zeros((8, DN_QKV), F32), conv_wt)
    smt = hp[:, COL_SM:COL_SM + LANE].T
    o_b_p, dn_s_p = _dn_chunk_prompt(qkvn, hp, smt, prow, pcol, nw, C=_tile(t, (128,)))
    o_b_s, dn_s_s = _dn_sample(conv_buf, hs3, conv_wt, prow, nw, dn_s0)

    w_out_b = w_out.astype(BF16)
    yp = _mm_ln([o_a_p, o_b_p], w_out_b, xp, g, b)
    ys = _mm_ln([o_a_s[:, 0], o_b_s[:, 0]], w_out_b, xs, g, b)

    kv6 = lambda a: a.reshape(a.shape[:-1] + (2, NSA_KV, HD))
    raw_p = hp[:, COL_DQKV:COL_DQKV + DN_QKV]
    raw_s = hs[:, COL_DQKV:COL_DQKV + DN_QKV]
    wlen = min(WINDOW, t)
    outs = dict(
        cmp_p=kv6(kvc_p)[None], cmp_s=kv6(kvc_s)[:, None],
        slc_p=kv6(kvs_p)[None], slc_s=kv6(kvs_s)[:, None],
        win_p=kv6(kvw_p[t - wlen:])[None],
        win_s=jnp.concatenate([win_buf, kv6(kvw_s)[:, None]], axis=1)[:, 1:],
        conv_p=jnp.concatenate([jnp.zeros((CONV_W - 1, DN_QKV), F32), raw_p], axis=0)[t:][None],
        conv_s=jnp.concatenate([conv_buf, raw_s[:, None]], axis=1)[:, 1:],
        dns_p=dn_s_p[None], dns_s=dn_s_s,
    )
    return yp, ys, outs


def _odd_layer(xp, xs, shift_s, s0_s, mix, wr, wk, wv, wo, w0, w1, w2, a0, a1, a2, g1, g2, k_k, k_a, r_k,
               ln_w, ln_b, g, b):
    t, d = xp.shape
    bs = xs.shape[0]
    npair = d // LANE
    first_p = jnp.zeros((8, d), F32)

    def padk(wa, wb_):
        r = wa.shape[1]
        rp = -(-r // LANE) * LANE
        return (jnp.pad(wa, ((0, 0), (0, rp - r))).astype(BF16), jnp.pad(wb_, ((0, rp - r), (0, 0))).astype(BF16))

    wrb, wkb, wvb, wob = (w.astype(BF16) for w in (wr, wk, wv, wo))
    w1b, w2b = padk(w1, w2)
    a1b, a2b = padk(a1, a2)
    g1b, g2b = padk(g1, g2)
    prm = jnp.stack([w0, a0, k_k, k_a, r_k.reshape(d), ln_w, ln_b, jnp.zeros((d,), F32)])

    def proj(x, prev_kw, pair_out):
        mm = lambda w_, i, **kw: _mm(x, w_, mixrow=mix[i:i + 1], **prev_kw, **kw)
        r = mm(wrb, 0, pair_out=pair_out)
        wl = _mm(mm(w1b, 1, act="tanh"), w2b, pair_out=pair_out)
        k = mm(wkb, 2, pair_out=pair_out)
        v = mm(wvb, 3, pair_out=pair_out)
        al = _mm(mm(a1b, 4), a2b, pair_out=pair_out)
        gate = _mm(mm(g1b, 5, act="sigmoid"), g2b, pair_out=pair_out)
        return r, k, v, wl, al, gate

    pp = proj(xp, dict(first8=first_p), True)
    prm_pair = prm.reshape(8, npair, LANE).transpose(1, 0, 2)
    y_p, s_p = _rwkv_chunk_prompt(*pp, prm_pair, C=_tile(t, (64, 32, 16, 8)), PP=16)
    yp = _mm_ln([y_p], wob, xp, g, b, pair_in=True)

    ps = [a[:, None, :] for a in proj(xs, dict(xprev=shift_s), False)]
    y_s, s_s = _rwkv_sample(*ps, prm, s0_s)
    ys = _mm_ln([y_s[:, 0]], wob, xs, g, b)
    return yp, ys, dict(shift_p=xp[t - 1:t], shift_s=xs, rs_p=s_p[None], rs_s=s_s)


def kernel(x_prompt, x_sample, cache_nsa_cmp, cache_nsa_slc, cache_nsa_win, state_dn_conv, state_dn_S, state_rwkv_shift, state_rwkv_S, page_table, ln_g, ln_b, ffn_wi, ffn_wo, mix_w_in, mix_w_out, nsa_cmp_pe, nsa_cmp_w1, nsa_cmp_b1, nsa_cmp_w2, dn_conv_w, dn_a_log, dn_dt_bias, dn_norm_w, rwkv_mix, rwkv_wr, rwkv_wk, rwkv_wv, rwkv_wo, rwkv_w0, rwkv_w1, rwkv_w2, rwkv_a0, rwkv_a1, rwkv_a2, rwkv_g1, rwkv_g2, rwkv_k_k, rwkv_k_a, rwkv_r_k, rwkv_ln_w, rwkv_ln_b):
    bp, t, d = x_prompt.shape
    assert bp == 1 and x_sample.shape[1] == 1
    depth = ffn_wi.shape[0]
    xp = x_prompt[0]
    xs = x_sample[:, 0]
    even, odd = [], []
    wi, wo = ffn_wi, ffn_wo
    for l in range(depth):
        gl = lambda i: (ln_g[l, i][None], ln_b[l, i][None])
        xp = _ffn_ln(xp, wi, wo, l, 0, *gl(0))
        xs = _ffn_ln(xs, wi, wo, l, 0, *gl(0))
        if l % 2 == 0:
            e = l // 2
            xp, xs, o = _even_layer(
                xp, xs, mix_w_in[e], mix_w_out[e], nsa_cmp_pe[e], nsa_cmp_w1[e], nsa_cmp_b1[e], nsa_cmp_w2[e],
                dn_conv_w[e], dn_a_log[e], dn_dt_bias[e], dn_norm_w[e], cache_nsa_cmp[e], cache_nsa_slc[e],
                cache_nsa_win[e], state_dn_conv[e], state_dn_S[e], page_table, *gl(1))
            even.append(o)
        else:
            c = l // 2
            xp, xs, o = _odd_layer(
                xp, xs, state_rwkv_shift[c], state_rwkv_S[c], rwkv_mix[c], rwkv_wr[c], rwkv_wk[c], rwkv_wv[c],
                rwkv_wo[c], rwkv_w0[c], rwkv_w1[c], rwkv_w2[c], rwkv_a0[c], rwkv_a1[c], rwkv_a2[c], rwkv_g1[c],
                rwkv_g2[c], rwkv_k_k[c], rwkv_k_a[c], rwkv_r_k[c], rwkv_ln_w[c], rwkv_ln_b[c], *gl(1))
            odd.append(o)
        xp = _ffn_ln(xp, wi, wo, l, 1, *gl(2))
        xs = _ffn_ln(xs, wi, wo, l, 1, *gl(2))
    st = lambda lst, key: jnp.stack([o[key] for o in lst])
    return (xp[None], xs[:, None],
            st(even, "cmp_p"), st(even, "cmp_s"), st(even, "slc_p"), st(even, "slc_s"),
            st(even, "win_p"), st(even, "win_s"), st(even, "conv_p"), st(even, "conv_s"),
            st(even, "dns_p"), st(even, "dns_s"),
            st(odd, "shift_p"), st(odd, "shift_s"), st(odd, "rs_p"), st(odd, "rs_s"))
```

```python
import functools

import numpy as np
import jax
import jax.numpy as jnp
from jax import lax
from jax.experimental import pallas as pl
from jax.experimental.pallas import tpu as pltpu

F32 = jnp.float32
BF16 = jnp.bfloat16
HI = lax.Precision.HIGHEST

DEPTH = 2
ALPHA = (2 * DEPTH) ** 0.25
LN_EPS = 1e-5
RMS_EPS = 1e-6
NSA_HEADS = 8
NSA_KV = 2
NSA_GROUP = NSA_HEADS // NSA_KV
HD = 128
L_CMP = 32
D_CMP = 16
L_SLC = 64
N_SEL = 16
WINDOW = 512
NEG_INF = -1e30
FORCE_SCORE = 1e6
DN_HEADS = 8
DK = 128
DV = 128
CONV_W = 4
RWKV_HS = 64
GN_EPS = 64e-5

A_Q = NSA_HEADS * HD
A_KVW = 2 * NSA_KV * HD
A_GATES = 3 * NSA_HEADS
DN_QKV = DN_HEADS * (2 * DK + DV)
EVEN_SPLIT = (A_Q, A_KVW, A_KVW, A_KVW, A_GATES, DN_QKV, DN_HEADS * DV, DN_HEADS, DN_HEADS)
COL_QA = 0
COL_DQKV = 1024
COL_Z = 4096
COL_KVC = 5120
COL_KVS = 5632
COL_KVW = 6144
COL_SM = 6656
SM_BETA = A_GATES
SM_A = A_GATES + DN_HEADS
E_IN_PAD = 7168
LANE = 128
PAGE_ROWS = 128

VMEM_LIMIT = 56 * 1024 * 1024


def _cp(sem):
    return pltpu.CompilerParams(dimension_semantics=sem, vmem_limit_bytes=VMEM_LIMIT)


def _tile(n, prefs):
    for t in prefs:
        if n % t == 0:
            return t
    return n


def _dot(a, b, precision=None):
    return jnp.dot(a, b, preferred_element_type=F32, precision=precision)


def _dot_nt(a, b, precision=None):
    return lax.dot_general(a, b, (((1,), (1,)), ((), ())), preferred_element_type=F32, precision=precision)


def _dot_tn(a, b, precision=None):
    return lax.dot_general(a, b, (((0,), (0,)), ((), ())), preferred_element_type=F32, precision=precision)


def _layernorm_rows(y, g, b):
    mu = jnp.mean(y, axis=-1, keepdims=True)
    d = y - mu
    var = jnp.mean(d * d, axis=-1, keepdims=True)
    return d * lax.rsqrt(var + LN_EPS) * g + b


def _softplus(x):
    return jnp.maximum(x, 0.0) + jnp.log1p(jnp.exp(-jnp.abs(x)))


def _silu(x):
    return x * jax.nn.sigmoid(x)


def _ffn_kernel(x_ref, wg_ref, wu_ref, wo_ref, g_ref, b_ref, o_ref, xb_ref, *, nf):
    f = pl.program_id(1)

    @pl.when(f == 0)
    def _():
        xb_ref[...] = x_ref[...].astype(BF16)
        o_ref[...] = jnp.zeros_like(o_ref)

    xb = xb_ref[...]
    gate = _dot(xb, wg_ref[...].astype(BF16))
    up = _dot(xb, wu_ref[...].astype(BF16))
    act = (_silu(gate) * up).astype(BF16)
    o_ref[...] += _dot(act, wo_ref[...].astype(BF16))

    @pl.when(f == nf - 1)
    def _():
        y = ALPHA * x_ref[...] + 0.5 * o_ref[...]
        o_ref[...] = _layernorm_rows(y, g_ref[...], b_ref[...])


def _ffn_ln(x, wi, wo, layer, which, g, b):
    m, d = x.shape
    f = wo.shape[2]
    tm = _tile(m, (1024, 512, 256, 128, 64, 32, 16, 8))
    tf = _tile(f, (256, 128))
    nf = f // tf
    return pl.pallas_call(
        functools.partial(_ffn_kernel, nf=nf),
        grid=(m // tm, nf),
        in_specs=[
            pl.BlockSpec((tm, d), lambda i, j: (i, 0), pipeline_mode=pl.Buffered(1)),
            pl.BlockSpec((None, None, d, tf), lambda i, j: (layer, which, 0, j)),
            pl.BlockSpec((None, None, d, tf), lambda i, j: (layer, which, 0, j + nf)),
            pl.BlockSpec((None, None, tf, d), lambda i, j: (layer, which, j, 0)),
            pl.BlockSpec((1, d), lambda i, j: (0, 0)),
            pl.BlockSpec((1, d), lambda i, j: (0, 0)),
        ],
        out_specs=pl.BlockSpec((tm, d), lambda i, j: (i, 0)),
        out_shape=jax.ShapeDtypeStruct((m, d), F32),
        scratch_shapes=[pltpu.VMEM((tm, d), BF16)],
        compiler_params=_cp(("parallel", "arbitrary")),
        name="ffn_ln",
    )(x, wi, wi, wo, g, b)


def _mm_kernel(*refs, mix, act, pair_out):
    if mix == "array":
        x_ref, xp_ref, m_ref, w_ref, o_ref, xb_ref = refs
    elif mix == "shift":
        x_ref, prev_ref, first_ref, m_ref, w_ref, o_ref, xb_ref, hist_ref = refs
    else:
        x_ref, w_ref, o_ref, xb_ref = refs

    @pl.when(pl.program_id(1) == 0)
    def _():
        x = x_ref[...]
        if mix == "array":
            x = x + (xp_ref[...] - x) * m_ref[...]
        elif mix == "shift":
            tm = x_ref.shape[0]
            hist_ref[0:8, :] = jnp.where(pl.program_id(0) == 0, first_ref[...], prev_ref[...])
            hist_ref[8:8 + tm, :] = x
            x = x + (hist_ref[7:7 + tm, :] - x) * m_ref[...]
        xb_ref[...] = x.astype(BF16)

    y = _dot(xb_ref[...], w_ref[...])
    if act == "tanh":
        y = jnp.tanh(y)
    elif act == "sigmoid":
        y = jax.nn.sigmoid(y)
    if pair_out:
        for p in range(o_ref.shape[0]):
            o_ref[p] = y[:, p * LANE:(p + 1) * LANE]
    else:
        o_ref[...] = y


def _mm(x, w, *, xprev=None, first8=None, mixrow=None, act=None, pair_out=False, tn_prefs=(512, 256, 128)):
    m, k = x.shape
    n = w.shape[1]
    tm = _tile(m, (1024, 512, 256, 128, 64, 32, 16, 8))
    tn = _tile(n, tn_prefs)
    mix = "array" if xprev is not None else ("shift" if first8 is not None else None)
    in_specs = [pl.BlockSpec((tm, k), lambda i, j: (i, 0))]
    args = [x]
    scratch = [pltpu.VMEM((tm, k), BF16)]
    if mix == "array":
        in_specs += [pl.BlockSpec((tm, k), lambda i, j: (i, 0)), pl.BlockSpec((1, k), lambda i, j: (0, 0))]
        args += [xprev, mixrow]
    elif mix == "shift":
        in_specs += [pl.BlockSpec((8, k), lambda i, j: (jnp.maximum(i * (tm // 8) - 1, 0), 0)),
                     pl.BlockSpec((8, k), lambda i, j: (0, 0)), pl.BlockSpec((1, k), lambda i, j: (0, 0))]
        args += [x, first8, mixrow]
        scratch.append(pltpu.VMEM((tm + 8, k), F32))
    in_specs.append(pl.BlockSpec((k, tn), lambda i, j: (0, j)))
    args.append(w)
    if pair_out:
        npb = tn // LANE
        out_spec = pl.BlockSpec((npb, tm, LANE), lambda i, j: (j, i, 0))
        out_shape = jax.ShapeDtypeStruct((n // LANE, m, LANE), F32)
    else:
        out_spec = pl.BlockSpec((tm, tn), lambda i, j: (i, j))
        out_shape = jax.ShapeDtypeStruct((m, n), F32)
    return pl.pallas_call(
        functools.partial(_mm_kernel, mix=mix, act=act, pair_out=pair_out),
        grid=(m // tm, n // tn),
        in_specs=in_specs,
        out_specs=out_spec,
        out_shape=out_shape,
        scratch_shapes=scratch,
        compiler_params=_cp(("parallel", "arbitrary")),
        name="matmul",
    )(*args)


def _mmln_kernel(*refs, pair_in, n_parts):
    a_refs = refs[:n_parts]
    w_ref, x_ref, g_ref, b_ref, o_ref = refs[n_parts:]
    if pair_in:
        a_ref = a_refs[0]
        y = _dot(jnp.concatenate([a_ref[p].astype(BF16) for p in range(a_ref.shape[0])], axis=1), w_ref[...])
    else:
        y = None
        k0 = 0
        for a_ref in a_refs:
            kw = a_ref.shape[1]
            part = _dot(a_ref[...].astype(BF16), w_ref[k0:k0 + kw, :])
            y = part if y is None else y + part
            k0 += kw
    o_ref[...] = _layernorm_rows(ALPHA * x_ref[...] + y, g_ref[...], b_ref[...])


def _mm_ln(a_parts, w, x, g, b, *, pair_in=False):
    m, d = x.shape
    k = w.shape[0]
    tm = _tile(m, (512, 256, 128, 64, 32, 16, 8))
    if pair_in:
        a_specs = [pl.BlockSpec((k // LANE, tm, LANE), lambda i: (0, i, 0))]
    else:
        a_specs = [pl.BlockSpec((tm, a.shape[1]), lambda i: (i, 0)) for a in a_parts]
    return pl.pallas_call(
        functools.partial(_mmln_kernel, pair_in=pair_in, n_parts=len(a_parts)),
        grid=(m // tm,),
        in_specs=a_specs + [
            pl.BlockSpec((k, d), lambda i: (0, 0)),
            pl.BlockSpec((tm, d), lambda i: (i, 0)),
            pl.BlockSpec((1, d), lambda i: (0, 0)),
            pl.BlockSpec((1, d), lambda i: (0, 0)),
        ],
        out_specs=pl.BlockSpec((tm, d), lambda i: (i, 0)),
        out_shape=jax.ShapeDtypeStruct((m, d), F32),
        compiler_params=_cp(("parallel",)),
        name="matmul_ln",
    )(*a_parts, w, x, g, b)


CH_PER_PAGE = PAGE_ROWS // D_CMP
KV_PARTS = 2 * NSA_KV
PAGE_SUBROWS = PAGE_ROWS * KV_PARTS


def _cmp1_kernel(pt_ref, *refs, G):
    del pt_ref
    page_refs = refs[:G]
    w_ref, o_ref = refs[G:]
    rows = G * CH_PER_PAGE
    for sg in range(KV_PARTS):
        s = sg // NSA_KV
        acc = jnp.zeros((rows, 2 * HD), F32)
        for p in range(D_CMP):
            x = jnp.concatenate(
                [page_refs[j][0, pl.ds(p * KV_PARTS + sg, CH_PER_PAGE, stride=D_CMP * KV_PARTS), :]
                 for j in range(G)], axis=0)
            acc = acc + _dot(x.astype(BF16), w_ref[s, p])
        o_ref[0, :, sg * 2 * HD:(sg + 1) * 2 * HD] = acc


def _cmp_stage1(pool, page_table, w4):
    b, n_pages = page_table.shape
    G = _tile(n_pages, (16, 8, 4, 2, 1))

    def page_map(j):
        return lambda bi, p, pt: (pt[bi, p * G + j], 0, 0)

    in_specs = [pl.BlockSpec((1, PAGE_SUBROWS, HD), page_map(j)) for j in range(G)]
    in_specs.append(pl.BlockSpec(w4.shape, lambda bi, p, pt: (0, 0, 0, 0)))
    grid_spec = pltpu.PrefetchScalarGridSpec(
        num_scalar_prefetch=1,
        grid=(b, n_pages // G),
        in_specs=in_specs,
        out_specs=pl.BlockSpec((1, G * CH_PER_PAGE, 8 * HD), lambda bi, p, pt: (bi, p, 0)),
    )
    return pl.pallas_call(
        functools.partial(_cmp1_kernel, G=G),
        grid_spec=grid_spec,
        out_shape=jax.ShapeDtypeStruct((b, n_pages * CH_PER_PAGE, 8 * HD), F32),
        compiler_params=_cp(("parallel", "arbitrary")),
        name="nsa_cmp_stage1",
    )(page_table, *([pool] * G), w4)


def _gelu_tanh(x):
    return 0.5 * x * (1.0 + jnp.tanh(np.sqrt(2.0 / np.pi).astype(np.float32) * (x + 0.044715 * (x * x * x))))


def _cmp2_kernel(h_ref, pe_ref, w4_ref, b1_ref, w2_ref, o_ref):
    nch = h_ref.shape[1]
    for s in range(2):
        pacc = jnp.zeros((8, 2 * HD), F32)
        for p in range(D_CMP):
            pacc = pacc + _dot(pe_ref[s, p].astype(BF16), w4_ref[s, p])
        const = pacc[0:1, 0:HD] + pacc[1:2, HD:2 * HD] + b1_ref[s:s + 1, :]
        for gi in range(NSA_KV):
            sg = s * NSA_KV + gi
            h0 = h_ref[0, :, sg * 2 * HD:sg * 2 * HD + HD]
            h1 = h_ref[0, :, sg * 2 * HD + HD:(sg + 1) * 2 * HD]
            hid = h0 + pltpu.roll(h1, nch - 1, 0) + const
            o_ref[0, :, sg * HD:(sg + 1) * HD] = _dot(_gelu_tanh(hid).astype(BF16), w2_ref[s])


def _cmp_stage2(h, pe8, w4, b1, w2):
    b, nch, _ = h.shape
    return pl.pallas_call(
        _cmp2_kernel,
        grid=(b,),
        in_specs=[
            pl.BlockSpec((1, nch, 8 * HD), lambda i: (i, 0, 0)),
            pl.BlockSpec(pe8.shape, lambda i: (0, 0, 0, 0)),
            pl.BlockSpec(w4.shape, lambda i: (0, 0, 0, 0)),
            pl.BlockSpec(b1.shape, lambda i: (0, 0)),
            pl.BlockSpec(w2.shape, lambda i: (0, 0, 0)),
        ],
        out_specs=pl.BlockSpec((1, nch, A_KVW), lambda i: (i, 0, 0)),
        out_shape=jax.ShapeDtypeStruct((b, nch, A_KVW), F32),
        compiler_params=_cp(("parallel",)),
        name="nsa_cmp_stage2",
    )(h, pe8, w4, b1, w2)


def _nsa_sel_kernel(slopes_ref, q_ref, cmp_ref, ov_ref, ocmp_ref, sel_ref, *, TQ, NC, NS, q_off, n_pick, RS):
    i = pl.program_id(1)
    bb = q_ref.shape[0]
    nch = cmp_ref.shape[1]
    nsp = ov_ref.shape[1]
    qpos = q_off + i * TQ + lax.broadcasted_iota(jnp.int32, (TQ, 1), 0)
    cidx = lax.broadcasted_iota(jnp.int32, (1, nch), 1)
    cstart = cidx * D_CMP
    cmask = jnp.logical_and(cstart + (L_CMP - 1) <= qpos, cidx < NC)
    mask_add = jnp.where(cmask, 0.0, NEG_INF)
    mask_mul = jnp.where(cmask, 1.0, 0.0)
    cdist = (qpos - cstart).astype(F32) - 0.5 * (L_CMP - 1)
    sid = lax.broadcasted_iota(jnp.int32, (1, nsp), 1)
    cur = lax.shift_right_arithmetic(qpos, int(np.log2(L_SLC)))
    svalid = jnp.logical_and(sid * L_SLC <= qpos, sid < NS)
    forced = jnp.logical_or(sid == 0, jnp.logical_or(sid == cur, sid == cur - 1))
    ov = ov_ref[...]
    scores = []
    for bi in range(bb):
        q = q_ref[bi]
        cm = cmp_ref[bi]
        for g in range(NSA_KV):
            kb = cm[:, g * HD:(g + 1) * HD].astype(BF16)
            vb = cm[:, (NSA_KV + g) * HD:(NSA_KV + g + 1) * HD].astype(BF16)
            pcs = jnp.zeros((TQ, nch), F32)
            for j in range(NSA_GROUP):
                h = NSA_GROUP * g + j
                qj = (q[:, h * HD:(h + 1) * HD] * HD ** -0.5).astype(BF16)
                lg = _dot_nt(qj, kb) - slopes_ref[h] * cdist + mask_add
                e = jnp.exp(lg - jnp.max(lg, axis=-1, keepdims=True))
                p = e * (mask_mul * (1.0 / jnp.sum(e, axis=-1, keepdims=True)))
                pcs = pcs + p
                ocmp_ref[bi, :, h * HD:(h + 1) * HD] = _dot(p.astype(BF16), vb)
            imp = _dot(pcs, ov, precision=HI)
            scores.append(jnp.where(svalid, jnp.where(forced, FORCE_SCORE, imp), NEG_INF))
    sidb = jnp.broadcast_to(sid, (RS, nsp))
    validb = [jnp.logical_and(sid * L_SLC <= qpos[r0:r0 + RS], sid < NS) for _ in scores for r0 in range(0, TQ, RS)]
    chains = [sc[r0:r0 + RS] for sc in scores for r0 in range(0, TQ, RS)]
    sels = [jnp.zeros((RS, nsp), F32) for _ in chains]
    for _ in range(n_pick):
        firsts = [jnp.argmax(sc, axis=-1, keepdims=True).astype(jnp.int32) for sc in chains]
        picks = [sidb == f for f in firsts]
        sels = [jnp.where(jnp.logical_and(pk, vb), 1.0, sl) for pk, vb, sl in zip(picks, validb, sels)]
        chains = [jnp.where(pk, -jnp.inf, sc) for pk, sc in zip(picks, chains)]
    nsub = TQ // RS
    for bi in range(bb):
        for g in range(NSA_KV):
            for r in range(nsub):
                sel_ref[bi, g, r * RS:(r + 1) * RS, :] = sels[(bi * NSA_KV + g) * nsub + r]


def _nsa_select(q, cmp, overlap, slopes, *, TQ, NC, NS, q_off, BB):
    b, t, _ = q.shape
    nch = cmp.shape[1]
    nsp = overlap.shape[1]
    kern = functools.partial(_nsa_sel_kernel, TQ=TQ, NC=NC, NS=NS, q_off=q_off, n_pick=min(N_SEL, NS),
                             RS=_tile(TQ, (32, 16, 8)))
    return pl.pallas_call(
        kern,
        grid=(b // BB, t // TQ),
        in_specs=[
            pl.BlockSpec(memory_space=pltpu.SMEM),
            pl.BlockSpec((BB, TQ, A_Q), lambda bi, i: (bi, i, 0)),
            pl.BlockSpec((BB, nch, A_KVW), lambda bi, i: (bi, 0, 0)),
            pl.BlockSpec((nch, nsp), lambda bi, i: (0, 0)),
        ],
        out_specs=[
            pl.BlockSpec((BB, TQ, A_Q), lambda bi, i: (bi, i, 0)),
            pl.BlockSpec((BB, NSA_KV, TQ, nsp), lambda bi, i: (bi, 0, i, 0)),
        ],
        out_shape=[jax.ShapeDtypeStruct((b, t, A_Q), F32), jax.ShapeDtypeStruct((b, NSA_KV, t, nsp), F32)],
        compiler_params=_cp(("parallel", "parallel")),
        name="nsa_cmp_select",
    )(slopes, q, cmp, overlap)


def _stack_heads(q, tq):
    del tq
    return jnp.concatenate([q[:, j * HD:(j + 1) * HD] for j in range(NSA_GROUP)], axis=0)


def _online_step(carry, s, dist, mask_add, slopes, v):
    m, l, acc = carry
    lg = s + jnp.concatenate([mask_add - sl * dist for sl in slopes], axis=0)
    m_new = jnp.maximum(m, jnp.max(lg, axis=-1, keepdims=True))
    p = jnp.exp(lg - m_new)
    a = jnp.exp(m - m_new)
    l = a * l + jnp.sum(p, axis=-1, keepdims=True)
    acc = a * acc + _dot(p.astype(BF16), v)
    return m_new, l, acc


def _nsa_slc_kernel(slopes_ref, q_ref, sel_ref, k_ref, v_ref, kw_ref, vw_ref, ga_ref, ocmp_ref, o_ref, *, TQ, TK):
    g = pl.program_id(0)
    i = pl.program_id(1)
    nsp = sel_ref.shape[-1]
    rows = NSA_GROUP * TQ
    q4 = (_stack_heads(q_ref[...], TQ) * HD ** -0.5).astype(BF16)
    qposf = (i * TQ + lax.broadcasted_iota(jnp.int32, (TQ, 1), 0)).astype(F32)
    slopes = [slopes_ref[NSA_GROUP * g + j] for j in range(NSA_GROUP)]
    selb = sel_ref[0].astype(BF16)
    blk_per_tile = TK // L_SLC
    delta = (lax.broadcasted_iota(jnp.int32, (nsp, TK), 0)
             - lax.shift_right_arithmetic(lax.broadcasted_iota(jnp.int32, (nsp, TK), 1), int(np.log2(L_SLC))))
    kcol = lax.broadcasted_iota(jnp.int32, (1, TK), 1)

    def body(kt, carry):
        k0 = pl.multiple_of(kt * TK, TK)
        k = k_ref[pl.ds(k0, TK), :]
        v = v_ref[pl.ds(k0, TK), :]
        s = _dot_nt(q4, k)
        dist = qposf - (k0 + kcol).astype(F32)
        expand = jnp.where(delta == kt * blk_per_tile, 1.0, 0.0).astype(BF16)
        se = _dot(selb, expand)
        mask_add = jnp.where(jnp.logical_and(se > 0.5, dist >= 0.0), 0.0, NEG_INF)
        return _online_step(carry, s, dist, mask_add, slopes, v)

    ntile = (i * TQ + TQ + TK - 1) // TK
    init = (jnp.full((rows, 1), NEG_INF, F32), jnp.zeros((rows, 1), F32), jnp.zeros((rows, HD), F32))
    _, l, acc = lax.fori_loop(0, ntile, body, init)
    o_slc = acc / l

    span = min(WINDOW + TQ, kw_ref.shape[0])
    w0 = pl.multiple_of(jnp.maximum((i + 1) * TQ - span, 0), TQ)
    wdist = qposf - (w0 + lax.broadcasted_iota(jnp.int32, (1, span), 1)).astype(F32)
    wmask = jnp.where(jnp.logical_and(wdist >= 0.0, wdist <= float(WINDOW)), 0.0, NEG_INF)
    kw = kw_ref[pl.ds(w0, span), :]
    _, lw, accw = _online_step(init, _dot_nt(q4, kw), wdist, wmask, slopes, vw_ref[pl.ds(w0, span), :])
    o_win = accw / lw

    gates = jax.nn.sigmoid(ga_ref[...])
    for j in range(NSA_GROUP):
        ca = 3 * j
        cb = 3 * (NSA_GROUP + j)

        def gate(c):
            return jnp.where(g == 0, gates[:, ca + c:ca + c + 1], gates[:, cb + c:cb + c + 1])

        sl = slice(j * HD, (j + 1) * HD)
        rw = slice(j * TQ, (j + 1) * TQ)
        o_ref[:, sl] = gate(0) * ocmp_ref[:, sl] + gate(1) * o_slc[rw] + gate(2) * o_win[rw]


def _nsa_slc_win_prompt(h_in, sel, kvb, o_cmp, slopes, *, TQ, TK):
    t = h_in.shape[0]
    nsp = sel.shape[-1]
    ks0 = A_KVW // HD
    kw0 = 2 * A_KVW // HD
    hspec = pl.BlockSpec((TQ, NSA_GROUP * HD), lambda g, i: (i, g))
    kv = lambda c: pl.BlockSpec((t, HD), lambda g, i: (0, c + g))
    return pl.pallas_call(
        functools.partial(_nsa_slc_kernel, TQ=TQ, TK=TK),
        grid=(NSA_KV, t // TQ),
        in_specs=[
            pl.BlockSpec(memory_space=pltpu.SMEM),
            hspec,
            pl.BlockSpec((1, TQ, nsp), lambda g, i: (g, i, 0)),
            kv(ks0), kv(ks0 + NSA_KV), kv(kw0), kv(kw0 + NSA_KV),
            pl.BlockSpec((TQ, LANE), lambda g, i: (i, COL_SM // LANE)),
            hspec,
        ],
        out_specs=hspec,
        out_shape=jax.ShapeDtypeStruct((t, A_Q), F32),
        compiler_params=_cp(("parallel", "parallel")),
        name="nsa_slc_win_prompt",
    )(slopes, h_in, sel, kvb, kvb, kvb, kvb, h_in, o_cmp)


def _rows8(row, width):
    return jnp.concatenate([row[:, h * width:(h + 1) * width] for h in range(NSA_HEADS)], axis=0)


def _kv_rows8(kn, off):
    return jnp.concatenate(
        [kn[:, off + (h // NSA_GROUP) * HD: off + (h // NSA_GROUP + 1) * HD] for h in range(NSA_HEADS)], axis=0)


def _slope8(slopes_ref):
    hrow = lax.broadcasted_iota(jnp.int32, (NSA_HEADS, 1), 0)
    out = jnp.zeros((NSA_HEADS, 1), F32)
    for h in range(NSA_HEADS):
        out = jnp.where(hrow == h, slopes_ref[h], out)
    return out


def _bf(x):
    return x.astype(BF16).astype(F32)


def _nsa_slc_dec_kernel(pt_ref, slopes_ref, *refs, G, past):
    del pt_ref
    pages = refs[:G]
    q_ref, sel_ref, knew_ref, o_ref, kv_s, m_s, l_s, acc_s = refs[G:]
    pg = pl.program_id(1)
    nsp = sel_ref.shape[-1]
    tk = G * PAGE_ROWS
    q8 = _rows8(q_ref[0], HD) * HD ** -0.5
    q8b = q8.astype(BF16)
    hrow = lax.broadcasted_iota(jnp.int32, (NSA_HEADS, 1), 0)
    first_group = hrow < NSA_GROUP

    @pl.when(pg == 0)
    def _():
        kn = knew_ref[0]
        m_s[...] = jnp.sum(_bf(q8) * _bf(_kv_rows8(kn, 0)), axis=-1, keepdims=True)
        l_s[...] = jnp.ones_like(l_s)
        acc_s[...] = _bf(_kv_rows8(kn, NSA_KV * HD))

    for j in range(G):
        for sg in range(KV_PARTS):
            kv_s[sg, j * PAGE_ROWS:(j + 1) * PAGE_ROWS, :] = pages[j][0, pl.ds(sg, PAGE_ROWS, stride=KV_PARTS), :]
    kpos = pg * tk + lax.broadcasted_iota(jnp.int32, (1, tk), 1)
    dist = (past - kpos).astype(F32)
    srow = lax.broadcasted_iota(jnp.int32, (nsp, tk), 0)
    scol = (pg * tk + lax.broadcasted_iota(jnp.int32, (nsp, tk), 1)) // L_SLC
    expand = jnp.where(srow == scol, 1.0, 0.0).astype(BF16)
    sel2 = sel_ref[0]
    sel8 = jnp.where(first_group, sel2[0:1, :], sel2[1:2, :]).astype(BF16)
    mask = _dot(sel8, expand) > 0.5
    s8 = jnp.where(first_group, _dot_nt(q8b, kv_s[0].astype(BF16)), _dot_nt(q8b, kv_s[1].astype(BF16)))
    lg = jnp.where(mask, s8 - _slope8(slopes_ref) * dist, NEG_INF)
    m = m_s[...]
    m_new = jnp.maximum(m, jnp.max(lg, axis=-1, keepdims=True))
    p = jnp.where(mask, jnp.exp(lg - m_new), 0.0)
    a = jnp.exp(m - m_new)
    pb = p.astype(BF16)
    pv = jnp.where(first_group, _dot(pb, kv_s[2].astype(BF16)), _dot(pb, kv_s[3].astype(BF16)))
    m_s[...] = m_new
    l_s[...] = a * l_s[...] + jnp.sum(p, axis=-1, keepdims=True)
    acc_s[...] = a * acc_s[...] + pv

    @pl.when(pg == pl.num_programs(1) - 1)
    def _():
        o = acc_s[...] / l_s[...]
        for h in range(NSA_HEADS):
            o_ref[0, :, h * HD:(h + 1) * HD] = o[h:h + 1, :]


def _nsa_slc_sample(q, sel, knew, pool, page_table, slopes):
    b, n_pages = page_table.shape
    G = _tile(n_pages, (32, 16, 8, 4, 2, 1))
    nsp = sel.shape[-1]
    past = n_pages * PAGE_ROWS

    def page_map(j):
        return lambda bi, p, pt: (pt[bi, p * G + j], 0, 0)

    in_specs = [pl.BlockSpec(memory_space=pltpu.SMEM)]
    in_specs += [pl.BlockSpec((1, PAGE_SUBROWS, HD), page_map(j)) for j in range(G)]
    in_specs += [
        pl.BlockSpec((1, 1, A_Q), lambda bi, p, pt: (bi, 0, 0)),
        pl.BlockSpec((1, NSA_KV, nsp), lambda bi, p, pt: (bi, 0, 0)),
        pl.BlockSpec((1, 1, A_KVW), lambda bi, p, pt: (bi, 0, 0)),
    ]
    grid_spec = pltpu.PrefetchScalarGridSpec(
        num_scalar_prefetch=1,
        grid=(b, n_pages // G),
        in_specs=in_specs,
        out_specs=pl.BlockSpec((1, 1, A_Q), lambda bi, p, pt: (bi, 0, 0)),
        scratch_shapes=[pltpu.VMEM((KV_PARTS, G * PAGE_ROWS, HD), F32), pltpu.VMEM((NSA_HEADS, 1), F32),
                        pltpu.VMEM((NSA_HEADS, 1), F32), pltpu.VMEM((NSA_HEADS, HD), F32)],
    )
    return pl.pallas_call(
        functools.partial(_nsa_slc_dec_kernel, G=G, past=past),
        grid_spec=grid_spec,
        out_shape=jax.ShapeDtypeStruct((b, 1, A_Q), F32),
        compiler_params=_cp(("parallel", "arbitrary")),
        name="nsa_slc_sample",
    )(page_table, slopes, *([pool] * G), q, sel, knew)


def _nsa_win_dec_kernel(slopes_ref, q_ref, wb_ref, knew_ref, ga_ref, ocmp_ref, oslc_ref, o_ref, *, past):
    nb = wb_ref.shape[1] // KV_PARTS
    q8 = _rows8(q_ref[0], HD) * HD ** -0.5
    q8b = q8.astype(BF16)
    hrow = lax.broadcasted_iota(jnp.int32, (NSA_HEADS, 1), 0)
    first_group = hrow < NSA_GROUP
    kn = knew_ref[0]
    wb = [wb_ref[0, pl.ds(sg, nb, stride=KV_PARTS), :].astype(BF16) for sg in range(KV_PARTS)]
    kwpos = past - nb + lax.broadcasted_iota(jnp.int32, (1, nb), 1)
    wd = past - kwpos
    mask = jnp.logical_and(jnp.logical_and(wd >= 0, wd <= WINDOW), kwpos >= 0)
    s8 = jnp.where(first_group, _dot_nt(q8b, wb[0]), _dot_nt(q8b, wb[1]))
    lg = jnp.where(mask, s8 - _slope8(slopes_ref) * wd.astype(F32), NEG_INF)
    s_self = jnp.sum(_bf(q8) * _bf(_kv_rows8(kn, 0)), axis=-1, keepdims=True)
    m = jnp.maximum(jnp.max(lg, axis=-1, keepdims=True), s_self)
    p = jnp.where(mask, jnp.exp(lg - m), 0.0)
    p_self = jnp.exp(s_self - m)
    pb = p.astype(BF16)
    pv = jnp.where(first_group, _dot(pb, wb[2]), _dot(pb, wb[3]))
    pv = pv + _bf(p_self) * _bf(_kv_rows8(kn, NSA_KV * HD))
    o_win = pv / (jnp.sum(p, axis=-1, keepdims=True) + p_self)
    gates = jax.nn.sigmoid(ga_ref[0])
    ocmp = ocmp_ref[0]
    oslc = oslc_ref[0]
    for h in range(NSA_HEADS):
        sl = slice(h * HD, (h + 1) * HD)
        o_ref[0, :, sl] = (gates[:, 3 * h:3 * h + 1] * ocmp[:, sl] + gates[:, 3 * h + 1:3 * h + 2] * oslc[:, sl]
                           + gates[:, 3 * h + 2:3 * h + 3] * o_win[h:h + 1, :])


def _nsa_win_sample(q, win_buf, knew, ga, o_cmp, o_slc, slopes, *, past):
    b = q.shape[0]
    nsub = win_buf.shape[1]
    row = lambda w: pl.BlockSpec((1, 1, w), lambda bi: (bi, 0, 0))
    return pl.pallas_call(
        functools.partial(_nsa_win_dec_kernel, past=past),
        grid=(b,),
        in_specs=[pl.BlockSpec(memory_space=pltpu.SMEM), row(A_Q),
                  pl.BlockSpec((1, nsub, HD), lambda bi: (bi, 0, 0)), row(A_KVW), row(LANE), row(A_Q), row(A_Q)],
        out_specs=row(A_Q),
        out_shape=jax.ShapeDtypeStruct((b, 1, A_Q), F32),
        compiler_params=_cp(("parallel",)),
        name="nsa_win_sample",
    )(slopes, q, win_buf, knew, ga, o_cmp, o_slc)


def _dn_conv_kernel(x_ref, prev_ref, buf_ref, w_ref, o_ref, hist_ref):
    c = pl.program_id(0)
    t = pl.program_id(1)
    tt = x_ref.shape[0]
    hist_ref[0:8, :] = jnp.where(t == 0, buf_ref[...], prev_ref[...])
    hist_ref[8:8 + tt, :] = x_ref[...]
    w = w_ref[...]
    y = jnp.zeros((tt, LANE), F32)
    for i in range(CONV_W):
        y = y + w[i:i + 1, :] * hist_ref[8 - (CONV_W - 1) + i: 8 - (CONV_W - 1) + i + tt, :]
    y = _silu(y)
    nrm = y * lax.rsqrt(jnp.sum(y * y, axis=-1, keepdims=True) + 1e-6)
    o_ref[...] = jnp.where(c < DN_HEADS, nrm * DK ** -0.5, jnp.where(c < 2 * DN_HEADS, nrm, y))


def _dn_conv_prompt(h_in, buf8, conv_wt):
    t = h_in.shape[0]
    tt = _tile(t, (1024, 512, 256, 128, 64, 32, 16, 8))
    c0 = COL_DQKV // LANE
    return pl.pallas_call(
        _dn_conv_kernel,
        grid=(DN_QKV // LANE, t // tt),
        in_specs=[
            pl.BlockSpec((tt, LANE), lambda c, i: (i, c0 + c)),
            pl.BlockSpec((8, LANE), lambda c, i: (jnp.maximum(i * (tt // 8) - 1, 0), c0 + c)),
            pl.BlockSpec((8, LANE), lambda c, i: (0, c)),
            pl.BlockSpec((CONV_W, LANE), lambda c, i: (0, c)),
        ],
        out_specs=pl.BlockSpec((tt, LANE), lambda c, i: (i, c)),
        out_shape=jax.ShapeDtypeStruct((t, DN_QKV), F32),
        scratch_shapes=[pltpu.VMEM((tt + 8, LANE), F32)],
        compiler_params=_cp(("parallel", "parallel")),
        name="dn_conv",
    )(h_in, h_in, buf8, conv_wt)


def _dot1(x, y):
    return _dot(x.astype(BF16), y.astype(BF16))


def _unit_lower_inverse_many(mats, n, mm):
    ri = lax.broadcasted_iota(jnp.int32, (n, n), 0)
    ci = lax.broadcasted_iota(jnp.int32, (n, n), 1)
    eye = jnp.where(ri == ci, 1.0, 0.0).astype(F32)
    base = min(16, n)

    def same_block(b):
        return (ri // b) == (ci // b)

    ps = [jnp.where(same_block(base), -a, 0.0) for a in mats]
    rs = [eye + p for p in ps]
    for _ in range(int(np.log2(base)) - 1):
        ps = [mm(p, p) for p in ps]
        rs = [r + mm(r, p) for r, p in zip(rs, ps)]
    b = base
    while b < n:
        offm = jnp.logical_and(same_block(2 * b), jnp.logical_not(same_block(b)))
        ts = [mm(jnp.where(offm, a, 0.0), r) for a, r in zip(mats, rs)]
        rs = [r - mm(r, t) for r, t in zip(rs, ts)]
        b *= 2
    return rs


def _dn_chunk_kernel(q_ref, k_ref, v_ref, sm_ref, smt_ref, z_ref, prow_ref, pcol_ref, nw_ref,
                     o_ref, s_out_ref, s_ref):
    c = pl.program_id(0)
    C = q_ref.shape[0]

    @pl.when(c == 0)
    def _():
        s_ref[...] = jnp.zeros_like(s_ref)

    ri = lax.broadcasted_iota(jnp.int32, (C, C), 0)
    ci = lax.broadcasted_iota(jnp.int32, (C, C), 1)
    lower = ri >= ci
    tril = jnp.where(lower, 1.0, 0.0).astype(F32)
    triu = jnp.where(ri <= ci, 1.0, 0.0).astype(F32)
    ri2 = lax.broadcasted_iota(jnp.int32, (2 * C, C), 0)
    ci2 = lax.broadcasted_iota(jnp.int32, (2 * C, C), 1)
    mask2 = jnp.where(ri2 < C, ri2, ri2 - (C - 1)) > ci2
    sm = sm_ref[...]
    smt = smt_ref[...]
    g_cols = -jnp.exp(prow_ref[0:1, :]) * _softplus(sm + prow_ref[1:2, :])
    g_rows = -jnp.exp(pcol_ref[:, 0:1]) * _softplus(smt + pcol_ref[:, 1:2])
    gcum_cols = _dot(tril, g_cols, HI)
    gcum_rows = _dot(g_rows, triu, HI)
    beta_cols = jax.nn.sigmoid(sm)
    nw = nw_ref[...]
    heads = []
    for h in range(DN_HEADS):
        sl = slice(h * DK, (h + 1) * DK)
        q = q_ref[:, sl]
        k = k_ref[:, sl]
        gc = gcum_cols[:, SM_A + h:SM_A + h + 1]
        gr = gcum_rows[SM_A + h:SM_A + h + 1, :]
        beta = beta_cols[:, SM_BETA + h:SM_BETA + h + 1]
        decay = jnp.where(lower, jnp.exp(jnp.where(lower, gc - gr, 0.0)), 0.0)
        kb = k * beta
        egc = jnp.exp(gc)
        gl = gc[C - 1:C, :]
        heads.append(dict(
            h=h, sl=sl, kb16=k.astype(BF16), decay2=jnp.concatenate([decay, decay], axis=0),
            kbq=jnp.concatenate([kb, q], axis=0).astype(BF16),
            rhs=jnp.concatenate([v_ref[:, sl] * beta, kb * egc], axis=1).astype(BF16),
            qg=q * egc, kdec=(k * jnp.exp(gl - gc)).astype(BF16), egl=jnp.exp(gl)))
    for d in heads:
        d["aa"] = jnp.where(mask2, _dot_nt(d["kbq"], d["kb16"]) * d["decay2"], 0.0)
    tinv = _unit_lower_inverse_many([d["aa"][:C] for d in heads], C, _dot1)
    for d, ti in zip(heads, tinv):
        d["sol"] = _dot(ti.astype(BF16), d["rhs"])
    for d in heads:
        d["s"] = s_ref[d["h"]]
        d["ks2"] = _dot1(jnp.concatenate([d["sol"][:, DV:], d["qg"]], axis=0), d["s"])
    for d in heads:
        d["v_new"] = (d["sol"][:, :DV] - d["ks2"][:C]).astype(BF16)
    for d in heads:
        s_ref[d["h"]] = d["s"] * d["egl"] + _dot_tn(d["kdec"], d["v_new"])
    for d in heads:
        o = d["ks2"][C:] + _dot(d["aa"][C:].astype(BF16), d["v_new"])
        o = o * lax.rsqrt(jnp.mean(o * o, axis=-1, keepdims=True) + RMS_EPS) * nw
        o_ref[:, d["sl"]] = o * _silu(z_ref[:, d["sl"]])

    @pl.when(c == pl.num_programs(0) - 1)
    def _():
        s_out_ref[...] = s_ref[...]


def _dn_chunk_prompt(qkvn, h_in, smt, prow, pcol, norm_w, *, C):
    t = qkvn.shape[0]
    hw = DN_HEADS * DK
    return pl.pallas_call(
        _dn_chunk_kernel,
        grid=(t // C,),
        in_specs=[
            pl.BlockSpec((C, hw), lambda c: (c, 0)),
            pl.BlockSpec((C, hw), lambda c: (c, 1)),
            pl.BlockSpec((C, hw), lambda c: (c, 2)),
            pl.BlockSpec((C, LANE), lambda c: (c, COL_SM // LANE)),
            pl.BlockSpec((LANE, C), lambda c: (0, c)),
            pl.BlockSpec((C, hw), lambda c: (c, COL_Z // hw)),
            pl.BlockSpec((2, LANE), lambda c: (0, 0)),
            pl.BlockSpec((LANE, 2), lambda c: (0, 0)),
            pl.BlockSpec((1, DV), lambda c: (0, 0)),
        ],
        out_specs=[pl.BlockSpec((C, hw), lambda c: (c, 0)),
                   pl.BlockSpec((DN_HEADS, DK, DV), lambda c: (0, 0, 0))],
        out_shape=[jax.ShapeDtypeStruct((t, hw), F32), jax.ShapeDtypeStruct((DN_HEADS, DK, DV), F32)],
        scratch_shapes=[pltpu.VMEM((DN_HEADS, DK, DV), F32)],
        compiler_params=_cp(("arbitrary",)),
        name="dn_chunk",
    )(qkvn, qkvn, qkvn, h_in, smt, h_in, prow, pcol, norm_w)


def _row_to_col(row, n):
    ri = lax.broadcasted_iota(jnp.int32, (n, n), 0)
    ci = lax.broadcasted_iota(jnp.int32, (n, n), 1)
    return jnp.sum(jnp.where(ri == ci, jnp.broadcast_to(row, (n, n)), 0.0), axis=1, keepdims=True)


def _col_to_row(col, n):
    ri = lax.broadcasted_iota(jnp.int32, (n, n), 0)
    ci = lax.broadcasted_iota(jnp.int32, (n, n), 1)
    return jnp.sum(jnp.where(ri == ci, jnp.broadcast_to(col, (n, n)), 0.0), axis=0, keepdims=True)


def _dn_dec_kernel(buf_ref, xq_ref, xk_ref, xv_ref, w_ref, sm_ref, z_ref, prow_ref, nw_ref, s0_ref,
                   o_ref, s_out_ref):
    hw = DN_HEADS * DK
    buf = buf_ref[0]
    w = w_ref[...]
    sm = sm_ref[0]
    g_row = -jnp.exp(prow_ref[0:1, :]) * _softplus(sm + prow_ref[1:2, :])
    beta_row = jax.nn.sigmoid(sm)
    nw = nw_ref[...]
    z = z_ref[0]
    parts = []
    for part, x_ref in enumerate((xq_ref, xk_ref, xv_ref)):
        sl = slice(part * hw, (part + 1) * hw)
        y = w[CONV_W - 1:CONV_W, sl] * x_ref[0]
        for i in range(CONV_W - 1):
            y = y + w[i:i + 1, sl] * buf[i:i + 1, sl]
        parts.append(_silu(y))
    for h in range(DN_HEADS):
        sl = slice(h * DK, (h + 1) * DK)
        q = parts[0][:, sl]
        k = parts[1][:, sl]
        v = parts[2][:, sl]
        q = q * lax.rsqrt(jnp.sum(q * q, axis=-1, keepdims=True) + 1e-6) * DK ** -0.5
        k = k * lax.rsqrt(jnp.sum(k * k, axis=-1, keepdims=True) + 1e-6)
        a = jnp.exp(g_row[:, SM_A + h:SM_A + h + 1])
        beta = beta_row[:, SM_BETA + h:SM_BETA + h + 1]
        k_col = _row_to_col(k, DK)
        q_col = _row_to_col(q, DK)
        s0 = s0_ref[0, h]
        u = beta * (v - a * jnp.sum(s0 * k_col, axis=0, keepdims=True))
        s_new = a * s0 + k_col * u
        s_out_ref[0, h] = s_new
        o = jnp.sum(s_new * q_col, axis=0, keepdims=True)
        o = o * lax.rsqrt(jnp.mean(o * o, axis=-1, keepdims=True) + RMS_EPS) * nw
        o_ref[0, :, sl] = o * _silu(z[:, sl])


def _dn_sample(conv_buf, h3, conv_wt, prow, norm_w, s0):
    b = h3.shape[0]
    hw = DN_HEADS * DK
    c0 = COL_DQKV // hw
    row = lambda w, j: pl.BlockSpec((1, 1, w), lambda bi: (bi, 0, j))
    return pl.pallas_call(
        _dn_dec_kernel,
        grid=(b,),
        in_specs=[
            pl.BlockSpec((1, CONV_W - 1, DN_QKV), lambda bi: (bi, 0, 0)),
            row(hw, c0), row(hw, c0 + 1), row(hw, c0 + 2),
            pl.BlockSpec((CONV_W, DN_QKV), lambda bi: (0, 0)),
            row(LANE, COL_SM // LANE),
            row(hw, COL_Z // hw),
            pl.BlockSpec((2, LANE), lambda bi: (0, 0)),
            pl.BlockSpec((1, DV), lambda bi: (0, 0)),
            pl.BlockSpec((1, DN_HEADS, DK, DV), lambda bi: (bi, 0, 0, 0)),
        ],
        out_specs=[row(hw, 0), pl.BlockSpec((1, DN_HEADS, DK, DV), lambda bi: (bi, 0, 0, 0))],
        out_shape=[jax.ShapeDtypeStruct((b, 1, hw), F32), jax.ShapeDtypeStruct(s0.shape, F32)],
        compiler_params=_cp(("parallel",)),
        name="dn_sample",
    )(conv_buf, h3, h3, h3, conv_wt, h3, h3, prow, norm_w, s0)


def _rwkv_prep(r, k, wl, al, w0, a0, k_k, k_a):
    w_log = -_softplus(-(w0 + wl)) - 0.5
    log_decay = -jnp.exp(w_log)
    a = jax.nn.sigmoid(a0 + al)
    kk_raw = k * k_k
    k_h = k * (1.0 + (a - 1.0) * k_a)
    del r
    return log_decay, a, kk_raw, k_h


def _rwkv_chunk_kernel(r_ref, k_ref, v_ref, wl_ref, al_ref, gate_ref, prm_ref, o_ref, s_out_ref, s_ref):
    c = pl.program_id(1)
    PP = r_ref.shape[0]
    C = r_ref.shape[1]
    N = RWKV_HS
    hpp = LANE // N

    @pl.when(c == 0)
    def _():
        s_ref[...] = jnp.zeros_like(s_ref)

    ri = lax.broadcasted_iota(jnp.int32, (2 * C, C), 0)
    ci = lax.broadcasted_iota(jnp.int32, (2 * C, C), 1)
    mask2 = jnp.where(ri < C, ri, ri - (C - 1)) > ci
    trilb = jnp.where(lax.broadcasted_iota(jnp.int32, (C, C), 0) >= lax.broadcasted_iota(jnp.int32, (C, C), 1),
                      1.0, 0.0).astype(BF16)
    heads = []
    for pp in range(PP):
        prm = prm_ref[pp]
        r2 = r_ref[pp]
        v2 = v_ref[pp]
        log_decay2, a2, kk_raw2, kh2 = _rwkv_prep(r2, k_ref[pp], wl_ref[pp], al_ref[pp],
                                                  prm[0:1], prm[1:2], prm[2:3], prm[3:4])
        ld_h = log_decay2.astype(BF16)
        rem = log_decay2 - ld_h.astype(F32)
        ld_m = rem.astype(BF16)
        ld_l = (rem - ld_m.astype(F32)).astype(BF16)
        gcum2 = _dot(trilb, ld_h) + (_dot(trilb, ld_m) + _dot(trilb, ld_l))
        for hh in range(hpp):
            sl = slice(hh * N, (hh + 1) * N)
            r = r2[:, sl]
            v = v2[:, sl]
            k_h = kh2[:, sl]
            kk = kk_raw2[:, sl]
            kk = kk / jnp.maximum(jnp.sqrt(jnp.sum(kk * kk, axis=-1, keepdims=True)), 1e-12)
            gc = gcum2[:, sl]
            p_incl = jnp.exp(gc)
            p_inv = jnp.exp(-gc)
            at = -kk * jnp.exp(gc - log_decay2[:, sl])
            bt = kk * a2[:, sl] * p_inv
            kt = k_h * p_inv
            rt = r * p_incl
            heads.append(dict(
                idx=pp * hpp + hh, v=v, p_last=p_incl[C - 1:C, :],
                lhs=jnp.concatenate([at, rt], axis=0).astype(BF16),
                bk=jnp.concatenate([bt, kt], axis=0).astype(BF16),
                bonus=jnp.sum(r * k_h * prm[4:5, sl], axis=-1, keepdims=True) * v,
                ln_w=prm[5:6, sl], ln_b=prm[6:7, sl]))
    for h in heads:
        h["g_b"] = jnp.where(mask2, _dot_nt(h["lhs"], h["bk"][:C]), 0.0)
    tinv = _unit_lower_inverse_many([-h["g_b"][:C] for h in heads], C, _dot1)
    for h in heads:
        h["g_k"] = jnp.where(mask2, _dot_nt(h["lhs"], h["bk"][C:]), 0.0)
    for h in heads:
        h["w_kv"] = _dot1(h["g_k"], h["v"])
    for h in heads:
        h["s0"] = s_ref[h["idx"]]
        h["g_s"] = _dot_nt(h["lhs"], h["s0"].astype(BF16))
    for h, ti in zip(heads, tinv):
        h["u"] = _dot1(ti, h["g_s"][:C] + h["w_kv"][:C])
    for h in heads:
        uv = jnp.concatenate([h["u"], h["v"]], axis=0).astype(BF16)
        s_ref[h["idx"]] = (h["s0"] + _dot_tn(uv, h["bk"])) * h["p_last"]
    outs = []
    for h in heads:
        out = h["g_s"][C:] + h["w_kv"][C:] + _dot1(h["g_b"][C:], h["u"])
        mu = jnp.mean(out, axis=-1, keepdims=True)
        d = out - mu
        var = jnp.mean(d * d, axis=-1, keepdims=True)
        outs.append(d * lax.rsqrt(var + GN_EPS) * h["ln_w"] + h["ln_b"] + h["bonus"])
    for pp in range(PP):
        o_ref[pp] = jnp.concatenate(outs[pp * hpp:(pp + 1) * hpp], axis=1) * gate_ref[pp]

    @pl.when(c == pl.num_programs(1) - 1)
    def _():
        s_out_ref[...] = s_ref[...]


def _rwkv_chunk_prompt(r, k, v, wl, al, gate, prm, *, C, PP):
    npair, t, _ = r.shape
    hpp = LANE // RWKV_HS
    seq = pl.BlockSpec((PP, C, LANE), lambda p, c: (p, c, 0))
    return pl.pallas_call(
        _rwkv_chunk_kernel,
        grid=(npair // PP, t // C),
        in_specs=[seq] * 6 + [pl.BlockSpec((PP, 8, LANE), lambda p, c: (p, 0, 0))],
        out_specs=[seq, pl.BlockSpec((PP * hpp, RWKV_HS, RWKV_HS), lambda p, c: (p, 0, 0))],
        out_shape=[jax.ShapeDtypeStruct((npair, t, LANE), F32),
                   jax.ShapeDtypeStruct((npair * hpp, RWKV_HS, RWKV_HS), F32)],
        scratch_shapes=[pltpu.VMEM((PP * hpp, RWKV_HS, RWKV_HS), F32)],
        compiler_params=_cp(("parallel", "arbitrary")),
        name="rwkv_chunk",
    )(r, k, v, wl, al, gate, prm)


def _rwkv_dec_kernel(r_ref, k_ref, v_ref, wl_ref, al_ref, gate_ref, prm_ref, s0_ref, o_ref, s_out_ref):
    N = RWKV_HS
    prm = prm_ref[...]
    r2 = r_ref[0]
    k2 = k_ref[0]
    v2 = v_ref[0]
    log_decay2, a2, kk_raw2, kh2 = _rwkv_prep(r2, k2, wl_ref[0], al_ref[0], prm[0:1], prm[1:2], prm[2:3], prm[3:4])
    w2 = jnp.exp(log_decay2)
    gate = gate_ref[0]
    nh = r2.shape[1] // N
    sls = [slice(h * N, (h + 1) * N) for h in range(nh)]
    lane_sum = lambda xs: [jnp.sum(x, axis=-1, keepdims=True) for x in xs]
    kks = [kk_raw2[:, sl] for sl in sls]
    kks = [kk / jnp.maximum(jnp.sqrt(n2), 1e-12) for kk, n2 in zip(kks, lane_sum([kk * kk for kk in kks]))]
    s0s = [s0_ref[0, h] for h in range(nh)]
    sas = lane_sum([s0 * (-kk) for s0, kk in zip(s0s, kks)])
    vcols = [_row_to_col(v2[:, sl], N) for sl in sls]
    s_news = [s0 * w2[:, sl] + sa * (kk * a2[:, sl]) + vc * kh2[:, sl]
              for s0, sl, sa, kk, vc in zip(s0s, sls, sas, kks, vcols)]
    for h in range(nh):
        s_out_ref[0, h] = s_news[h]
    outs = [_col_to_row(oc, N) for oc in lane_sum([sn * r2[:, sl] for sn, sl in zip(s_news, sls)])]
    mus = lane_sum(outs)
    ds = [o - mu * (1.0 / N) for o, mu in zip(outs, mus)]
    vars_ = lane_sum([d * d for d in ds])
    bon = lane_sum([r2[:, sl] * kh2[:, sl] * prm[4:5, sl] for sl in sls])
    for sl, d, var, bo in zip(sls, ds, vars_, bon):
        gn = d * lax.rsqrt(var * (1.0 / N) + GN_EPS) * prm[5:6, sl] + prm[6:7, sl]
        o_ref[0, :, sl] = (gn + bo * v2[:, sl]) * gate[:, sl]


def _rwkv_sample(r, k, v, wl, al, gate, prm, s0):
    b, _, d = r.shape
    row = pl.BlockSpec((1, 1, d), lambda bi: (bi, 0, 0))
    st = pl.BlockSpec((1,) + s0.shape[1:], lambda bi: (bi, 0, 0, 0))
    return pl.pallas_call(
        _rwkv_dec_kernel,
        grid=(b,),
        in_specs=[row] * 6 + [pl.BlockSpec((8, d), lambda bi: (0, 0)), st],
        out_specs=[row, st],
        out_shape=[jax.ShapeDtypeStruct((b, 1, d), F32), jax.ShapeDtypeStruct(s0.shape, F32)],
        compiler_params=_cp(("parallel",)),
        name="rwkv_sample",
    )(r, k, v, wl, al, gate, prm, s0)


def _alibi_slopes():
    return jnp.asarray(2.0 ** (-8.0 * np.arange(1, NSA_HEADS + 1) / NSA_HEADS), dtype=F32)


def _overlap_matrix(nch, nsp):
    cstart = np.arange(nch)[:, None] * D_CMP
    sstart = np.arange(nsp)[None, :] * L_SLC
    return jnp.asarray(((cstart < sstart + L_SLC) & (cstart + L_CMP > sstart)).astype(np.float32))


def _pack_w_in(w):
    offs = np.concatenate([[0], np.cumsum(EVEN_SPLIT)])
    qa, kvc, kvs, kvw, ga, qkv, z, b, a = [w[:, offs[i]:offs[i + 1]] for i in range(len(EVEN_SPLIT))]
    used = COL_SM + A_GATES + 2 * DN_HEADS
    pad = jnp.zeros((w.shape[0], E_IN_PAD - used), w.dtype)
    return jnp.concatenate([qa, qkv, z, kvc, kvs, kvw, ga, b, a, pad], axis=1).astype(BF16)


def _pack_cmp_w1(w1):
    hid = w1.shape[-1]
    w = w1.reshape(2, 2, D_CMP, HD, hid).transpose(0, 2, 3, 1, 4)
    return w.reshape(2, D_CMP, HD, 2 * hid).astype(BF16)


def _pack_cmp_pe(pe):
    p = pe.reshape(2, 2, D_CMP, HD).transpose(0, 2, 1, 3)
    return jnp.concatenate([p, jnp.zeros((2, D_CMP, 6, HD), pe.dtype)], axis=2)


def _dn_gate_params(a_log, dt_bias):
    row = jnp.zeros((2, LANE), F32).at[0, SM_A:SM_A + DN_HEADS].set(a_log).at[1, SM_A:SM_A + DN_HEADS].set(dt_bias)
    return row, row.T


def _even_layer(xp, xs, w_in, w_out, pe, w1, b1, w2, conv_w, a_log, dt_bias, norm_w,
                cache_cmp, cache_slc, win_buf, conv_buf, dn_s0, page_table, g, b):
    t = xp.shape[0]
    bs = xs.shape[0]
    n_pages = page_table.shape[1]
    past = n_pages * PAGE_ROWS
    slopes = _alibi_slopes()
    w_in_p = _pack_w_in(w_in)
    w4 = _pack_cmp_w1(w1)
    pe8 = _pack_cmp_pe(pe)
    w2b = w2.astype(BF16)
    conv_wt = conv_w.T
    prow, pcol = _dn_gate_params(a_log, dt_bias)
    nw = norm_w.reshape(1, DV)

    hp = _mm(xp, w_in_p)
    hs = _mm(xs, w_in_p)
    kvc_p = hp[:, COL_KVC:COL_KVC + A_KVW]
    kvs_p = hp[:, COL_KVS:COL_KVS + A_KVW]
    kvw_p = hp[:, COL_KVW:COL_KVW + A_KVW]
    kvc_s = hs[:, COL_KVC:COL_KVC + A_KVW]
    kvs_s = hs[:, COL_KVS:COL_KVS + A_KVW]
    kvw_s = hs[:, COL_KVW:COL_KVW + A_KVW]

    TQ = _tile(t, (128, 64, 32, 16, 8))
    nch = t // D_CMP
    ns = t // L_SLC
    nsp = -(-ns // LANE) * LANE
    arange_pt = jnp.arange(t // PAGE_ROWS, dtype=jnp.int32)[None]
    h1 = _cmp_stage1(kvc_p.reshape(t // PAGE_ROWS, PAGE_SUBROWS, HD), arange_pt, w4)
    cmp_p = _cmp_stage2(h1, pe8, w4, b1, w2b)
    q_p = hp
    o_cmp, sel = _nsa_select(q_p[None], cmp_p, _overlap_matrix(nch, nsp), slopes,
                             TQ=_tile(t, (512, 256, 128, 64, 32, 16, 8)), NC=(t - L_CMP) // D_CMP + 1, NS=ns, q_off=0,
                             BB=1)
    kvb = hp[:, COL_KVC:COL_KVC + 3 * A_KVW].astype(BF16)
    o_a_p = _nsa_slc_win_prompt(hp, sel[0], kvb, o_cmp[0], slopes, TQ=TQ, TK=_tile(t, (1024, 512, 256, 128, 64)))

    nch_s = n_pages * CH_PER_PAGE
    nc_s = (past + 1 - L_CMP) // D_CMP + 1
    ns_s = -(-(past + 1) // L_SLC)
    nsp_s = -(-ns_s // LANE) * LANE
    h1s = _cmp_stage1(cache_cmp.reshape(cache_cmp.shape[0], PAGE_SUBROWS, HD), page_table, w4)
    cmp_s = _cmp_stage2(h1s, pe8, w4, b1, w2b)
    hs3 = hs[:, None, :]
    q_s = hs3[:, :, COL_QA:COL_QA + A_Q]
    o_cmp_s, sel_s = _nsa_select(q_s, cmp_s, _overlap_matrix(nch_s, nsp_s), slopes,
                                 TQ=1, NC=nc_s, NS=ns_s, q_off=past, BB=_tile(bs, (8, 4, 2, 1)))
    o_slc_s = _nsa_slc_sample(q_s, sel_s[:, :, 0, :], kvs_s[:, None, :],
                              cache_slc.reshape(cache_slc.shape[0], PAGE_SUBROWS, HD), page_table, slopes)
    wb = win_buf.reshape(bs, win_buf.shape[1] * KV_PARTS, HD)
    o_a_s = _nsa_win_sample(q_s, wb, kvw_s[:, None, :], hs3[:, :, COL_SM:COL_SM + LANE], o_cmp_s, o_slc_s,
                            slopes, past=past)

    qkvn = _dn_conv_prompt(hp, jnp.zeros((8, DN_QKV), F32), conv_wt)
    smt = hp[:, COL_SM:COL_SM + LANE].T
    o_b_p, dn_s_p = _dn_chunk_prompt(qkvn, hp, smt, prow, pcol, nw, C=_tile(t, (128,)))
    o_b_s, dn_s_s = _dn_sample(conv_buf, hs3, conv_wt, prow, nw, dn_s0)

    w_out_b = w_out.astype(BF16)
    yp = _mm_ln([o_a_p, o_b_p], w_out_b, xp, g, b)
    ys = _mm_ln([o_a_s[:, 0], o_b_s[:, 0]], w_out_b, xs, g, b)

    kv6 = lambda a: a.reshape(a.shape[:-1] + (2, NSA_KV, HD))
    raw_p = hp[:, COL_DQKV:COL_DQKV + DN_QKV]
    raw_s = hs[:, COL_DQKV:COL_DQKV + DN_QKV]
    wlen = min(WINDOW, t)
    outs = dict(
        cmp_p=kv6(kvc_p)[None], cmp_s=kv6(kvc_s)[:, None],
        slc_p=kv6(kvs_p)[None], slc_s=kv6(kvs_s)[:, None],
        win_p=kv6(kvw_p[t - wlen:])[None],
        win_s=jnp.concatenate([win_buf, kv6(kvw_s)[:, None]], axis=1)[:, 1:],
        conv_p=jnp.concatenate([jnp.zeros((CONV_W - 1, DN_QKV), F32), raw_p], axis=0)[t:][None],
        conv_s=jnp.concatenate([conv_buf, raw_s[:, None]], axis=1)[:, 1:],
        dns_p=dn_s_p[None], dns_s=dn_s_s,
    )
    return yp, ys, outs


def _odd_layer(xp, xs, shift_s, s0_s, mix, wr, wk, wv, wo, w0, w1, w2, a0, a1, a2, g1, g2, k_k, k_a, r_k,
               ln_w, ln_b, g, b):
    t, d = xp.shape
    bs = xs.shape[0]
    npair = d // LANE
    first_p = jnp.zeros((8, d), F32)

    def padk(wa, wb_):
        r = wa.shape[1]
        rp = -(-r // LANE) * LANE
        return (jnp.pad(wa, ((0, 0), (0, rp - r))).astype(BF16), jnp.pad(wb_, ((0, rp - r), (0, 0))).astype(BF16))

    wrb, wkb, wvb, wob = (w.astype(BF16) for w in (wr, wk, wv, wo))
    w1b, w2b = padk(w1, w2)
    a1b, a2b = padk(a1, a2)
    g1b, g2b = padk(g1, g2)
    prm = jnp.stack([w0, a0, k_k, k_a, r_k.reshape(d), ln_w, ln_b, jnp.zeros((d,), F32)])

    def proj(x, prev_kw, pair_out):
        mm = lambda w_, i, **kw: _mm(x, w_, mixrow=mix[i:i + 1], **prev_kw, **kw)
        r = mm(wrb, 0, pair_out=pair_out)
        wl = _mm(mm(w1b, 1, act="tanh"), w2b, pair_out=pair_out)
        k = mm(wkb, 2, pair_out=pair_out)
        v = mm(wvb, 3, pair_out=pair_out)
        al = _mm(mm(a1b, 4), a2b, pair_out=pair_out)
        gate = _mm(mm(g1b, 5, act="sigmoid"), g2b, pair_out=pair_out)
        return r, k, v, wl, al, gate

    pp = proj(xp, dict(first8=first_p), True)
    prm_pair = prm.reshape(8, npair, LANE).transpose(1, 0, 2)
    y_p, s_p = _rwkv_chunk_prompt(*pp, prm_pair, C=_tile(t, (64, 32, 16, 8)), PP=16)
    yp = _mm_ln([y_p], wob, xp, g, b, pair_in=True)

    ps = [a[:, None, :] for a in proj(xs, dict(xprev=shift_s), False)]
    y_s, s_s = _rwkv_sample(*ps, prm, s0_s)
    ys = _mm_ln([y_s[:, 0]], wob, xs, g, b)
    return yp, ys, dict(shift_p=xp[t - 1:t], shift_s=xs, rs_p=s_p[None], rs_s=s_s)


def kernel(x_prompt, x_sample, cache_nsa_cmp, cache_nsa_slc, cache_nsa_win, state_dn_conv, state_dn_S, state_rwkv_shift, state_rwkv_S, page_table, ln_g, ln_b, ffn_wi, ffn_wo, mix_w_in, mix_w_out, nsa_cmp_pe, nsa_cmp_w1, nsa_cmp_b1, nsa_cmp_w2, dn_conv_w, dn_a_log, dn_dt_bias, dn_norm_w, rwkv_mix, rwkv_wr, rwkv_wk, rwkv_wv, rwkv_wo, rwkv_w0, rwkv_w1, rwkv_w2, rwkv_a0, rwkv_a1, rwkv_a2, rwkv_g1, rwkv_g2, rwkv_k_k, rwkv_k_a, rwkv_r_k, rwkv_ln_w, rwkv_ln_b):
    bp, t, d = x_prompt.shape
    assert bp == 1 and x_sample.shape[1] == 1
    depth = ffn_wi.shape[0]
    xp = x_prompt[0]
    xs = x_sample[:, 0]
    even, odd = [], []
    wi, wo = ffn_wi, ffn_wo
    for l in range(depth):
        gl = lambda i: (ln_g[l, i][None], ln_b[l, i][None])
        xp = _ffn_ln(xp, wi, wo, l, 0, *gl(0))
        xs = _ffn_ln(xs, wi, wo, l, 0, *gl(0))
        if l % 2 == 0:
            e = l // 2
            xp, xs, o = _even_layer(
                xp, xs, mix_w_in[e], mix_w_out[e], nsa_cmp_pe[e], nsa_cmp_w1[e], nsa_cmp_b1[e], nsa_cmp_w2[e],
                dn_conv_w[e], dn_a_log[e], dn_dt_bias[e], dn_norm_w[e], cache_nsa_cmp[e], cache_nsa_slc[e],
                cache_nsa_win[e], state_dn_conv[e], state_dn_S[e], page_table, *gl(1))
            even.append(o)
        else:
            c = l // 2
            xp, xs, o = _odd_layer(
                xp, xs, state_rwkv_shift[c], state_rwkv_S[c], rwkv_mix[c], rwkv_wr[c], rwkv_wk[c], rwkv_wv[c],
                rwkv_wo[c], rwkv_w0[c], rwkv_w1[c], rwkv_w2[c], rwkv_a0[c], rwkv_a1[c], rwkv_a2[c], rwkv_g1[c],
                rwkv_g2[c], rwkv_k_k[c], rwkv_k_a[c], rwkv_r_k[c], rwkv_ln_w[c], rwkv_ln_b[c], *gl(1))
            odd.append(o)
        xp = _ffn_ln(xp, wi, wo, l, 1, *gl(2))
        xs = _ffn_ln(xs, wi, wo, l, 1, *gl(2))
    st = lambda lst, key: jnp.stack([o[key] for o in lst])
    return (xp[None], xs[:, None],
            st(even, "cmp_p"), st(even, "cmp_s"), st(even, "slc_p"), st(even, "slc_s"),
            st(even, "win_p"), st(even, "win_s"), st(even, "conv_p"), st(even, "conv_s"),
            st(even, "dns_p"), st(even, "dns_s"),
            st(odd, "shift_p"), st(odd, "shift_s"), st(odd, "rs_p"), st(odd, "rs_s"))
```
